```python
import jax, jax.numpy as jnp
from jax import lax
import numpy as np

D_MODEL = 1024
BATCH = 8
SEQ = 4096
DEPTH = 1

CTX_LEN = 256
GRID_W = 64

GLA_HEADS = 4
GLA_DK = 128
GLA_DV = 256
GLA_KW = GLA_HEADS * GLA_DK
GLA_VW = GLA_HEADS * GLA_DV
GATE_RANK = 16
GATE_NORM = 16.0
CHUNK = 64
ROPE_BASE = 10000.0

POOL_WINDOWS = (2, 4, 8, 16)
POOL_GROUPS = 4
POOL_W = D_MODEL
POOL_GW = POOL_W // POOL_GROUPS

EPS = 1e-6

IN_SPLITS = (GLA_KW, GLA_KW, GLA_VW, GLA_VW, GATE_RANK, GATE_RANK, POOL_W, POOL_W, D_MODEL, D_MODEL)
D_IN = 2 * GLA_KW + 2 * GLA_VW + 2 * GATE_RANK + 2 * POOL_W + 2 * D_MODEL

kernel_name = "hybrid_gla_pool_prefix_dit"


def rmsnorm(x, g):
    xf = x.astype(jnp.float32)
    y = xf * lax.rsqrt(jnp.mean(xf * xf, axis=-1, keepdims=True) + EPS)
    return (y * g.astype(jnp.float32)).astype(x.dtype)


def rope_2d(t, row, col):
    half = GLA_DK // 2
    nf = half // 2
    freqs = ROPE_BASE ** (-jnp.arange(nf, dtype=jnp.float32) / nf)

    def rot(xp, pos):
        ang = pos[:, None] * freqs
        cos = jnp.cos(ang)[None, :, None, :]
        sin = jnp.sin(ang)[None, :, None, :]
        xa, xb = xp[..., :nf], xp[..., nf:]
        return jnp.concatenate([xa * cos - xb * sin, xa * sin + xb * cos], axis=-1)

    return jnp.concatenate([rot(t[..., :half], row), rot(t[..., half:], col)], axis=-1)


def gla_chunked(q, k, v, log_a, s0, with_output):
    B, L, H, DK = q.shape
    n = L // CHUNK

    def to_chunks(t):
        return t.astype(jnp.float32).reshape(B, n, CHUNK, H, t.shape[-1]).transpose(1, 0, 3, 2, 4)

    tri = jnp.tril(jnp.ones((CHUNK, CHUNK), dtype=bool))

    def step(S, inp):
        qc, kc, vc, ac = inp
        b = jnp.cumsum(ac, axis=2)
        bC = b[:, :, -1:, :]
        S_new = jnp.exp(bC[:, :, 0, :, None]) * S + jnp.einsum(
            'bhjd,bhjv->bhdv', kc * jnp.exp(bC - b), vc)
        if not with_output:
            return S_new, None
        diff = b[:, :, :, None, :] - b[:, :, None, :, :]
        decay = jnp.exp(jnp.where(tri[:, :, None], diff, -jnp.inf))
        scores = jnp.einsum('bhid,bhjd,bhijd->bhij', qc, kc, decay)
        o = jnp.einsum('bhij,bhjv->bhiv', scores, vc) + jnp.einsum(
            'bhid,bhdv->bhiv', qc * jnp.exp(b), S)
        return S_new, o

    S, o = lax.scan(step, s0, (to_chunks(q), to_chunks(k), to_chunks(v), to_chunks(log_a)))
    if not with_output:
        return None, S
    o = o.transpose(1, 0, 3, 2, 4).reshape(B, L, H, v.shape[-1])
    return o, S


def gla_bidir(q, k, v, la_f, la_b, s0_f, s0_b, with_output):
    flip = lambda t: t[:, ::-1]
    o_f, s_f = gla_chunked(q, k, v, la_f, s0_f, with_output)
    o_b, s_b = gla_chunked(flip(q), flip(k), flip(v), flip(la_b), s0_b, with_output)
    o = o_f + flip(o_b) if with_output else None
    return o, s_f, s_b


def gla_log_decay(lowrank, w_up, b):
    B, L, _ = lowrank.shape
    gk = (lowrank @ w_up + b).astype(jnp.float32)
    return (jax.nn.log_sigmoid(gk) / GATE_NORM).reshape(B, L, GLA_HEADS, GLA_DK)


def mixer_inputs(h, w_in, w_up_f, b_f, w_up_b, b_b, row, col):
    B, L, _ = h.shape
    p = h @ w_in
    idx = np.cumsum(IN_SPLITS)[:-1].tolist()
    pq, pk, pv, zA, lr_f, lr_b, uB, zB, gA, gB = jnp.split(p, idx, axis=-1)
    q = pq.reshape(B, L, GLA_HEADS, GLA_DK) * (GLA_DK ** -0.5)
    k = pk.reshape(B, L, GLA_HEADS, GLA_DK)
    if row is not None:
        q = rope_2d(q, row, col)
        k = rope_2d(k, row, col)
    v = pv.reshape(B, L, GLA_HEADS, GLA_DV)
    la_f = gla_log_decay(lr_f, w_up_f, b_f)
    la_b = gla_log_decay(lr_b, w_up_b, b_b)
    return q, k, v, la_f, la_b, zA, uB, zB, gA, gB


def multiscale_pool(u, w_pool, pool_scale):
    B, L, _ = u.shape
    uf = u.astype(jnp.float32)
    cs = jnp.concatenate([jnp.zeros_like(uf[:, :1]), jnp.cumsum(uf, axis=1)], axis=1)
    t = jnp.arange(L)
    parts = []
    for g, w in enumerate(POOL_WINDOWS):
        lo = jnp.clip(t - w // 2, 0, L - 1)
        hi = jnp.clip(t + (w - 1 - w // 2), 0, L - 1)
        sl = slice(g * POOL_GW, (g + 1) * POOL_GW)
        csg = cs[..., sl]
        cnt = (hi - lo + 1).astype(jnp.float32)[:, None]
        parts.append((csg[:, hi + 1] - csg[:, lo]) / cnt - uf[..., sl])
    d = jnp.stack(parts, axis=2)
    y = jnp.einsum('blgc,gcd->blgd', d, w_pool.astype(jnp.float32)).reshape(B, L, POOL_W)
    return (y * pool_scale.astype(jnp.float32)).astype(u.dtype)


def branch_out(o, zA, uB, zB, gA, gB, gla_norm_g, w_pool, pool_scale, w_branch_a, w_branch_b, w_out):
    B, L, _ = zA.shape
    o_n = rmsnorm(o, gla_norm_g).astype(zA.dtype).reshape(B, L, GLA_VW)
    yA = (o_n * jax.nn.silu(zA)) @ w_branch_a
    yB = (multiscale_pool(uB, w_pool, pool_scale) * jax.nn.silu(zB)) @ w_branch_b
    merged = jax.nn.sigmoid(gA) * yA + jax.nn.sigmoid(gB) * yB
    return merged @ w_out


def setup_inputs(seed: int = 0) -> dict:
    key = jax.random.key(seed)
    ks = jax.random.split(key, 24)
    f32 = jnp.float32
    nrm = lambda k, s, sc: jax.random.normal(k, s, f32) * sc
    return {
        "x": nrm(ks[0], (BATCH, SEQ, D_MODEL), 1.0),
        "c": nrm(ks[1], (BATCH, D_MODEL), 1.0),
        "ctx": nrm(ks[2], (BATCH, CTX_LEN, D_MODEL), 1.0),
        "c_ctx": nrm(ks[3], (D_MODEL,), 1.0),
        "w_mod": nrm(ks[4], (DEPTH, D_MODEL, 3 * D_MODEL), D_MODEL ** -0.5),
        "b_mod": nrm(ks[5], (DEPTH, 3 * D_MODEL), 0.01),
        "norm_g": 1.0 + nrm(ks[6], (DEPTH, D_MODEL), 0.1),
        "w_in": nrm(ks[7], (DEPTH, D_MODEL, D_IN), D_MODEL ** -0.5),
        "w_gate_up_f": nrm(ks[8], (DEPTH, GATE_RANK, GLA_KW), GATE_RANK ** -0.5),
        "b_gate_f": jax.random.uniform(ks[9], (DEPTH, GLA_KW), f32, 1.0, 5.0),
        "w_gate_up_b": nrm(ks[10], (DEPTH, GATE_RANK, GLA_KW), GATE_RANK ** -0.5),
        "b_gate_b": jax.random.uniform(ks[11], (DEPTH, GLA_KW), f32, 1.0, 5.0),
        "gla_norm_g": 1.0 + nrm(ks[12], (DEPTH, GLA_DV), 0.1),
        "w_pool": nrm(ks[13], (DEPTH, POOL_GROUPS, POOL_GW, POOL_GW), POOL_GW ** -0.5),
        "pool_scale": 1.0 + nrm(ks[14], (DEPTH, POOL_W), 0.1),
        "w_branch_a": nrm(ks[15], (DEPTH, GLA_VW, D_MODEL), GLA_VW ** -0.5),
        "w_branch_b": nrm(ks[16], (DEPTH, POOL_W, D_MODEL), POOL_W ** -0.5),
        "w_out": nrm(ks[17], (DEPTH, D_MODEL, D_MODEL), D_MODEL ** -0.5),
        "final_norm_g": 1.0 + nrm(ks[18], (D_MODEL,), 0.1),
    }


def reference(x, c, ctx, c_ctx, w_mod, b_mod, norm_g, w_in, w_gate_up_f, b_gate_f,
              w_gate_up_b, b_gate_b, gla_norm_g, w_pool, pool_scale, w_branch_a,
              w_branch_b, w_out, final_norm_g):
    B, L, _ = x.shape
    ROWS = L // GRID_W
    t = jnp.arange(ROWS * GRID_W)
    row = (t // GRID_W).astype(jnp.float32)
    col = (t % GRID_W).astype(jnp.float32)

    for l in range(DEPTH):
        last = l == DEPTH - 1
        shift_x, scale_x, gate_x = jnp.split(jax.nn.silu(c) @ w_mod[l] + b_mod[l], 3, axis=-1)
        shift_x, scale_x, gate_x = shift_x[:, None], scale_x[:, None], gate_x[:, None]
        shift_c, scale_c, gate_c = jnp.split(jax.nn.silu(c_ctx) @ w_mod[l] + b_mod[l], 3, axis=-1)

        hc = rmsnorm(ctx, norm_g[l]) * (1.0 + scale_c) + shift_c
        qc, kc, vc, laf_c, lab_c, zAc, uBc, zBc, gAc, gBc = mixer_inputs(
            hc, w_in[l], w_gate_up_f[l], b_gate_f[l], w_gate_up_b[l], b_gate_b[l], None, None)
        s_zero = jnp.zeros((B, GLA_HEADS, GLA_DK, GLA_DV), jnp.float32)
        oc, s_f, s_b = gla_bidir(qc, kc, vc, laf_c, lab_c, s_zero, s_zero, not last)

        hx = rmsnorm(x, norm_g[l]) * (1.0 + scale_x) + shift_x
        q, k, v, la_f, la_b, zA, uB, zB, gA, gB = mixer_inputs(
            hx, w_in[l], w_gate_up_f[l], b_gate_f[l], w_gate_up_b[l], b_gate_b[l], row, col)
        ox, _, _ = gla_bidir(q, k, v, la_f, la_b, s_f, s_b, True)
        y = branch_out(ox.astype(x.dtype), zA, uB, zB, gA, gB, gla_norm_g[l], w_pool[l],
                       pool_scale[l], w_branch_a[l], w_branch_b[l], w_out[l])
        x_new = x + gate_x * y

        if not last:
            yc = branch_out(oc.astype(ctx.dtype), zAc, uBc, zBc, gAc, gBc, gla_norm_g[l], w_pool[l],
                            pool_scale[l], w_branch_a[l], w_branch_b[l], w_out[l])
            ctx = ctx + gate_c * yc
        x = x_new

    return rmsnorm(x, final_norm_g)
```

```python
import functools

import numpy as np
import jax
import jax.numpy as jnp
from jax import lax
from jax.experimental import pallas as pl
from jax.experimental.pallas import tpu as pltpu

F32 = jnp.float32
BF16 = jnp.bfloat16

D_MODEL = 1024
HEADS = 4
DK = 128
DV = 256
KW = HEADS * DK
VW = HEADS * DV
RANK = 16
GATE_NORM = 16.0
CHUNK = 64
GRID_W = 64
ROPE_BASE = 10000.0
POOL_WINDOWS = (2, 4, 8, 16)
POOL_GW = D_MODEL // len(POOL_WINDOWS)
EPS = 1e-6

LANES = 128
HALO = 16
TOKEN_TILE = 256
SAFE_DECAY = 60.0
VMEM_LIMIT = 56 * 1024 * 1024

_C_Q, _C_K, _C_V, _C_ZA, _C_UB, _C_ZB, _C_GA, _C_GB, _C_END = (
    0, 512, 1024, 2048, 3072, 4096, 5120, 6144, 7168)


def _mod_kernel(c_ref, w_ref, b_ref, o_ref):
    c = c_ref[...]
    s = c * jax.nn.sigmoid(c)
    o_ref[...] = jnp.dot(s, w_ref[...], preferred_element_type=F32,
                         precision=lax.Precision.HIGHEST) + b_ref[...]


def _modulation(c_rows, w_mod, b_mod):
    return pl.pallas_call(
        _mod_kernel,
        out_shape=jax.ShapeDtypeStruct((c_rows.shape[0], w_mod.shape[1]), F32),
        compiler_params=pltpu.CompilerParams(vmem_limit_bytes=VMEM_LIMIT),
    )(c_rows, w_mod, b_mod)


def _proj_kernel(x_ref, shift_ref, scale_ref, g_ref, tab_ref, w_ref, wlr_ref, wup_ref, bup_ref,
                 q_ref, k_ref, v_ref, za_ref, ub_ref, zb_ref, ga_ref, gb_ref,
                 bf_ref, bb_ref, dmax_ref):
    tm = x_ref.shape[1]
    x = x_ref[0]
    ms = jnp.mean(x * x, axis=-1, keepdims=True)
    h = (x * lax.rsqrt(ms + EPS)) * g_ref[...]
    h = h * (1.0 + scale_ref[0]) + shift_ref[0]
    hb = h.astype(BF16)

    def proj(c0, c1):
        return jnp.dot(hb, w_ref[:, c0:c1], preferred_element_type=F32)

    def rope(p, t0, dst_ref):
        cos, sin_up, sin_dn = tab_ref[t0], tab_ref[t0 + 1], tab_ref[t0 + 2]
        for hd in range(HEADS):
            ph = p[:, hd * DK:(hd + 1) * DK]
            out = (ph * cos + pltpu.roll(ph, LANES - 32, 1) * sin_up
                   + pltpu.roll(ph, 32, 1) * sin_dn)
            dst_ref[0, :, hd * DK:(hd + 1) * DK] = out.astype(dst_ref.dtype)

    rope(proj(_C_Q, _C_K), 0, q_ref)
    rope(proj(_C_K, _C_V), 3, k_ref)
    v_ref[0] = proj(_C_V, _C_ZA).astype(v_ref.dtype)
    za_ref[0] = proj(_C_ZA, _C_UB).astype(za_ref.dtype)
    ub_ref[0] = proj(_C_UB, _C_ZB).astype(ub_ref.dtype)
    zb_ref[0] = proj(_C_ZB, _C_GA).astype(zb_ref.dtype)
    ga_ref[0] = proj(_C_GA, _C_GB).astype(ga_ref.dtype)
    gb_ref[0] = proj(_C_GB, _C_END).astype(gb_ref.dtype)

    lr = jnp.dot(hb, wlr_ref[...], preferred_element_type=F32).astype(BF16)
    gk = jnp.dot(lr, wup_ref[...], preferred_element_type=F32) + bup_ref[...]
    la = (jnp.minimum(gk, 0.0) - jnp.log1p(jnp.exp(-jnp.abs(gk)))) * (1.0 / GATE_NORM)

    pos = lax.broadcasted_iota(jnp.int32, (tm, KW), 0) & (CHUNK - 1)
    cf = la[:, :KW]
    cb = la[:, KW:]
    s = 1
    while s < CHUNK:
        cf = cf + jnp.where(pos >= s, pltpu.roll(cf, s, 0), 0.0)
        cb = cb + jnp.where(pos < CHUNK - s, pltpu.roll(cb, tm - s, 0), 0.0)
        s *= 2
    bf_ref[0] = cf
    bb_ref[0] = cb
    worst = jnp.maximum(jnp.max(-cf, axis=0, keepdims=True), jnp.max(-cb, axis=0, keepdims=True))
    dmax_ref[0] = jnp.broadcast_to(jnp.max(worst, axis=1, keepdims=True), dmax_ref.shape[1:])


def _project(x, mod3, mod_row, norm_g, tables, w_main, w_lr, w_up, b_up):
    B, L, _ = x.shape
    tm = TOKEN_TILE
    nt = L // tm
    tok = lambda width: pl.BlockSpec((1, tm, width), lambda b, i: (b, i, 0))
    const2 = lambda arr: pl.BlockSpec(arr.shape, lambda b, i: (0, 0))
    bf = lambda width: jax.ShapeDtypeStruct((B, L, width), BF16)
    return pl.pallas_call(
        _proj_kernel,
        grid=(B, nt),
        in_specs=[
            tok(D_MODEL),
            pl.BlockSpec((1, 1, D_MODEL), lambda b, i: (mod_row(b), 0, 0)),
            pl.BlockSpec((1, 1, D_MODEL), lambda b, i: (mod_row(b), 0, 1)),
            const2(norm_g),
            pl.BlockSpec((6, tm, LANES), lambda b, i: (0, i, 0)),
            const2(w_main), const2(w_lr), const2(w_up), const2(b_up),
        ],
        out_specs=[tok(KW), tok(KW), tok(VW), tok(VW), tok(D_MODEL), tok(D_MODEL),
                   tok(D_MODEL), tok(D_MODEL), tok(KW), tok(KW),
                   pl.BlockSpec((1, 8, LANES), lambda b, i: (b * nt + i, 0, 0))],
        out_shape=[bf(KW), bf(KW), bf(VW), bf(VW), bf(D_MODEL), bf(D_MODEL),
                   bf(D_MODEL), bf(D_MODEL),
                   jax.ShapeDtypeStruct((B, L, KW), F32), jax.ShapeDtypeStruct((B, L, KW), F32),
                   jax.ShapeDtypeStruct((B * nt, 8, LANES), F32)],
        compiler_params=pltpu.CompilerParams(
            dimension_semantics=("parallel", "arbitrary"), vmem_limit_bytes=VMEM_LIMIT),
    )(x, mod3, mod3, norm_g, tables, w_main, w_lr, w_up, b_up)


def _gla_kernel(flags_ref, q_ref, k_ref, v_ref, b_ref, s0_ref, *rest, reverse, mode, nblk):
    if mode == "final":
        oprev_ref, z_ref, g_ref, out_ref, sfin_ref, state, scores, kf32, bf32 = rest
    elif mode == "partial":
        out_ref, sfin_ref, state, scores, kf32, bf32 = rest
    else:
        sfin_ref, state, scores, kf32, bf32 = rest
    bi = pl.program_id(0)
    i = pl.program_id(1)
    tb = q_ref.shape[1]
    nch = tb // CHUNK

    @pl.when(i == 0)
    def _():
        state[...] = s0_ref[0]

    blk = (nblk - 1 - i) if reverse else i
    exact_path = flags_ref[bi * nblk + blk] != 0

    row = lax.broadcasted_iota(jnp.int32, (CHUNK, CHUNK), 0)
    col = lax.broadcasted_iota(jnp.int32, (CHUNK, CHUNK), 1)
    keep = (col >= row) if reverse else (col <= row)
    edge = 0 if reverse else CHUNK - 1

    def chunk_body(ci, carry):
        c = (nch - 1 - ci) if reverse else ci
        r0 = pl.multiple_of(c * CHUNK, CHUNK)
        rows = pl.ds(r0, CHUNK)
        for hd in range(HEADS):
            ks = slice(hd * DK, (hd + 1) * DK)
            vs = slice(hd * DV, (hd + 1) * DV)
            q = q_ref[0, rows, ks].astype(F32)
            k = k_ref[0, rows, ks].astype(F32)
            v = v_ref[0, rows, vs]
            b = b_ref[0, rows, ks]
            kt = k.T
            bt = b.T
            b_edge = bt[:, edge:edge + 1]
            s_old = state[hd]
            k_dec_t = (kt * jnp.exp(b_edge - bt)).astype(BF16)
            state[hd] = jnp.exp(b_edge) * s_old + jnp.dot(k_dec_t, v, preferred_element_type=F32)
            if mode == "state":
                continue
            qh = (q * jnp.exp(b)).astype(BF16)

            @pl.when(jnp.logical_not(exact_path))
            def _():
                kh = (k * jnp.exp(-b)).astype(BF16)
                scores[...] = lax.dot_general(qh, kh, (((1,), (1,)), ((), ())),
                                              preferred_element_type=F32)

            @pl.when(exact_path)
            def _():
                kf32[...] = k
                bf32[...] = b

                def col_body(j, sc):
                    kj = kf32[pl.ds(j, 1), :]
                    bj = bf32[pl.ds(j, 1), :]
                    w = q * kj * jnp.exp(jnp.minimum(b - bj, 0.0))
                    return jnp.where(col == j, jnp.sum(w, axis=1, keepdims=True), sc)

                scores[...] = lax.fori_loop(0, CHUNK, col_body, jnp.zeros((CHUNK, CHUNK), F32))

            sc = jnp.where(keep, scores[...], 0.0).astype(BF16)
            o = (jnp.dot(sc, v, preferred_element_type=F32)
                 + jnp.dot(qh, s_old.astype(BF16), preferred_element_type=F32))
            if mode == "final":
                o = o + oprev_ref[0, rows, vs]
                ms = jnp.mean(o * o, axis=-1, keepdims=True)
                on = o * lax.rsqrt(ms + EPS) * g_ref[...]
                z = z_ref[0, rows, vs].astype(F32)
                out_ref[0, rows, vs] = (on * (z * jax.nn.sigmoid(z))).astype(out_ref.dtype)
            else:
                out_ref[0, rows, vs] = o
        return carry

    lax.fori_loop(0, nch, chunk_body, 0)

    @pl.when(i == nblk - 1)
    def _():
        sfin_ref[0] = state[...]


def _gla_scan(flags, q, k, v, b, s0, *, reverse, mode, oprev=None, z=None, g=None):
    B, L, _ = q.shape
    tb = TOKEN_TILE
    nblk = L // tb
    blk = (lambda i: nblk - 1 - i) if reverse else (lambda i: i)
    tok = lambda width: pl.BlockSpec((1, tb, width), lambda bi, i, fl: (bi, blk(i), 0))
    st_spec = pl.BlockSpec((1, HEADS, DK, DV), lambda bi, i, fl: (bi, 0, 0, 0))
    st_shape = jax.ShapeDtypeStruct((B, HEADS, DK, DV), F32)
    in_specs = [tok(KW), tok(KW), tok(VW), tok(KW), st_spec]
    args = [q, k, v, b, s0]
    out_specs, out_shape = [st_spec], [st_shape]
    if mode == "final":
        in_specs += [tok(VW), tok(VW), pl.BlockSpec(g.shape, lambda bi, i, fl: (0, 0))]
        args += [oprev, z, g]
        out_specs = [tok(VW)] + out_specs
        out_shape = [jax.ShapeDtypeStruct((B, L, VW), BF16)] + out_shape
    elif mode == "partial":
        out_specs = [tok(VW)] + out_specs
        out_shape = [jax.ShapeDtypeStruct((B, L, VW), F32)] + out_shape
    res = pl.pallas_call(
        functools.partial(_gla_kernel, reverse=reverse, mode=mode, nblk=nblk),
        grid_spec=pltpu.PrefetchScalarGridSpec(
            num_scalar_prefetch=1, grid=(B, nblk), in_specs=in_specs, out_specs=out_specs,
            scratch_shapes=[pltpu.VMEM((HEADS, DK, DV), F32), pltpu.VMEM((CHUNK, CHUNK), F32),
                            pltpu.VMEM((CHUNK, DK), F32), pltpu.VMEM((CHUNK, DK), F32)]),
        out_shape=out_shape,
        compiler_params=pltpu.CompilerParams(
            dimension_semantics=("parallel", "arbitrary"), vmem_limit_bytes=VMEM_LIMIT),
    )(flags, *args)
    return res


def _out_kernel(a_ref, up_ref, uc_ref, un_ref, zb_ref, ga_ref, gb_ref, x_ref, gate_ref,
                wpool_ref, pscale_ref, wa_ref, wb_ref, wo_ref, fg_ref, o_ref, *, seq_len):
    i = pl.program_id(1)
    nt = pl.num_programs(1)
    tm = uc_ref.shape[1]
    ext = tm + 2 * HALO

    u_prev = jnp.where(i > 0, up_ref[0].astype(F32), 0.0)
    u_next = jnp.where(i < nt - 1, un_ref[0].astype(F32), 0.0)
    u_cur = uc_ref[0].astype(F32)
    u_ext = jnp.concatenate([u_prev, u_cur, u_next], axis=0)

    t = i * tm + lax.broadcasted_iota(jnp.int32, (tm, POOL_GW), 0)
    pooled = []
    for gi, w in enumerate(POOL_WINDOWS):
        cs = slice(gi * POOL_GW, (gi + 1) * POOL_GW)
        acc = u_ext[:, cs]
        acc = acc + pltpu.roll(acc, 1, 0)
        half = 1
        while 2 * half < w:
            acc = pltpu.roll(acc, half, 0) + pltpu.roll(acc, ext - half, 0)
            half *= 2
        win = acc[HALO:HALO + tm]
        lo = jnp.maximum(t - w // 2, 0)
        hi = jnp.minimum(t + (w - 1 - w // 2), seq_len - 1)
        cnt = (hi - lo + 1).astype(F32)
        d = win / cnt - u_cur[:, cs]
        pooled.append(jnp.dot(d.astype(BF16), wpool_ref[gi], preferred_element_type=F32))
    pooled = jnp.concatenate(pooled, axis=1) * pscale_ref[...]

    zb = zb_ref[0].astype(F32)
    bm = (pooled * (zb * jax.nn.sigmoid(zb))).astype(BF16)
    y_a = jnp.dot(a_ref[0], wa_ref[...], preferred_element_type=F32)
    y_b = jnp.dot(bm, wb_ref[...], preferred_element_type=F32)
    merged = (jax.nn.sigmoid(ga_ref[0].astype(F32)) * y_a
              + jax.nn.sigmoid(gb_ref[0].astype(F32)) * y_b)
    y = jnp.dot(merged.astype(BF16), wo_ref[...], preferred_element_type=F32)
    xn = x_ref[0] + gate_ref[0] * y
    ms = jnp.mean(xn * xn, axis=-1, keepdims=True)
    o_ref[0] = xn * lax.rsqrt(ms + EPS) * fg_ref[...]


def _output_block(a, ub, zb, ga, gb, x, mod3, w_pool, pool_scale, w_a, w_b, w_o, final_g):
    B, L, _ = x.shape
    tm = TOKEN_TILE
    nt = L // tm
    per_halo = tm // HALO
    n_halo = L // HALO
    tok = pl.BlockSpec((1, tm, D_MODEL), lambda b, i: (b, i, 0))
    const = lambda arr: pl.BlockSpec(arr.shape, lambda b, i: (0,) * arr.ndim)
    return pl.pallas_call(
        functools.partial(_out_kernel, seq_len=L),
        grid=(B, nt),
        in_specs=[
            tok,
            pl.BlockSpec((1, HALO, D_MODEL), lambda b, i: (b, jnp.maximum(i * per_halo - 1, 0), 0)),
            tok,
            pl.BlockSpec((1, HALO, D_MODEL),
                         lambda b, i: (b, jnp.minimum((i + 1) * per_halo, n_halo - 1), 0)),
            tok, tok, tok, tok,
            pl.BlockSpec((1, 1, D_MODEL), lambda b, i: (b, 0, 2)),
            const(w_pool), const(pool_scale), const(w_a), const(w_b), const(w_o), const(final_g),
        ],
        out_specs=tok,
        out_shape=jax.ShapeDtypeStruct((B, L, D_MODEL), F32),
        compiler_params=pltpu.CompilerParams(
            dimension_semantics=("parallel", "arbitrary"), vmem_limit_bytes=VMEM_LIMIT),
    )(a, ub, ub, ub, zb, ga, gb, x, mod3, w_pool, pool_scale, w_a, w_b, w_o, final_g)


def _rope_tables(seq_len, rotate):
    scale = DK ** -0.5
    if not rotate:
        one = np.ones((seq_len, LANES), np.float32)
        zero = np.zeros((seq_len, LANES), np.float32)
        return jnp.asarray(np.stack([one * scale, zero, zero, one, zero, zero]))
    nf = DK // 4
    t = jnp.arange(seq_len)
    rowp = (t // GRID_W).astype(F32)
    colp = (t % GRID_W).astype(F32)
    freqs = ROPE_BASE ** (-jnp.arange(nf, dtype=F32) / nf)
    ang = jnp.concatenate([rowp[:, None] * freqs] * 2 + [colp[:, None] * freqs] * 2, axis=1)
    cos, sin = jnp.cos(ang), jnp.sin(ang)
    first = (jnp.arange(LANES) % (2 * nf)) < nf
    sin_up = jnp.where(first, -sin, 0.0)
    sin_dn = jnp.where(first, 0.0, sin)
    return jnp.stack([cos * scale, sin_up * scale, sin_dn * scale, cos, sin_up, sin_dn])


def _flags(dmax):
    return (dmax[:, 0, 0] > SAFE_DECAY).astype(jnp.int32)


def kernel(x, c, ctx, c_ctx, w_mod, b_mod, norm_g, w_in, w_gate_up_f, b_gate_f, w_gate_up_b,
           b_gate_b, gla_norm_g, w_pool, pool_scale, w_branch_a, w_branch_b, w_out, final_norm_g):
    B, L, _ = x.shape
    assert w_mod.shape[0] == 1, "single-layer configuration"
    wi = w_in[0]
    w_main = jnp.concatenate([wi[:, :3072], wi[:, 3104:]], axis=1).astype(BF16)
    w_lr = jnp.pad(wi[:, 3072:3104], ((0, 0), (0, LANES - 2 * RANK))).astype(BF16)
    w_up = jnp.zeros((LANES, 2 * KW), F32)
    w_up = w_up.at[:RANK, :KW].set(w_gate_up_f[0]).at[RANK:2 * RANK, KW:].set(w_gate_up_b[0])
    w_up = w_up.astype(BF16)
    b_up = jnp.concatenate([b_gate_f[0], b_gate_b[0]])[None, :]

    c_rows = jnp.zeros((16, D_MODEL), F32).at[:B].set(c).at[B].set(c_ctx)
    mod = _modulation(c_rows, w_mod[0], b_mod[0][None, :])
    mod3 = mod[:, None, :]

    ng = norm_g[0][None, :]
    tab_x = _rope_tables(L, True)
    tab_c = _rope_tables(ctx.shape[1], False)

    pc = _project(ctx, mod3, lambda b: B, ng, tab_c, w_main, w_lr, w_up, b_up)
    _, kc, vc, _, _, _, _, _, bfc, bbc, dmc = pc
    fl_c = _flags(dmc)
    s_zero = jnp.zeros((B, HEADS, DK, DV), F32)
    (s_f,) = _gla_scan(fl_c, kc, kc, vc, bfc, s_zero, reverse=False, mode="state")
    (s_b,) = _gla_scan(fl_c, kc, kc, vc, bbc, s_zero, reverse=True, mode="state")

    q, k, v, za, ub, zb, ga, gb, bf, bb, dm = _project(
        x, mod3, lambda b: b, ng, tab_x, w_main, w_lr, w_up, b_up)
    fl = _flags(dm)
    o_b, _ = _gla_scan(fl, q, k, v, bb, s_b, reverse=True, mode="partial")
    a, _ = _gla_scan(fl, q, k, v, bf, s_f, reverse=False, mode="final",
                     oprev=o_b, z=za, g=gla_norm_g[0][None, :])

    return _output_block(a, ub, zb, ga, gb, x, mod3, w_pool[0].astype(BF16),
                         pool_scale[0][None, :], w_branch_a[0].astype(BF16),
                         w_branch_b[0].astype(BF16), w_out[0].astype(BF16),
                         final_norm_g[None, :])
```

```python
import functools

import numpy as np
import jax
import jax.numpy as jnp
from jax import lax
from jax.experimental import pallas as pl
from jax.experimental.pallas import tpu as pltpu

F32 = jnp.float32
BF16 = jnp.bfloat16

D_MODEL = 1024
HEADS = 4
DK = 128
DV = 256
KW = HEADS * DK
VW = HEADS * DV
RANK = 16
GATE_NORM = 16.0
CHUNK = 128
GRID_W = 64
ROPE_BASE = 10000.0
POOL_WINDOWS = (2, 4, 8, 16)
POOL_GW = D_MODEL // len(POOL_WINDOWS)
EPS = 1e-6

LANES = 128
SUBLANES = 8
HALO = 16
TOKEN_TILE = 256
SAFE_DECAY = 60.0
VMEM_LIMIT = 56 * 1024 * 1024

_C_Q, _C_K, _C_V, _C_ZA, _C_UB, _C_ZB, _C_GA, _C_GB, _C_END = (
    0, 512, 1024, 2048, 3072, 4096, 5120, 6144, 7168)


def _mod_kernel(c_ref, w_ref, b_ref, o_ref):
    c = c_ref[...]
    s = c * jax.nn.sigmoid(c)
    o_ref[...] = jnp.dot(s, w_ref[...], preferred_element_type=F32,
                         precision=lax.Precision.HIGHEST) + b_ref[...]


def _modulation(c_rows, w_mod, b_mod):
    return pl.pallas_call(
        _mod_kernel,
        out_shape=jax.ShapeDtypeStruct((c_rows.shape[0], w_mod.shape[1]), F32),
        compiler_params=pltpu.CompilerParams(vmem_limit_bytes=VMEM_LIMIT),
    )(c_rows, w_mod, b_mod)


def _proj_kernel(x_ref, shift_ref, scale_ref, g_ref, tab_ref, w_ref, wlr_ref, wup_ref, bup_ref,
                 q_ref, k_ref, v_ref, za_ref, ub_ref, zb_ref, ga_ref, gb_ref,
                 bf_ref, bb_ref, dmax_ref):
    tm = x_ref.shape[1]
    x = x_ref[0]
    ms = jnp.mean(x * x, axis=-1, keepdims=True)
    h = (x * lax.rsqrt(ms + EPS)) * g_ref[...]
    h = h * (1.0 + scale_ref[0]) + shift_ref[0]
    hb = h.astype(BF16)

    def proj(c0, c1):
        return jnp.dot(hb, w_ref[:, c0:c1], preferred_element_type=F32)

    def rope(p, t0, dst_ref):
        cos, sin_up, sin_dn = tab_ref[t0], tab_ref[t0 + 1], tab_ref[t0 + 2]
        for hd in range(HEADS):
            ph = p[:, hd * DK:(hd + 1) * DK]
            out = (ph * cos + pltpu.roll(ph, LANES - 32, 1) * sin_up
                   + pltpu.roll(ph, 32, 1) * sin_dn)
            dst_ref[0, :, hd * DK:(hd + 1) * DK] = out.astype(dst_ref.dtype)

    rope(proj(_C_Q, _C_K), 0, q_ref)
    rope(proj(_C_K, _C_V), 3, k_ref)
    v_ref[0] = proj(_C_V, _C_ZA).astype(v_ref.dtype)
    za_ref[0] = proj(_C_ZA, _C_UB).astype(za_ref.dtype)
    ub_ref[0] = proj(_C_UB, _C_ZB).astype(ub_ref.dtype)
    zb_ref[0] = proj(_C_ZB, _C_GA).astype(zb_ref.dtype)
    ga_ref[0] = proj(_C_GA, _C_GB).astype(ga_ref.dtype)
    gb_ref[0] = proj(_C_GB, _C_END).astype(gb_ref.dtype)

    lr = jnp.dot(hb, wlr_ref[...], preferred_element_type=F32).astype(BF16)
    gk = jnp.dot(lr, wup_ref[...], preferred_element_type=F32) + bup_ref[...]
    la = (jnp.minimum(gk, 0.0) - jnp.log1p(jnp.exp(-jnp.abs(gk)))) * (1.0 / GATE_NORM)

    pos = lax.broadcasted_iota(jnp.int32, (tm, KW), 0) & (CHUNK - 1)
    cf = la[:, :KW]
    cb = la[:, KW:]
    s = 1
    while s < CHUNK:
        cf = cf + jnp.where(pos >= s, pltpu.roll(cf, s, 0), 0.0)
        cb = cb + jnp.where(pos < CHUNK - s, pltpu.roll(cb, tm - s, 0), 0.0)
        s *= 2
    bf_ref[0] = cf
    bb_ref[0] = cb
    worst = jnp.maximum(jnp.max(-cf, axis=0, keepdims=True), jnp.max(-cb, axis=0, keepdims=True))
    dmax_ref[0] = jnp.broadcast_to(jnp.max(worst, axis=1, keepdims=True), dmax_ref.shape[1:])


def _project(x, mod3, mod_row, norm_g, tables, w_main, w_lr, w_up, b_up):
    B, L, _ = x.shape
    tm = TOKEN_TILE
    nt = L // tm
    tok = lambda width: pl.BlockSpec((1, tm, width), lambda b, i: (b, i, 0))
    const2 = lambda arr: pl.BlockSpec(arr.shape, lambda b, i: (0, 0))
    bf = lambda width: jax.ShapeDtypeStruct((B, L, width), BF16)
    return pl.pallas_call(
        _proj_kernel,
        grid=(B, nt),
        in_specs=[
            tok(D_MODEL),
            pl.BlockSpec((1, 1, D_MODEL), lambda b, i: (mod_row(b), 0, 0)),
            pl.BlockSpec((1, 1, D_MODEL), lambda b, i: (mod_row(b), 0, 1)),
            const2(norm_g),
            pl.BlockSpec((6, tm, LANES), lambda b, i: (0, i, 0)),
            const2(w_main), const2(w_lr), const2(w_up), const2(b_up),
        ],
        out_specs=[tok(KW), tok(KW), tok(VW), tok(VW), tok(D_MODEL), tok(D_MODEL),
                   tok(D_MODEL), tok(D_MODEL), tok(KW), tok(KW),
                   pl.BlockSpec((1, SUBLANES, LANES), lambda b, i: (b * nt + i, 0, 0))],
        out_shape=[bf(KW), bf(KW), bf(VW), bf(VW), bf(D_MODEL), bf(D_MODEL),
                   bf(D_MODEL), bf(D_MODEL),
                   jax.ShapeDtypeStruct((B, L, KW), F32), jax.ShapeDtypeStruct((B, L, KW), F32),
                   jax.ShapeDtypeStruct((B * nt, SUBLANES, LANES), F32)],
        compiler_params=pltpu.CompilerParams(
            dimension_semantics=("parallel", "arbitrary"), vmem_limit_bytes=VMEM_LIMIT),
    )(x, mod3, mod3, norm_g, tables, w_main, w_lr, w_up, b_up)


def _as_column(row):
    return jnp.broadcast_to(row, (SUBLANES, DK)).T[:, 0:1]


def _state_step(state, hd, k, b, v, edge):
    b_edge = b[edge:edge + 1, :]
    kd_t = (k * jnp.exp(b_edge - b)).T.astype(BF16)
    state[hd] = (jnp.exp(_as_column(b_edge)) * state[hd]
                 + jnp.dot(kd_t, v, preferred_element_type=F32))


def _state_kernel(k_ref, v_ref, b_ref, s0_ref, *rest, reverse, emit, nblk):
    if emit:
        sb_ref, sfin_ref, state = rest
    else:
        sfin_ref, state = rest
    i = pl.program_id(1)
    nch = k_ref.shape[1] // CHUNK

    @pl.when(i == 0)
    def _():
        state[...] = s0_ref[0]

    edge = 0 if reverse else CHUNK - 1
    for ci in range(nch):
        c = (nch - 1 - ci) if reverse else ci
        rows = slice(c * CHUNK, (c + 1) * CHUNK)
        for hd in range(HEADS):
            ks = slice(hd * DK, (hd + 1) * DK)
            vs = slice(hd * DV, (hd + 1) * DV)
            if emit:
                sb_ref[0, c, ks, :] = state[hd].astype(BF16)
            _state_step(state, hd, k_ref[0, rows, ks].astype(F32), b_ref[0, rows, ks],
                        v_ref[0, rows, vs], edge)

    @pl.when(i == nblk - 1)
    def _():
        sfin_ref[0] = state[...]


def _state_scan(k, v, b, s0, *, reverse, emit):
    B, L, _ = k.shape
    tb = TOKEN_TILE
    nblk = L // tb
    nch = tb // CHUNK
    blk = (lambda i: nblk - 1 - i) if reverse else (lambda i: i)
    tok = lambda width: pl.BlockSpec((1, tb, width), lambda bi, i: (bi, blk(i), 0))
    st_spec = pl.BlockSpec((1, HEADS, DK, DV), lambda bi, i: (bi, 0, 0, 0))
    out_specs = [st_spec]
    out_shape = [jax.ShapeDtypeStruct((B, HEADS, DK, DV), F32)]
    if emit:
        out_specs = [pl.BlockSpec((1, nch, KW, DV), lambda bi, i: (bi, blk(i), 0, 0))] + out_specs
        out_shape = [jax.ShapeDtypeStruct((B, L // CHUNK, KW, DV), BF16)] + out_shape
    return pl.pallas_call(
        functools.partial(_state_kernel, reverse=reverse, emit=emit, nblk=nblk),
        grid=(B, nblk),
        in_specs=[tok(KW), tok(VW), tok(KW), st_spec],
        out_specs=out_specs, out_shape=out_shape,
        scratch_shapes=[pltpu.VMEM((HEADS, DK, DV), F32)],
        compiler_params=pltpu.CompilerParams(
            dimension_semantics=("parallel", "arbitrary"), vmem_limit_bytes=VMEM_LIMIT),
    )(k, v, b, s0)


def _gla_kernel(flags_ref, q_ref, k_ref, v_ref, bf_ref, bb_ref, sb_ref, s0_ref, z_ref, g_ref,
                out_ref, state, k_rows, bf_rows, bb_rows, *, nblk):
    bi = pl.program_id(0)
    i = pl.program_id(1)
    nch = q_ref.shape[1] // CHUNK

    @pl.when(i == 0)
    def _():
        state[...] = s0_ref[0]

    exact_path = flags_ref[bi * nblk + i] != 0
    row = lax.broadcasted_iota(jnp.int32, (CHUNK, CHUNK), 0)
    col = lax.broadcasted_iota(jnp.int32, (CHUNK, CHUNK), 1)
    nt_dims = (((1,), (1,)), ((), ()))

    def head_chunk(c, hd, exact):
        rows = slice(c * CHUNK, (c + 1) * CHUNK)
        ks = slice(hd * DK, (hd + 1) * DK)
        vs = slice(hd * DV, (hd + 1) * DV)
        q = q_ref[0, rows, ks].astype(F32)
        k = k_ref[0, rows, ks].astype(F32)
        v = v_ref[0, rows, vs]
        bf = bf_ref[0, rows, ks]
        bb = bb_ref[0, rows, ks]
        qf = (q * jnp.exp(bf)).astype(BF16)
        qb = (q * jnp.exp(bb)).astype(BF16)
        if not exact:
            kf = (k * jnp.exp(-bf)).astype(BF16)
            kb = (k * jnp.exp(-bb)).astype(BF16)
            sc = (jnp.where(col <= row, lax.dot_general(qf, kf, nt_dims,
                                                        preferred_element_type=F32), 0.0)
                  + jnp.where(col >= row, lax.dot_general(qb, kb, nt_dims,
                                                          preferred_element_type=F32), 0.0))
        else:
            k_rows[...] = k
            bf_rows[...] = bf
            bb_rows[...] = bb
            tok = lax.broadcasted_iota(jnp.int32, (CHUNK, DK), 0)

            def col_body(j, acc):
                dec = (jnp.where(tok >= j, jnp.exp(jnp.minimum(bf - bf_rows[pl.ds(j, 1), :], 0.0)), 0.0)
                       + jnp.where(tok <= j, jnp.exp(jnp.minimum(bb - bb_rows[pl.ds(j, 1), :], 0.0)), 0.0))
                w = q * k_rows[pl.ds(j, 1), :] * dec
                return jnp.where(col == j, jnp.sum(w, axis=1, keepdims=True), acc)

            sc = lax.fori_loop(0, CHUNK, col_body, jnp.zeros((CHUNK, CHUNK), F32))
        s_cat = jnp.concatenate([state[hd].astype(BF16), sb_ref[0, c, ks, :]], axis=0)
        q_cat = jnp.concatenate([qf, qb], axis=1)
        o = (jnp.dot(sc.astype(BF16), v, preferred_element_type=F32)
             + jnp.dot(q_cat, s_cat, preferred_element_type=F32))
        _state_step(state, hd, k, bf, v, CHUNK - 1)
        ms = jnp.mean(o * o, axis=-1, keepdims=True)
        on = o * lax.rsqrt(ms + EPS) * g_ref[...]
        z = z_ref[0, rows, vs].astype(F32)
        out_ref[0, rows, vs] = (on * (z * jax.nn.sigmoid(z))).astype(out_ref.dtype)

    for exact in (False, True):
        @pl.when(exact_path == exact)
        def _():
            for c in range(nch):
                for hd in range(HEADS):
                    head_chunk(c, hd, exact)


def _gla_fused(flags, q, k, v, bf, bb, sb, s0, z, g):
    B, L, _ = q.shape
    tb = TOKEN_TILE
    nblk = L // tb
    nch = tb // CHUNK
    tok = lambda width: pl.BlockSpec((1, tb, width), lambda bi, i, fl: (bi, i, 0))
    return pl.pallas_call(
        functools.partial(_gla_kernel, nblk=nblk),
        grid_spec=pltpu.PrefetchScalarGridSpec(
            num_scalar_prefetch=1, grid=(B, nblk),
            in_specs=[tok(KW), tok(KW), tok(VW), tok(KW), tok(KW),
                      pl.BlockSpec((1, nch, KW, DV), lambda bi, i, fl: (bi, i, 0, 0)),
                      pl.BlockSpec((1, HEADS, DK, DV), lambda bi, i, fl: (bi, 0, 0, 0)),
                      tok(VW), pl.BlockSpec(g.shape, lambda bi, i, fl: (0, 0))],
            out_specs=tok(VW),
            scratch_shapes=[pltpu.VMEM((HEADS, DK, DV), F32)]
                           + [pltpu.VMEM((CHUNK, DK), F32)] * 3),
        out_shape=jax.ShapeDtypeStruct((B, L, VW), BF16),
        compiler_params=pltpu.CompilerParams(
            dimension_semantics=("parallel", "arbitrary"), vmem_limit_bytes=VMEM_LIMIT),
    )(flags, q, k, v, bf, bb, sb, s0, z, g)


def _out_kernel(a_ref, up_ref, uc_ref, un_ref, zb_ref, ga_ref, gb_ref, x_ref, gate_ref,
                wpool_ref, pscale_ref, wa_ref, wb_ref, wo_ref, fg_ref, o_ref, *, seq_len):
    i = pl.program_id(1)
    nt = pl.num_programs(1)
    tm = uc_ref.shape[1]
    ext = tm + 2 * HALO

    u_prev = jnp.where(i > 0, up_ref[0].astype(F32), 0.0)
    u_next = jnp.where(i < nt - 1, un_ref[0].astype(F32), 0.0)
    u_cur = uc_ref[0].astype(F32)
    u_ext = jnp.concatenate([u_prev, u_cur, u_next], axis=0)

    t = i * tm + lax.broadcasted_iota(jnp.int32, (tm, POOL_GW), 0)
    pooled = []
    for gi, w in enumerate(POOL_WINDOWS):
        cs = slice(gi * POOL_GW, (gi + 1) * POOL_GW)
        acc = u_ext[:, cs]
        acc = acc + pltpu.roll(acc, 1, 0)
        half = 1
        while 2 * half < w:
            acc = pltpu.roll(acc, half, 0) + pltpu.roll(acc, ext - half, 0)
            half *= 2
        win = acc[HALO:HALO + tm]
        lo = jnp.maximum(t - w // 2, 0)
        hi = jnp.minimum(t + (w - 1 - w // 2), seq_len - 1)
        cnt = (hi - lo + 1).astype(F32)
        d = win / cnt - u_cur[:, cs]
        pooled.append(jnp.dot(d.astype(BF16), wpool_ref[gi], preferred_element_type=F32))
    pooled = jnp.concatenate(pooled, axis=1) * pscale_ref[...]

    zb = zb_ref[0].astype(F32)
    bm = (pooled * (zb * jax.nn.sigmoid(zb))).astype(BF16)
    y_a = jnp.dot(a_ref[0], wa_ref[...], preferred_element_type=F32)
    y_b = jnp.dot(bm, wb_ref[...], preferred_element_type=F32)
    merged = (jax.nn.sigmoid(ga_ref[0].astype(F32)) * y_a
              + jax.nn.sigmoid(gb_ref[0].astype(F32)) * y_b)
    y = jnp.dot(merged.astype(BF16), wo_ref[...], preferred_element_type=F32)
    xn = x_ref[0] + gate_ref[0] * y
    ms = jnp.mean(xn * xn, axis=-1, keepdims=True)
    o_ref[0] = xn * lax.rsqrt(ms + EPS) * fg_ref[...]


def _output_block(a, ub, zb, ga, gb, x, mod3, w_pool, pool_scale, w_a, w_b, w_o, final_g):
    B, L, _ = x.shape
    tm = TOKEN_TILE
    nt = L // tm
    per_halo = tm // HALO
    n_halo = L // HALO
    tok = pl.BlockSpec((1, tm, D_MODEL), lambda b, i: (b, i, 0))
    const = lambda arr: pl.BlockSpec(arr.shape, lambda b, i: (0,) * arr.ndim)
    return pl.pallas_call(
        functools.partial(_out_kernel, seq_len=L),
        grid=(B, nt),
        in_specs=[
            tok,
            pl.BlockSpec((1, HALO, D_MODEL), lambda b, i: (b, jnp.maximum(i * per_halo - 1, 0), 0)),
            tok,
            pl.BlockSpec((1, HALO, D_MODEL),
                         lambda b, i: (b, jnp.minimum((i + 1) * per_halo, n_halo - 1), 0)),
            tok, tok, tok, tok,
            pl.BlockSpec((1, 1, D_MODEL), lambda b, i: (b, 0, 2)),
            const(w_pool), const(pool_scale), const(w_a), const(w_b), const(w_o), const(final_g),
        ],
        out_specs=tok,
        out_shape=jax.ShapeDtypeStruct((B, L, D_MODEL), F32),
        compiler_params=pltpu.CompilerParams(
            dimension_semantics=("parallel", "arbitrary"), vmem_limit_bytes=VMEM_LIMIT),
    )(a, ub, ub, ub, zb, ga, gb, x, mod3, w_pool, pool_scale, w_a, w_b, w_o, final_g)


def _rope_tables(seq_len, rotate):
    scale = DK ** -0.5
    if not rotate:
        one = np.ones((seq_len, LANES), np.float32)
        zero = np.zeros((seq_len, LANES), np.float32)
        return jnp.asarray(np.stack([one * scale, zero, zero, one, zero, zero]))
    nf = DK // 4
    t = jnp.arange(seq_len)
    rowp = (t // GRID_W).astype(F32)
    colp = (t % GRID_W).astype(F32)
    freqs = ROPE_BASE ** (-jnp.arange(nf, dtype=F32) / nf)
    ang = jnp.concatenate([rowp[:, None] * freqs] * 2 + [colp[:, None] * freqs] * 2, axis=1)
    cos, sin = jnp.cos(ang), jnp.sin(ang)
    first = (jnp.arange(LANES) % (2 * nf)) < nf
    sin_up = jnp.where(first, -sin, 0.0)
    sin_dn = jnp.where(first, 0.0, sin)
    return jnp.stack([cos * scale, sin_up * scale, sin_dn * scale, cos, sin_up, sin_dn])


def _flags(dmax):
    return (dmax[:, 0, 0] > SAFE_DECAY).astype(jnp.int32)


def kernel(x, c, ctx, c_ctx, w_mod, b_mod, norm_g, w_in, w_gate_up_f, b_gate_f, w_gate_up_b,
           b_gate_b, gla_norm_g, w_pool, pool_scale, w_branch_a, w_branch_b, w_out, final_norm_g):
    B, L, _ = x.shape
    assert w_mod.shape[0] == 1, "single-layer configuration"
    wi = w_in[0]
    w_main = jnp.concatenate([wi[:, :3072], wi[:, 3104:]], axis=1).astype(BF16)
    w_lr = jnp.pad(wi[:, 3072:3104], ((0, 0), (0, LANES - 2 * RANK))).astype(BF16)
    w_up = jnp.zeros((LANES, 2 * KW), F32)
    w_up = w_up.at[:RANK, :KW].set(w_gate_up_f[0]).at[RANK:2 * RANK, KW:].set(w_gate_up_b[0])
    w_up = w_up.astype(BF16)
    b_up = jnp.concatenate([b_gate_f[0], b_gate_b[0]])[None, :]

    c_rows = jnp.zeros((16, D_MODEL), F32).at[:B].set(c).at[B].set(c_ctx)
    mod = _modulation(c_rows, w_mod[0], b_mod[0][None, :])
    mod3 = mod[:, None, :]

    ng = norm_g[0][None, :]
    tab_x = _rope_tables(L, True)
    tab_c = _rope_tables(ctx.shape[1], False)

    pc = _project(ctx, mod3, lambda b: B, ng, tab_c, w_main, w_lr, w_up, b_up)
    _, kc, vc, _, _, _, _, _, bfc, bbc, _ = pc
    s_zero = jnp.zeros((B, HEADS, DK, DV), F32)
    (s_f,) = _state_scan(kc, vc, bfc, s_zero, reverse=False, emit=False)
    (s_b,) = _state_scan(kc, vc, bbc, s_zero, reverse=True, emit=False)

    q, k, v, za, ub, zb, ga, gb, bf, bb, dm = _project(
        x, mod3, lambda b: b, ng, tab_x, w_main, w_lr, w_up, b_up)
    sb_states, _ = _state_scan(k, v, bb, s_b, reverse=True, emit=True)
    a = _gla_fused(_flags(dm), q, k, v, bf, bb, sb_states, s_f, za, gla_norm_g[0][None, :])

    return _output_block(a, ub, zb, ga, gb, x, mod3, w_pool[0].astype(BF16),
                         pool_scale[0][None, :], w_branch_a[0].astype(BF16),
                         w_branch_b[0].astype(BF16), w_out[0].astype(BF16),
                         final_norm_g[None, :])
```

```python
import functools

import numpy as np
import jax
import jax.numpy as jnp
from jax import lax
from jax.experimental import pallas as pl
from jax.experimental.pallas import tpu as pltpu

F32 = jnp.float32
BF16 = jnp.bfloat16

D_MODEL = 1024
HEADS = 4
DK = 128
DV = 256
KW = HEADS * DK
VW = HEADS * DV
RANK = 16
GATE_NORM = 16.0
CHUNK = 128
GRID_W = 64
ROPE_BASE = 10000.0
POOL_WINDOWS = (2, 4, 8, 16)
POOL_GW = D_MODEL // len(POOL_WINDOWS)
EPS = 1e-6

LANES = 128
SUBLANES = 8
HALO = 16
TOKEN_TILE = 256
PROJ_TILE = 512
OUT_TILE = 512
SUB_TILE = 256
SAFE_DECAY = 60.0
VMEM_LIMIT = 56 * 1024 * 1024

_C_Q, _C_K, _C_V, _C_ZA, _C_UB, _C_ZB, _C_GA, _C_GB, _C_END = (
    0, 512, 1024, 2048, 3072, 4096, 5120, 6144, 7168)


def _sigmoid(x):
    return 0.5 * jnp.tanh(0.5 * x) + 0.5


def _silu(x):
    h = 0.5 * x
    return h + h * jnp.tanh(h)


def _mod_kernel(c_ref, w_ref, b_ref, o_ref):
    c = c_ref[...]
    s = c * jax.nn.sigmoid(c)
    o_ref[...] = jnp.dot(s, w_ref[...], preferred_element_type=F32,
                         precision=lax.Precision.HIGHEST) + b_ref[...]


def _modulation(c_rows, w_mod, b_mod):
    return pl.pallas_call(
        _mod_kernel,
        out_shape=jax.ShapeDtypeStruct((c_rows.shape[0], w_mod.shape[1]), F32),
        compiler_params=pltpu.CompilerParams(vmem_limit_bytes=VMEM_LIMIT),
    )(c_rows, w_mod, b_mod)


def _proj_kernel(x_ref, shift_ref, scale_ref, g_ref, tab_ref, w_ref, wlr_ref, wup_ref, bup_ref,
                 *outs, kv_only):
    if kv_only:
        k_ref, v_ref, bf_ref, bb_ref, dmax_ref = outs
    else:
        (q_ref, k_ref, v_ref, za_ref, ub_ref, zb_ref, ga_ref, gb_ref,
         bf_ref, bb_ref, dmax_ref) = outs
    tm = x_ref.shape[1]
    sub = min(tm, SUB_TILE)
    worst = []

    ri = lax.broadcasted_iota(jnp.int32, (sub, sub), 0)
    ci = lax.broadcasted_iota(jnp.int32, (sub, sub), 1)
    same_chunk = (ri // CHUNK) == (ci // CHUNK)
    tri_f = jnp.where(same_chunk & (ci <= ri), 1.0, 0.0).astype(BF16)
    tri_b = jnp.where(same_chunk & (ci >= ri), 1.0, 0.0).astype(BF16)
    tri_f = jnp.concatenate([tri_f] * 3, axis=1)
    tri_b = jnp.concatenate([tri_b] * 3, axis=1)

    def stages(r0):
        rows = slice(r0, r0 + sub)
        st = {}

        def prologue():
            x = x_ref[0, rows, :]
            ms = jnp.mean(x * x, axis=-1, keepdims=True)
            h = (x * lax.rsqrt(ms + EPS)) * g_ref[...]
            h = h * (1.0 + scale_ref[0]) + shift_ref[0]
            st["hb"] = h.astype(BF16)
            lr = jnp.dot(st["hb"], wlr_ref[...], preferred_element_type=F32).astype(BF16)
            gk = jnp.dot(lr, wup_ref[...], preferred_element_type=F32) + bup_ref[...]
            la = (jnp.minimum(gk, 0.0) - jnp.log(1.0 + jnp.exp(-jnp.abs(gk)))) * (1.0 / GATE_NORM)
            hi = la.astype(BF16)
            r1 = la - hi.astype(F32)
            mid = r1.astype(BF16)
            lo = (r1 - mid.astype(F32)).astype(BF16)
            pieces = jnp.concatenate([hi, mid, lo], axis=0)
            cf = jnp.dot(tri_f, pieces[:, :KW], preferred_element_type=F32)
            cb = jnp.dot(tri_b, pieces[:, KW:], preferred_element_type=F32)
            bf_ref[0, rows, :] = cf
            bb_ref[0, rows, :] = cb
            for c0 in range(0, sub, CHUNK):
                worst.append(jnp.maximum(-cf[c0 + CHUNK - 1:c0 + CHUNK, :], -cb[c0:c0 + 1, :]))

        def plain(c0, dst_ref, act=None):
            def run():
                p = jnp.dot(st["hb"], w_ref[:, c0:c0 + dst_ref.shape[2]],
                            preferred_element_type=F32)
                dst_ref[0, rows, :] = (p if act is None else act(p)).astype(dst_ref.dtype)
            return run

        def rotary(c0, t0, dst_ref):
            def run():
                p = jnp.dot(st["hb"], w_ref[:, c0:c0 + KW], preferred_element_type=F32)
                cos, sin_up, sin_dn = (tab_ref[t0 + n, rows, :] for n in range(3))
                for hd in range(HEADS):
                    ph = p[:, hd * DK:(hd + 1) * DK]
                    out = (ph * cos + pltpu.roll(ph, LANES - 32, 1) * sin_up
                           + pltpu.roll(ph, 32, 1) * sin_dn)
                    dst_ref[0, rows, hd * DK:(hd + 1) * DK] = out.astype(dst_ref.dtype)
            return run

        matmuls = [rotary(_C_K, 3, k_ref), plain(_C_V, v_ref)]
        if not kv_only:
            matmuls += [rotary(_C_Q, 0, q_ref), plain(_C_ZA, za_ref, _silu),
                        plain(_C_UB, ub_ref), plain(_C_ZB, zb_ref, _silu),
                        plain(_C_GA, ga_ref, _sigmoid), plain(_C_GB, gb_ref, _sigmoid)]
        return prologue, matmuls

    tiles = [stages(r0) for r0 in range(0, tm, sub)]
    tiles[0][0]()
    for n, (_, matmuls) in enumerate(tiles):
        _interleave(matmuls, [tiles[n + 1][0]] if n + 1 < len(tiles) else [])
    dmax_ref[0] = jnp.broadcast_to(
        jnp.max(functools.reduce(jnp.maximum, worst), axis=1, keepdims=True), dmax_ref.shape[1:])


def _interleave(*lists):
    order = sorted(((i + 0.5) / len(lst), which, i)
                   for which, lst in enumerate(lists) for i in range(len(lst)))
    for _, which, idx in order:
        lists[which][idx]()


def _resident(arr):
    return pl.BlockSpec(arr.shape, lambda *_: (0,) * arr.ndim, pipeline_mode=pl.Buffered(1))


def _project(x, mod3, mod_row, norm_g, tables, w_main, w_lr, w_up, b_up, *, kv_only):
    B, L, _ = x.shape
    tm = min(L, PROJ_TILE)
    nt = L // tm
    tok = lambda width: pl.BlockSpec((1, tm, width), lambda b, i: (b, i, 0))
    bf = lambda width: jax.ShapeDtypeStruct((B, L, width), BF16)
    if kv_only:
        widths = [KW, VW]
    else:
        widths = [KW, KW, VW, VW, D_MODEL, D_MODEL, D_MODEL, D_MODEL]
    return pl.pallas_call(
        functools.partial(_proj_kernel, kv_only=kv_only),
        grid=(B, nt),
        in_specs=[
            tok(D_MODEL),
            pl.BlockSpec((1, 1, D_MODEL), lambda b, i: (mod_row(b), 0, 0)),
            pl.BlockSpec((1, 1, D_MODEL), lambda b, i: (mod_row(b), 0, 1)),
            _resident(norm_g),
            pl.BlockSpec((6, tm, LANES), lambda b, i: (0, i, 0)),
            _resident(w_main), _resident(w_lr), _resident(w_up), _resident(b_up),
        ],
        out_specs=[tok(w) for w in widths] + [
            tok(KW), tok(KW),
            pl.BlockSpec((1, SUBLANES, LANES), lambda b, i: (b * nt + i, 0, 0))],
        out_shape=[bf(w) for w in widths] + [
            jax.ShapeDtypeStruct((B, L, KW), F32), jax.ShapeDtypeStruct((B, L, KW), F32),
            jax.ShapeDtypeStruct((B * nt, SUBLANES, LANES), F32)],
        compiler_params=pltpu.CompilerParams(
            dimension_semantics=("parallel", "arbitrary"), vmem_limit_bytes=VMEM_LIMIT),
    )(x, mod3, mod3, norm_g, tables, w_main, w_lr, w_up, b_up)


def _as_column(row):
    return jnp.broadcast_to(row, (SUBLANES, DK)).T[:, 0:1]


def _state_step(state, hd, k, b, v, edge):
    b_edge = b[edge:edge + 1, :]
    kd_t = (k * jnp.exp(b_edge - b)).T.astype(BF16)
    state[hd] = (jnp.exp(_as_column(b_edge)) * state[hd]
                 + jnp.dot(kd_t, v, preferred_element_type=F32))


def _state_kernel(k_ref, v_ref, b_ref, s0_ref, *rest, reverse, emit, nblk):
    if emit:
        sb_ref, sfin_ref, state = rest
    else:
        sfin_ref, state = rest
    i = pl.program_id(1)
    nch = k_ref.shape[1] // CHUNK

    @pl.when(i == 0)
    def _():
        state[...] = s0_ref[0]

    edge = 0 if reverse else CHUNK - 1
    for ci in range(nch):
        c = (nch - 1 - ci) if reverse else ci
        rows = slice(c * CHUNK, (c + 1) * CHUNK)
        for hd in range(HEADS):
            ks = slice(hd * DK, (hd + 1) * DK)
            vs = slice(hd * DV, (hd + 1) * DV)
            if emit:
                sb_ref[0, c, ks, :] = state[hd].astype(BF16)
            _state_step(state, hd, k_ref[0, rows, ks].astype(F32), b_ref[0, rows, ks],
                        v_ref[0, rows, vs], edge)

    @pl.when(i == nblk - 1)
    def _():
        sfin_ref[0] = state[...]


def _state_scan(k, v, b, s0, *, reverse, emit):
    B, L, _ = k.shape
    tb = TOKEN_TILE
    nblk = L // tb
    nch = tb // CHUNK
    blk = (lambda i: nblk - 1 - i) if reverse else (lambda i: i)
    tok = lambda width: pl.BlockSpec((1, tb, width), lambda bi, i: (bi, blk(i), 0))
    st_spec = pl.BlockSpec((1, HEADS, DK, DV), lambda bi, i: (bi, 0, 0, 0))
    out_specs = [st_spec]
    out_shape = [jax.ShapeDtypeStruct((B, HEADS, DK, DV), F32)]
    if emit:
        out_specs = [pl.BlockSpec((1, nch, KW, DV), lambda bi, i: (bi, blk(i), 0, 0))] + out_specs
        out_shape = [jax.ShapeDtypeStruct((B, L // CHUNK, KW, DV), BF16)] + out_shape
    return pl.pallas_call(
        functools.partial(_state_kernel, reverse=reverse, emit=emit, nblk=nblk),
        grid=(B, nblk),
        in_specs=[tok(KW), tok(VW), tok(KW), st_spec],
        out_specs=out_specs, out_shape=out_shape,
        scratch_shapes=[pltpu.VMEM((HEADS, DK, DV), F32)],
        compiler_params=pltpu.CompilerParams(
            dimension_semantics=("parallel", "arbitrary"), vmem_limit_bytes=VMEM_LIMIT),
    )(k, v, b, s0)


def _gla_kernel(flags_ref, q_ref, k_ref, v_ref, bf_ref, bb_ref, sb_ref, s0_ref, z_ref, g_ref,
                out_ref, state, k_rows, bf_rows, bb_rows, *, nblk, blk_per_flag):
    bi = pl.program_id(0)
    i = pl.program_id(1)
    nch = q_ref.shape[1] // CHUNK

    @pl.when(i == 0)
    def _():
        state[...] = s0_ref[0]

    exact_path = flags_ref[bi * (nblk // blk_per_flag) + i // blk_per_flag] != 0
    row = lax.broadcasted_iota(jnp.int32, (CHUNK, CHUNK), 0)
    col = lax.broadcasted_iota(jnp.int32, (CHUNK, CHUNK), 1)
    nt_dims = (((1,), (1,)), ((), ()))

    def head_chunk(c, hd, exact):
        rows = slice(c * CHUNK, (c + 1) * CHUNK)
        ks = slice(hd * DK, (hd + 1) * DK)
        vs = slice(hd * DV, (hd + 1) * DV)
        q = q_ref[0, rows, ks].astype(F32)
        k = k_ref[0, rows, ks].astype(F32)
        v = v_ref[0, rows, vs]
        bf = bf_ref[0, rows, ks]
        bb = bb_ref[0, rows, ks]
        qf = (q * jnp.exp(bf)).astype(BF16)
        qb = (q * jnp.exp(bb)).astype(BF16)
        if not exact:
            kf = (k * jnp.exp(-bf)).astype(BF16)
            kb = (k * jnp.exp(-bb)).astype(BF16)
            sc = (jnp.where(col <= row, lax.dot_general(qf, kf, nt_dims,
                                                        preferred_element_type=F32), 0.0)
                  + jnp.where(col >= row, lax.dot_general(qb, kb, nt_dims,
                                                          preferred_element_type=F32), 0.0))
        else:
            k_rows[...] = k
            bf_rows[...] = bf
            bb_rows[...] = bb
            tok = lax.broadcasted_iota(jnp.int32, (CHUNK, DK), 0)

            def col_body(j, acc):
                dec = (jnp.where(tok >= j, jnp.exp(jnp.minimum(bf - bf_rows[pl.ds(j, 1), :], 0.0)), 0.0)
                       + jnp.where(tok <= j, jnp.exp(jnp.minimum(bb - bb_rows[pl.ds(j, 1), :], 0.0)), 0.0))
                w = q * k_rows[pl.ds(j, 1), :] * dec
                return jnp.where(col == j, jnp.sum(w, axis=1, keepdims=True), acc)

            sc = lax.fori_loop(0, CHUNK, col_body, jnp.zeros((CHUNK, CHUNK), F32))
        s_cat = jnp.concatenate([state[hd].astype(BF16), sb_ref[0, c, ks, :]], axis=0)
        q_cat = jnp.concatenate([qf, qb], axis=1)
        o = (jnp.dot(sc.astype(BF16), v, preferred_element_type=F32)
             + jnp.dot(q_cat, s_cat, preferred_element_type=F32))
        _state_step(state, hd, k, bf, v, CHUNK - 1)
        ms = jnp.mean(o * o, axis=-1, keepdims=True)
        on = o * lax.rsqrt(ms + EPS) * g_ref[...]
        out_ref[0, rows, vs] = (on * z_ref[0, rows, vs].astype(F32)).astype(out_ref.dtype)

    for exact in (False, True):
        @pl.when(exact_path == exact)
        def _():
            for c in range(nch):
                for hd in range(HEADS):
                    head_chunk(c, hd, exact)


def _gla_fused(flags, q, k, v, bf, bb, sb, s0, z, g):
    B, L, _ = q.shape
    tb = TOKEN_TILE
    nblk = L // tb
    nch = tb // CHUNK
    tok = lambda width: pl.BlockSpec((1, tb, width), lambda bi, i, fl: (bi, i, 0))
    return pl.pallas_call(
        functools.partial(_gla_kernel, nblk=nblk, blk_per_flag=nblk * B // flags.shape[0]),
        grid_spec=pltpu.PrefetchScalarGridSpec(
            num_scalar_prefetch=1, grid=(B, nblk),
            in_specs=[tok(KW), tok(KW), tok(VW), tok(KW), tok(KW),
                      pl.BlockSpec((1, nch, KW, DV), lambda bi, i, fl: (bi, i, 0, 0)),
                      pl.BlockSpec((1, HEADS, DK, DV), lambda bi, i, fl: (bi, 0, 0, 0)),
                      tok(VW), pl.BlockSpec(g.shape, lambda bi, i, fl: (0, 0))],
            out_specs=tok(VW),
            scratch_shapes=[pltpu.VMEM((HEADS, DK, DV), F32)]
                           + [pltpu.VMEM((CHUNK, DK), F32)] * 3),
        out_shape=jax.ShapeDtypeStruct((B, L, VW), BF16),
        compiler_params=pltpu.CompilerParams(
            dimension_semantics=("parallel", "arbitrary"), vmem_limit_bytes=VMEM_LIMIT),
    )(flags, q, k, v, bf, bb, sb, s0, z, g)


def _out_kernel(a_ref, up_ref, uc_ref, un_ref, zb_ref, ga_ref, gb_ref, x_ref, gate_ref,
                wpool_ref, pscale_ref, wa_ref, wb_ref, wo_ref, fg_ref, o_ref, *, seq_len):
    i = pl.program_id(1)
    nt = pl.num_programs(1)
    tm = uc_ref.shape[1]
    sub = min(tm, SUB_TILE)
    ext = sub + 2 * HALO
    cw = POOL_GW
    ncol = D_MODEL // cw

    def stages(r0):
        rows = slice(r0, r0 + sub)
        st = {"bm": [None] * ncol, "ya": [None] * ncol, "mg": [None] * ncol, "ss": []}

        def pool_input(cs):
            if r0 == 0:
                before = jnp.where(i > 0, up_ref[0, :, cs].astype(F32), 0.0)
            else:
                before = uc_ref[0, r0 - HALO:r0, cs].astype(F32)
            if r0 + sub == tm:
                after = jnp.where(i < nt - 1, un_ref[0, :, cs].astype(F32), 0.0)
            else:
                after = uc_ref[0, r0 + sub:r0 + sub + HALO, cs].astype(F32)
            return jnp.concatenate([before, uc_ref[0, rows, cs].astype(F32), after], axis=0)

        def pool(gi):
            def run():
                w = POOL_WINDOWS[gi]
                cs = slice(gi * cw, (gi + 1) * cw)
                u_ext = pool_input(cs)
                acc = u_ext + pltpu.roll(u_ext, 1, 0)
                half = 1
                while 2 * half < w:
                    acc = pltpu.roll(acc, half, 0) + pltpu.roll(acc, ext - half, 0)
                    half *= 2
                t = i * tm + r0 + lax.broadcasted_iota(jnp.int32, (sub, LANES), 0)
                lo = jnp.maximum(t - w // 2, 0)
                hi = jnp.minimum(t + (w - 1 - w // 2), seq_len - 1)
                inv_cnt = 1.0 / (hi - lo + 1).astype(F32)
                inv_cnt = jnp.concatenate([inv_cnt] * (cw // LANES), axis=1)
                d = acc[HALO:HALO + sub] * inv_cnt - u_ext[HALO:HALO + sub]
                pooled = jnp.dot(d.astype(BF16), wpool_ref[gi], preferred_element_type=F32)
                st["bm"][gi] = (pooled * pscale_ref[:, cs]
                                * zb_ref[0, rows, cs].astype(F32)).astype(BF16)
            return run

        def branch_a(c):
            def run():
                cs = slice(c * cw, (c + 1) * cw)
                y_a = jnp.dot(a_ref[0, rows, :], wa_ref[:, cs], preferred_element_type=F32)
                st["ya"][c] = ga_ref[0, rows, cs].astype(F32) * y_a
            return run

        def branch_b(c):
            def run():
                cs = slice(c * cw, (c + 1) * cw)
                if c == 0:
                    st["bm"] = jnp.concatenate(st["bm"], axis=1)
                y_b = jnp.dot(st["bm"], wb_ref[:, cs], preferred_element_type=F32)
                st["mg"][c] = (st["ya"][c] + gb_ref[0, rows, cs].astype(F32) * y_b).astype(BF16)
            return run

        def output(c):
            def run():
                cs = slice(c * cw, (c + 1) * cw)
                if c == 0:
                    st["mg"] = jnp.concatenate(st["mg"], axis=1)
                y = jnp.dot(st["mg"], wo_ref[:, cs], preferred_element_type=F32)
                xn = x_ref[0, rows, cs] + gate_ref[0, :, cs] * y
                o_ref[0, rows, cs] = xn
                st["ss"].append(jnp.sum(xn * xn, axis=-1, keepdims=True))
            return run

        def normalise(c):
            def run():
                cs = slice(c * cw, (c + 1) * cw)
                ms = functools.reduce(jnp.add, st["ss"]) * (1.0 / D_MODEL)
                o_ref[0, rows, cs] = o_ref[0, rows, cs] * lax.rsqrt(ms + EPS) * fg_ref[:, cs]
            return run

        first = [f(c) for c in range(ncol) for f in (branch_a, pool)]
        return [first] + [[f(c) for c in range(ncol)] for f in (branch_b, output, normalise)]

    tiles = [stages(r0) for r0 in range(0, tm, sub)]
    nstage = len(tiles[0])
    for step in range(len(tiles) + nstage - 1):
        _interleave(*[tl[step - n] for n, tl in enumerate(tiles) if 0 <= step - n < nstage])


def _output_block(a, ub, zb, ga, gb, x, mod3, w_pool, pool_scale, w_a, w_b, w_o, final_g):
    B, L, _ = x.shape
    tm = OUT_TILE
    nt = L // tm
    per_halo = tm // HALO
    n_halo = L // HALO
    tok = pl.BlockSpec((1, tm, D_MODEL), lambda b, i: (b, i, 0))
    const = _resident
    return pl.pallas_call(
        functools.partial(_out_kernel, seq_len=L),
        grid=(B, nt),
        in_specs=[
            tok,
            pl.BlockSpec((1, HALO, D_MODEL), lambda b, i: (b, jnp.maximum(i * per_halo - 1, 0), 0)),
            tok,
            pl.BlockSpec((1, HALO, D_MODEL),
                         lambda b, i: (b, jnp.minimum((i + 1) * per_halo, n_halo - 1), 0)),
            tok, tok, tok, tok,
            pl.BlockSpec((1, 1, D_MODEL), lambda b, i: (b, 0, 2)),
            const(w_pool), const(pool_scale), const(w_a), const(w_b), const(w_o), const(final_g),
        ],
        out_specs=tok,
        out_shape=jax.ShapeDtypeStruct((B, L, D_MODEL), F32),
        compiler_params=pltpu.CompilerParams(
            dimension_semantics=("parallel", "arbitrary"), vmem_limit_bytes=VMEM_LIMIT),
    )(a, ub, ub, ub, zb, ga, gb, x, mod3, w_pool, pool_scale, w_a, w_b, w_o, final_g)


def _rope_tables(seq_len, rotate):
    scale = DK ** -0.5
    if not rotate:
        one = np.ones((seq_len, LANES), np.float32)
        zero = np.zeros((seq_len, LANES), np.float32)
        return jnp.asarray(np.stack([one * scale, zero, zero, one, zero, zero]))
    nf = DK // 4
    t = jnp.arange(seq_len)
    rowp = (t // GRID_W).astype(F32)
    colp = (t % GRID_W).astype(F32)
    freqs = ROPE_BASE ** (-jnp.arange(nf, dtype=F32) / nf)
    ang = jnp.concatenate([rowp[:, None] * freqs] * 2 + [colp[:, None] * freqs] * 2, axis=1)
    cos, sin = jnp.cos(ang), jnp.sin(ang)
    first = (jnp.arange(LANES) % (2 * nf)) < nf
    sin_up = jnp.where(first, -sin, 0.0)
    sin_dn = jnp.where(first, 0.0, sin)
    return jnp.stack([cos * scale, sin_up * scale, sin_dn * scale, cos, sin_up, sin_dn])


def _flags(dmax):
    return (dmax[:, 0, 0] > SAFE_DECAY).astype(jnp.int32)


def kernel(x, c, ctx, c_ctx, w_mod, b_mod, norm_g, w_in, w_gate_up_f, b_gate_f, w_gate_up_b,
           b_gate_b, gla_norm_g, w_pool, pool_scale, w_branch_a, w_branch_b, w_out, final_norm_g):
    B, L, _ = x.shape
    assert w_mod.shape[0] == 1, "single-layer configuration"
    wi = w_in[0]
    w_main = jnp.concatenate([wi[:, :3072], wi[:, 3104:]], axis=1).astype(BF16)
    w_lr = jnp.pad(wi[:, 3072:3104], ((0, 0), (0, LANES - 2 * RANK))).astype(BF16)
    w_up = jnp.zeros((LANES, 2 * KW), F32)
    w_up = w_up.at[:RANK, :KW].set(w_gate_up_f[0]).at[RANK:2 * RANK, KW:].set(w_gate_up_b[0])
    w_up = w_up.astype(BF16)
    b_up = jnp.concatenate([b_gate_f[0], b_gate_b[0]])[None, :]

    c_rows = jnp.zeros((16, D_MODEL), F32).at[:B].set(c).at[B].set(c_ctx)
    mod = _modulation(c_rows, w_mod[0], b_mod[0][None, :])
    mod3 = mod[:, None, :]

    ng = norm_g[0][None, :]
    tab_x = _rope_tables(L, True)
    tab_c = _rope_tables(ctx.shape[1], False)

    kc, vc, bfc, bbc, _ = _project(ctx, mod3, lambda b: B, ng, tab_c, w_main, w_lr, w_up, b_up,
                                   kv_only=True)
    s_zero = jnp.zeros((B, HEADS, DK, DV), F32)
    (s_f,) = _state_scan(kc, vc, bfc, s_zero, reverse=False, emit=False)
    (s_b,) = _state_scan(kc, vc, bbc, s_zero, reverse=True, emit=False)

    q, k, v, za, ub, zb, ga, gb, bf, bb, dm = _project(
        x, mod3, lambda b: b, ng, tab_x, w_main, w_lr, w_up, b_up, kv_only=False)
    sb_states, _ = _state_scan(k, v, bb, s_b, reverse=True, emit=True)
    a = _gla_fused(_flags(dm), q, k, v, bf, bb, sb_states, s_f, za, gla_norm_g[0][None, :])

    return _output_block(a, ub, zb, ga, gb, x, mod3, w_pool[0].astype(BF16),
                         pool_scale[0][None, :], w_branch_a[0].astype(BF16),
                         w_branch_b[0].astype(BF16), w_out[0].astype(BF16),
                         final_norm_g[None, :])
```

```python
import functools

import numpy as np
import jax
import jax.numpy as jnp
from jax import lax
from jax.experimental import pallas as pl
from jax.experimental.pallas import tpu as pltpu

F32 = jnp.float32
BF16 = jnp.bfloat16

D_MODEL = 1024
HEADS = 4
DK = 128
DV = 256
KW = HEADS * DK
VW = HEADS * DV
RANK = 16
GATE_NORM = 16.0
CHUNK = 128
GRID_W = 64
ROPE_BASE = 10000.0
POOL_WINDOWS = (2, 4, 8, 16)
POOL_GW = D_MODEL // len(POOL_WINDOWS)
EPS = 1e-6

LANES = 128
SUBLANES = 8
HALO = 16
GLA_TILE = 512
STATE_TILE = 1024
PROJ_TILE = 512
OUT_TILE = 512
SUB_TILE = 256
SAFE_DECAY = 60.0
VMEM_LIMIT = 56 * 1024 * 1024

_C_Q, _C_K, _C_V, _C_ZA = 0, 512, 1024, 2048
_C_UB, _C_ZB, _C_GA, _C_GB = 0, 1024, 2048, 3072
W_IN_FIRST = 3072


def _sigmoid(x):
    return 0.5 * jnp.tanh(0.5 * x) + 0.5


def _silu(x):
    h = 0.5 * x
    return h + h * jnp.tanh(h)


def _mod_kernel(c_ref, w_ref, b_ref, o_ref):
    c = c_ref[...]
    s = c * jax.nn.sigmoid(c)
    o_ref[...] = jnp.dot(s, w_ref[...], preferred_element_type=F32,
                         precision=lax.Precision.HIGHEST) + b_ref[...]


def _modulation(c_rows, w_mod, b_mod):
    return pl.pallas_call(
        _mod_kernel,
        out_shape=jax.ShapeDtypeStruct((c_rows.shape[0], w_mod.shape[1]), F32),
        compiler_params=pltpu.CompilerParams(vmem_limit_bytes=VMEM_LIMIT),
    )(c_rows, w_mod, b_mod)


def _proj_kernel(x_ref, shift_ref, scale_ref, g_ref, *rest, context):
    if context:
        w1_ref, wlr_ref, wup_ref, bup_ref, k_ref, v_ref, bf_ref, bb_ref, dmax_ref = rest
    else:
        (tab_ref, w1_ref, w2_ref, wlr_ref, wup_ref, bup_ref,
         q_ref, k_ref, v_ref, za_ref, ub_ref, zb_ref, ga_ref, gb_ref,
         bf_ref, bb_ref, dmax_ref) = rest
    tm = x_ref.shape[1]
    sub = min(tm, SUB_TILE)
    worst = []

    ri = lax.broadcasted_iota(jnp.int32, (sub, sub), 0)
    ci = lax.broadcasted_iota(jnp.int32, (sub, sub), 1)
    same_chunk = (ri // CHUNK) == (ci // CHUNK)
    tri_f = jnp.where(same_chunk & (ci <= ri), 1.0, 0.0).astype(BF16)
    tri_b = jnp.where(same_chunk & (ci >= ri), 1.0, 0.0).astype(BF16)
    tri_f = jnp.concatenate([tri_f] * 3, axis=1)
    tri_b = jnp.concatenate([tri_b] * 3, axis=1)

    def stages(r0):
        rows = slice(r0, r0 + sub)
        st = {}

        def prologue():
            x = x_ref[0, rows, :]
            ms = jnp.mean(x * x, axis=-1, keepdims=True)
            h = (x * lax.rsqrt(ms + EPS)) * g_ref[...]
            h = h * (1.0 + scale_ref[0]) + shift_ref[0]
            st["hb"] = h.astype(BF16)
            lr = jnp.dot(st["hb"], wlr_ref[...], preferred_element_type=F32).astype(BF16)
            gk = jnp.dot(lr, wup_ref[...], preferred_element_type=F32) + bup_ref[...]
            la = (jnp.minimum(gk, 0.0) - jnp.log(1.0 + jnp.exp(-jnp.abs(gk)))) * (1.0 / GATE_NORM)
            hi = la.astype(BF16)
            r1 = la - hi.astype(F32)
            mid = r1.astype(BF16)
            lo = (r1 - mid.astype(F32)).astype(BF16)
            pieces = jnp.concatenate([hi, mid, lo], axis=0)
            cf = jnp.dot(tri_f, pieces[:, :KW], preferred_element_type=F32)
            cb = jnp.dot(tri_b, pieces[:, KW:], preferred_element_type=F32)
            bf_ref[0, rows, :] = cf
            bb_ref[0, rows, :] = cb
            for c0 in range(0, sub, CHUNK):
                worst.append(jnp.maximum(-cf[c0 + CHUNK - 1:c0 + CHUNK, :], -cb[c0:c0 + 1, :]))

        def plain(w_ref, c0, dst_ref, act=None):
            def run():
                p = jnp.dot(st["hb"], w_ref[:, c0:c0 + dst_ref.shape[2]],
                            preferred_element_type=F32)
                dst_ref[0, rows, :] = (p if act is None else act(p)).astype(dst_ref.dtype)
            return run

        def rotary(c0, dst_ref, scale):
            def run():
                p = jnp.dot(st["hb"], w1_ref[:, c0:c0 + KW], preferred_element_type=F32)
                cos, sin_up, sin_dn = (tab_ref[n, rows, :] for n in range(3))
                for hd in range(HEADS):
                    ph = p[:, hd * DK:(hd + 1) * DK]
                    out = (ph * cos + pltpu.roll(ph, LANES - 32, 1) * sin_up
                           + pltpu.roll(ph, 32, 1) * sin_dn)
                    if scale is not None:
                        out = out * scale
                    dst_ref[0, rows, hd * DK:(hd + 1) * DK] = out.astype(dst_ref.dtype)
            return run

        if context:
            return prologue, [plain(w1_ref, _C_K, k_ref), plain(w1_ref, _C_V, v_ref)]
        return prologue, [
            rotary(_C_K, k_ref, None), plain(w1_ref, _C_V, v_ref), rotary(_C_Q, q_ref, DK ** -0.5),
            plain(w1_ref, _C_ZA, za_ref, _silu), plain(w2_ref, _C_UB, ub_ref),
            plain(w2_ref, _C_ZB, zb_ref, _silu), plain(w2_ref, _C_GA, ga_ref, _sigmoid),
            plain(w2_ref, _C_GB, gb_ref, _sigmoid)]

    tiles = [stages(r0) for r0 in range(0, tm, sub)]
    tiles[0][0]()
    for n, (_, matmuls) in enumerate(tiles):
        _interleave(matmuls, [tiles[n + 1][0]] if n + 1 < len(tiles) else [])
    dmax_ref[0] = jnp.broadcast_to(
        jnp.max(functools.reduce(jnp.maximum, worst), axis=1, keepdims=True), dmax_ref.shape[1:])


def _interleave(*lists):
    order = sorted(((i + 0.5) / len(lst), which, i)
                   for which, lst in enumerate(lists) for i in range(len(lst)))
    for _, which, idx in order:
        lists[which][idx]()


def _resident(arr):
    return pl.BlockSpec(arr.shape, lambda *_: (0,) * arr.ndim, pipeline_mode=pl.Buffered(1))


def _project(x, mod3, mod_row, norm_g, tables, weights, *, context):
    B, L, _ = x.shape
    tm = min(L, PROJ_TILE)
    nt = L // tm
    tok = lambda width: pl.BlockSpec((1, tm, width), lambda b, i: (b, i, 0))
    bf = lambda width: jax.ShapeDtypeStruct((B, L, width), BF16)
    if context:
        widths = [KW, VW]
        extra = [weights[0]] + list(weights[2:])
        extra_specs = [_resident(w) for w in extra]
    else:
        widths = [KW, KW, VW, VW, D_MODEL, D_MODEL, D_MODEL, D_MODEL]
        extra = [tables] + list(weights)
        extra_specs = ([pl.BlockSpec((3, tm, LANES), lambda b, i: (0, i, 0))]
                       + [_resident(w) for w in weights])
    return pl.pallas_call(
        functools.partial(_proj_kernel, context=context),
        grid=(B, nt),
        in_specs=[
            tok(D_MODEL),
            pl.BlockSpec((1, 1, D_MODEL), lambda b, i: (mod_row(b), 0, 0)),
            pl.BlockSpec((1, 1, D_MODEL), lambda b, i: (mod_row(b), 0, 1)),
            _resident(norm_g),
        ] + extra_specs,
        out_specs=[tok(w) for w in widths] + [
            tok(KW), tok(KW),
            pl.BlockSpec((1, SUBLANES, LANES), lambda b, i: (b * nt + i, 0, 0))],
        out_shape=[bf(w) for w in widths] + [
            jax.ShapeDtypeStruct((B, L, KW), F32), jax.ShapeDtypeStruct((B, L, KW), F32),
            jax.ShapeDtypeStruct((B * nt, SUBLANES, LANES), F32)],
        compiler_params=pltpu.CompilerParams(
            dimension_semantics=("parallel", "arbitrary"), vmem_limit_bytes=VMEM_LIMIT),
    )(x, mod3, mod3, norm_g, *extra)


def _as_column(row):
    return jnp.broadcast_to(row, (SUBLANES, DK)).T[:, 0:1]


def _state_step(state, hd, k, b, v, edge):
    b_edge = b[edge:edge + 1, :]
    kd_t = (k * jnp.exp(b_edge - b)).T.astype(BF16)
    state[hd] = (jnp.exp(_as_column(b_edge)) * state[hd]
                 + jnp.dot(kd_t, v, preferred_element_type=F32))


def _state_kernel(k_ref, v_ref, b_ref, s0_ref, *rest, reverse, emit, nblk):
    if emit:
        sb_ref, sfin_ref, state = rest
    else:
        sfin_ref, state = rest
    i = pl.program_id(1)
    nch = k_ref.shape[1] // CHUNK

    @pl.when(i == 0)
    def _():
        state[...] = s0_ref[0]

    edge = 0 if reverse else CHUNK - 1
    for ci in range(nch):
        c = (nch - 1 - ci) if reverse else ci
        rows = slice(c * CHUNK, (c + 1) * CHUNK)
        for hd in range(HEADS):
            ks = slice(hd * DK, (hd + 1) * DK)
            vs = slice(hd * DV, (hd + 1) * DV)
            if emit:
                sb_ref[0, c, ks, :] = state[hd].astype(BF16)
            _state_step(state, hd, k_ref[0, rows, ks].astype(F32), b_ref[0, rows, ks],
                        v_ref[0, rows, vs], edge)

    @pl.when(i == nblk - 1)
    def _():
        sfin_ref[0] = state[...]


def _state_scan(k, v, b, s0, *, reverse, emit):
    B, L, _ = k.shape
    tb = min(L, STATE_TILE)
    nblk = L // tb
    nch = tb // CHUNK
    blk = (lambda i: nblk - 1 - i) if reverse else (lambda i: i)
    tok = lambda width: pl.BlockSpec((1, tb, width), lambda bi, i: (bi, blk(i), 0))
    st_spec = pl.BlockSpec((1, HEADS, DK, DV), lambda bi, i: (bi, 0, 0, 0))
    out_specs = [st_spec]
    out_shape = [jax.ShapeDtypeStruct((B, HEADS, DK, DV), F32)]
    if emit:
        out_specs = [pl.BlockSpec((1, nch, KW, DV), lambda bi, i: (bi, blk(i), 0, 0))] + out_specs
        out_shape = [jax.ShapeDtypeStruct((B, L // CHUNK, KW, DV), BF16)] + out_shape
    return pl.pallas_call(
        functools.partial(_state_kernel, reverse=reverse, emit=emit, nblk=nblk),
        grid=(B, nblk),
        in_specs=[tok(KW), tok(VW), tok(KW), st_spec],
        out_specs=out_specs, out_shape=out_shape,
        scratch_shapes=[pltpu.VMEM((HEADS, DK, DV), F32)],
        compiler_params=pltpu.CompilerParams(
            dimension_semantics=("parallel", "arbitrary"), vmem_limit_bytes=VMEM_LIMIT),
    )(k, v, b, s0)


def _gla_kernel(flags_ref, q_ref, k_ref, v_ref, bf_ref, bb_ref, sb_ref, s0_ref, z_ref, g_ref,
                out_ref, state, k_rows, bf_rows, bb_rows, *, nblk, blk_per_flag):
    bi = pl.program_id(0)
    i = pl.program_id(1)
    nch = q_ref.shape[1] // CHUNK

    @pl.when(i == 0)
    def _():
        state[...] = s0_ref[0]

    exact_path = flags_ref[bi * (nblk // blk_per_flag) + i // blk_per_flag] != 0
    row = lax.broadcasted_iota(jnp.int32, (CHUNK, CHUNK), 0)
    col = lax.broadcasted_iota(jnp.int32, (CHUNK, CHUNK), 1)
    nt_dims = (((1,), (1,)), ((), ()))

    def head_chunk(c, hd, exact):
        rows = slice(c * CHUNK, (c + 1) * CHUNK)
        ks = slice(hd * DK, (hd + 1) * DK)
        vs = slice(hd * DV, (hd + 1) * DV)
        q = q_ref[0, rows, ks].astype(F32)
        k = k_ref[0, rows, ks].astype(F32)
        v = v_ref[0, rows, vs]
        bf = bf_ref[0, rows, ks]
        bb = bb_ref[0, rows, ks]
        qf = (q * jnp.exp(bf)).astype(BF16)
        qb = (q * jnp.exp(bb)).astype(BF16)
        if not exact:
            kf = (k * jnp.exp(-bf)).astype(BF16)
            kb = (k * jnp.exp(-bb)).astype(BF16)
            sc = (jnp.where(col <= row, lax.dot_general(qf, kf, nt_dims,
                                                        preferred_element_type=F32), 0.0)
                  + jnp.where(col >= row, lax.dot_general(qb, kb, nt_dims,
                                                          preferred_element_type=F32), 0.0))
        else:
            k_rows[...] = k
            bf_rows[...] = bf
            bb_rows[...] = bb
            tok = lax.broadcasted_iota(jnp.int32, (CHUNK, DK), 0)

            def col_body(j, acc):
                dec = (jnp.where(tok >= j, jnp.exp(jnp.minimum(bf - bf_rows[pl.ds(j, 1), :], 0.0)), 0.0)
                       + jnp.where(tok <= j, jnp.exp(jnp.minimum(bb - bb_rows[pl.ds(j, 1), :], 0.0)), 0.0))
                w = q * k_rows[pl.ds(j, 1), :] * dec
                return jnp.where(col == j, jnp.sum(w, axis=1, keepdims=True), acc)

            sc = lax.fori_loop(0, CHUNK, col_body, jnp.zeros((CHUNK, CHUNK), F32))
        s_cat = jnp.concatenate([state[hd].astype(BF16), sb_ref[0, c, ks, :]], axis=0)
        q_cat = jnp.concatenate([qf, qb], axis=1)
        o = (jnp.dot(sc.astype(BF16), v, preferred_element_type=F32)
             + jnp.dot(q_cat, s_cat, preferred_element_type=F32))
        _state_step(state, hd, k, bf, v, CHUNK - 1)
        ms = jnp.mean(o * o, axis=-1, keepdims=True)
        on = o * lax.rsqrt(ms + EPS) * g_ref[...]
        out_ref[0, rows, vs] = (on * z_ref[0, rows, vs].astype(F32)).astype(out_ref.dtype)

    for exact in (False, True):
        @pl.when(exact_path == exact)
        def _():
            for c in range(nch):
                for hd in range(HEADS):
                    head_chunk(c, hd, exact)


def _gla_fused(flags, q, k, v, bf, bb, sb, s0, z, g):
    B, L, _ = q.shape
    tb = GLA_TILE
    nblk = L // tb
    nch = tb // CHUNK
    tok = lambda width: pl.BlockSpec((1, tb, width), lambda bi, i, fl: (bi, i, 0))
    return pl.pallas_call(
        functools.partial(_gla_kernel, nblk=nblk, blk_per_flag=nblk * B // flags.shape[0]),
        grid_spec=pltpu.PrefetchScalarGridSpec(
            num_scalar_prefetch=1, grid=(B, nblk),
            in_specs=[tok(KW), tok(KW), tok(VW), tok(KW), tok(KW),
                      pl.BlockSpec((1, nch, KW, DV), lambda bi, i, fl: (bi, i, 0, 0)),
                      pl.BlockSpec((1, HEADS, DK, DV), lambda bi, i, fl: (bi, 0, 0, 0)),
                      tok(VW), pl.BlockSpec(g.shape, lambda bi, i, fl: (0, 0))],
            out_specs=tok(VW),
            scratch_shapes=[pltpu.VMEM((HEADS, DK, DV), F32)]
                           + [pltpu.VMEM((CHUNK, DK), F32)] * 3),
        out_shape=jax.ShapeDtypeStruct((B, L, VW), BF16),
        compiler_params=pltpu.CompilerParams(
            dimension_semantics=("parallel", "arbitrary"), vmem_limit_bytes=VMEM_LIMIT),
    )(flags, q, k, v, bf, bb, sb, s0, z, g)


def _out_kernel(a_ref, up_ref, uc_ref, un_ref, zb_ref, ga_ref, gb_ref, x_ref, gate_ref,
                wpool_ref, pscale_ref, wa_ref, wb_ref, wo_ref, fg_ref, o_ref, *, seq_len):
    i = pl.program_id(1)
    nt = pl.num_programs(1)
    tm = uc_ref.shape[1]
    sub = min(tm, SUB_TILE)
    ext = sub + 2 * HALO
    cw = POOL_GW
    ncol = D_MODEL // cw

    def stages(r0):
        rows = slice(r0, r0 + sub)
        st = {"bm": [None] * ncol, "ya": [None] * ncol, "mg": [None] * ncol, "ss": []}

        def pool_input(cs):
            if r0 == 0:
                before = jnp.where(i > 0, up_ref[0, :, cs].astype(F32), 0.0)
            else:
                before = uc_ref[0, r0 - HALO:r0, cs].astype(F32)
            if r0 + sub == tm:
                after = jnp.where(i < nt - 1, un_ref[0, :, cs].astype(F32), 0.0)
            else:
                after = uc_ref[0, r0 + sub:r0 + sub + HALO, cs].astype(F32)
            return jnp.concatenate([before, uc_ref[0, rows, cs].astype(F32), after], axis=0)

        def pool(gi):
            def run():
                w = POOL_WINDOWS[gi]
                cs = slice(gi * cw, (gi + 1) * cw)
                u_ext = pool_input(cs)
                acc = u_ext + pltpu.roll(u_ext, 1, 0)
                half = 1
                while 2 * half < w:
                    acc = pltpu.roll(acc, half, 0) + pltpu.roll(acc, ext - half, 0)
                    half *= 2
                t = i * tm + r0 + lax.broadcasted_iota(jnp.int32, (sub, LANES), 0)
                lo = jnp.maximum(t - w // 2, 0)
                hi = jnp.minimum(t + (w - 1 - w // 2), seq_len - 1)
                inv_cnt = 1.0 / (hi - lo + 1).astype(F32)
                inv_cnt = jnp.concatenate([inv_cnt] * (cw // LANES), axis=1)
                d = acc[HALO:HALO + sub] * inv_cnt - u_ext[HALO:HALO + sub]
                pooled = jnp.dot(d.astype(BF16), wpool_ref[gi], preferred_element_type=F32)
                st["bm"][gi] = (pooled * pscale_ref[:, cs]
                                * zb_ref[0, rows, cs].astype(F32)).astype(BF16)
            return run

        def branch_a(c):
            def run():
                cs = slice(c * cw, (c + 1) * cw)
                y_a = jnp.dot(a_ref[0, rows, :], wa_ref[:, cs], preferred_element_type=F32)
                st["ya"][c] = ga_ref[0, rows, cs].astype(F32) * y_a
            return run

        def branch_b(c):
            def run():
                cs = slice(c * cw, (c + 1) * cw)
                if c == 0:
                    st["bm"] = jnp.concatenate(st["bm"], axis=1)
                y_b = jnp.dot(st["bm"], wb_ref[:, cs], preferred_element_type=F32)
                st["mg"][c] = (st["ya"][c] + gb_ref[0, rows, cs].astype(F32) * y_b).astype(BF16)
            return run

        def output(c):
            def run():
                cs = slice(c * cw, (c + 1) * cw)
                if c == 0:
                    st["mg"] = jnp.concatenate(st["mg"], axis=1)
                y = jnp.dot(st["mg"], wo_ref[:, cs], preferred_element_type=F32)
                xn = x_ref[0, rows, cs] + gate_ref[0, :, cs] * y
                o_ref[0, rows, cs] = xn
                st["ss"].append(jnp.sum(xn * xn, axis=-1, keepdims=True))
            return run

        def normalise(c):
            def run():
                cs = slice(c * cw, (c + 1) * cw)
                ms = functools.reduce(jnp.add, st["ss"]) * (1.0 / D_MODEL)
                o_ref[0, rows, cs] = o_ref[0, rows, cs] * lax.rsqrt(ms + EPS) * fg_ref[:, cs]
            return run

        first = [f(c) for c in range(ncol) for f in (branch_a, pool)]
        return [first] + [[f(c) for c in range(ncol)] for f in (branch_b, output, normalise)]

    tiles = [stages(r0) for r0 in range(0, tm, sub)]
    nstage = len(tiles[0])
    for step in range(len(tiles) + nstage - 1):
        _interleave(*[tl[step - n] for n, tl in enumerate(tiles) if 0 <= step - n < nstage])


def _output_block(a, ub, zb, ga, gb, x, mod3, w_pool, pool_scale, w_a, w_b, w_o, final_g):
    B, L, _ = x.shape
    tm = OUT_TILE
    nt = L // tm
    per_halo = tm // HALO
    n_halo = L // HALO
    tok = pl.BlockSpec((1, tm, D_MODEL), lambda b, i: (b, i, 0))
    const = _resident
    return pl.pallas_call(
        functools.partial(_out_kernel, seq_len=L),
        grid=(B, nt),
        in_specs=[
            tok,
            pl.BlockSpec((1, HALO, D_MODEL), lambda b, i: (b, jnp.maximum(i * per_halo - 1, 0), 0)),
            tok,
            pl.BlockSpec((1, HALO, D_MODEL),
                         lambda b, i: (b, jnp.minimum((i + 1) * per_halo, n_halo - 1), 0)),
            tok, tok, tok, tok,
            pl.BlockSpec((1, 1, D_MODEL), lambda b, i: (b, 0, 2)),
            const(w_pool), const(pool_scale), const(w_a), const(w_b), const(w_o), const(final_g),
        ],
        out_specs=tok,
        out_shape=jax.ShapeDtypeStruct((B, L, D_MODEL), F32),
        compiler_params=pltpu.CompilerParams(
            dimension_semantics=("parallel", "arbitrary"), vmem_limit_bytes=VMEM_LIMIT),
    )(a, ub, ub, ub, zb, ga, gb, x, mod3, w_pool, pool_scale, w_a, w_b, w_o, final_g)


def _rope_tables(seq_len):
    nf = DK // 4
    t = np.arange(seq_len)
    freqs = ROPE_BASE ** (-np.arange(nf, dtype=np.float64) / nf)
    rowp, colp = (t // GRID_W)[:, None], (t % GRID_W)[:, None]
    ang = np.concatenate([rowp * freqs] * 2 + [colp * freqs] * 2, axis=1)
    cos, sin = np.cos(ang), np.sin(ang)
    first = (np.arange(LANES) % (2 * nf)) < nf
    sin_up = np.where(first, -sin, 0.0)
    sin_dn = np.where(first, 0.0, sin)
    return jnp.asarray(np.stack([cos, sin_up, sin_dn]).astype(np.float32))


def _flags(dmax):
    return (dmax[:, 0, 0] > SAFE_DECAY).astype(jnp.int32)


def kernel(x, c, ctx, c_ctx, w_mod, b_mod, norm_g, w_in, w_gate_up_f, b_gate_f, w_gate_up_b,
           b_gate_b, gla_norm_g, w_pool, pool_scale, w_branch_a, w_branch_b, w_out, final_norm_g):
    B, L, _ = x.shape
    assert w_mod.shape[0] == 1, "single-layer configuration"
    wi = w_in[0]
    w1 = wi[:, :W_IN_FIRST].astype(BF16)
    w2 = wi[:, W_IN_FIRST + 2 * RANK:].astype(BF16)
    w_lr = jnp.pad(wi[:, W_IN_FIRST:W_IN_FIRST + 2 * RANK],
                   ((0, 0), (0, LANES - 2 * RANK))).astype(BF16)
    w_up = jnp.zeros((LANES, 2 * KW), F32)
    w_up = w_up.at[:RANK, :KW].set(w_gate_up_f[0]).at[RANK:2 * RANK, KW:].set(w_gate_up_b[0])
    w_up = w_up.astype(BF16)
    b_up = jnp.concatenate([b_gate_f[0], b_gate_b[0]])[None, :]

    c_rows = jnp.zeros((16, D_MODEL), F32).at[:B].set(c).at[B].set(c_ctx)
    mod = _modulation(c_rows, w_mod[0], b_mod[0][None, :])
    mod3 = mod[:, None, :]

    ng = norm_g[0][None, :]
    weights = (w1, w2, w_lr, w_up, b_up)

    kc, vc, bfc, bbc, _ = _project(ctx, mod3, lambda b: B, ng, None, weights, context=True)
    s_zero = jnp.zeros((B, HEADS, DK, DV), F32)
    (s_f,) = _state_scan(kc, vc, bfc, s_zero, reverse=False, emit=False)
    (s_b,) = _state_scan(kc, vc, bbc, s_zero, reverse=True, emit=False)

    q, k, v, za, ub, zb, ga, gb, bf, bb, dm = _project(
        x, mod3, lambda b: b, ng, _rope_tables(L), weights, context=False)
    sb_states, _ = _state_scan(k, v, bb, s_b, reverse=True, emit=True)
    a = _gla_fused(_flags(dm), q, k, v, bf, bb, sb_states, s_f, za, gla_norm_g[0][None, :])

    return _output_block(a, ub, zb, ga, gb, x, mod3, w_pool[0].astype(BF16),
                         pool_scale[0][None, :], w_branch_a[0].astype(BF16),
                         w_branch_b[0].astype(BF16), w_out[0].astype(BF16),
                         final_norm_g[None, :])
```

```python
import functools

import numpy as np
import jax
import jax.numpy as jnp
from jax import lax
from jax.experimental import pallas as pl
from jax.experimental.pallas import tpu as pltpu

F32 = jnp.float32
BF16 = jnp.bfloat16

D_MODEL = 1024
HEADS = 4
DK = 128
DV = 256
KW = HEADS * DK
VW = HEADS * DV
RANK = 16
GATE_NORM = 16.0
CHUNK = 128
GRID_W = 64
ROPE_BASE = 10000.0
POOL_WINDOWS = (2, 4, 8, 16)
POOL_GW = D_MODEL // len(POOL_WINDOWS)
EPS = 1e-6

LANES = 128
SUBLANES = 8
HALO = 16
GLA_TILE = 512
PROJ_TILE = 512
OUT_TILE = 512
SUB_TILE = 256
SAFE_DECAY = 60.0
VMEM_LIMIT = 56 * 1024 * 1024

_C_Q, _C_K, _C_V, _C_ZA = 0, 512, 1024, 2048
_C_UB, _C_ZB, _C_GA, _C_GB = 0, 1024, 2048, 3072
W_IN_FIRST = 3072


def _sigmoid(x):
    return 0.5 * jnp.tanh(0.5 * x) + 0.5


def _silu(x):
    h = 0.5 * x
    return h + h * jnp.tanh(h)


def _mod_kernel(c_ref, w_ref, b_ref, o_ref):
    c = c_ref[...]
    s = c * jax.nn.sigmoid(c)
    o_ref[...] = jnp.dot(s, w_ref[...], preferred_element_type=F32,
                         precision=lax.Precision.HIGHEST) + b_ref[...]


def _modulation(c_rows, w_mod, b_mod):
    return pl.pallas_call(
        _mod_kernel,
        out_shape=jax.ShapeDtypeStruct((c_rows.shape[0], w_mod.shape[1]), F32),
        compiler_params=pltpu.CompilerParams(vmem_limit_bytes=VMEM_LIMIT),
    )(c_rows, w_mod, b_mod)


def _as_column(row):
    return jnp.broadcast_to(row, (SUBLANES, DK)).T[:, 0:1]


def _state_step(state, hd, k, b, v, edge):
    b_edge = b[edge:edge + 1, :]
    kd_t = (k * jnp.exp(b_edge - b)).T.astype(BF16)
    state[hd] = (jnp.exp(_as_column(b_edge)) * state[hd]
                 + jnp.dot(kd_t, v, preferred_element_type=F32))


def _proj_kernel(x_ref, shift_ref, scale_ref, g_ref, *rest, context):
    if context:
        (w1_ref, wlr_ref, wup_ref, bup_ref, k_ref, v_ref, bf_ref, bb_ref, dmax_ref,
         sf_ref, sb_ref, state_f, state_b) = rest
    else:
        (tab_ref, w1_ref, w2_ref, wlr_ref, wup_ref, bup_ref, s0_ref,
         q_ref, k_ref, v_ref, za_ref, ub_ref, zb_ref, ga_ref, gb_ref,
         bf_ref, bb_ref, dmax_ref, sb_ref, state_b) = rest
    tm = x_ref.shape[1]
    sub = min(tm, SUB_TILE)
    worst = []

    if context:
        state_f[...] = jnp.zeros_like(state_f)
        state_b[...] = jnp.zeros_like(state_b)
    else:
        @pl.when(pl.program_id(1) == 0)
        def _():
            state_b[...] = s0_ref[0]

    ri = lax.broadcasted_iota(jnp.int32, (sub, sub), 0)
    ci = lax.broadcasted_iota(jnp.int32, (sub, sub), 1)
    same_chunk = (ri // CHUNK) == (ci // CHUNK)
    tri_f = jnp.where(same_chunk & (ci <= ri), 1.0, 0.0).astype(BF16)
    tri_b = jnp.where(same_chunk & (ci >= ri), 1.0, 0.0).astype(BF16)
    tri_f = jnp.concatenate([tri_f] * 3, axis=1)
    tri_b = jnp.concatenate([tri_b] * 3, axis=1)

    def stages(r0):
        rows = slice(r0, r0 + sub)
        st = {}

        def prologue():
            x = x_ref[0, rows, :]
            ms = jnp.mean(x * x, axis=-1, keepdims=True)
            h = (x * lax.rsqrt(ms + EPS)) * g_ref[...]
            h = h * (1.0 + scale_ref[0]) + shift_ref[0]
            st["hb"] = h.astype(BF16)
            lr = jnp.dot(st["hb"], wlr_ref[...], preferred_element_type=F32).astype(BF16)
            gk = jnp.dot(lr, wup_ref[...], preferred_element_type=F32) + bup_ref[...]
            la = (jnp.minimum(gk, 0.0) - jnp.log(1.0 + jnp.exp(-jnp.abs(gk)))) * (1.0 / GATE_NORM)
            hi = la.astype(BF16)
            r1 = la - hi.astype(F32)
            mid = r1.astype(BF16)
            lo = (r1 - mid.astype(F32)).astype(BF16)
            pieces = jnp.concatenate([hi, mid, lo], axis=0)
            cf = jnp.dot(tri_f, pieces[:, :KW], preferred_element_type=F32)
            cb = jnp.dot(tri_b, pieces[:, KW:], preferred_element_type=F32)
            bf_ref[0, rows, :] = cf
            bb_ref[0, rows, :] = cb
            for c0 in range(0, sub, CHUNK):
                worst.append(jnp.maximum(-cf[c0 + CHUNK - 1:c0 + CHUNK, :], -cb[c0:c0 + 1, :]))

        def plain(w_ref, c0, dst_ref, act=None):
            def run():
                p = jnp.dot(st["hb"], w_ref[:, c0:c0 + dst_ref.shape[2]],
                            preferred_element_type=F32)
                dst_ref[0, rows, :] = (p if act is None else act(p)).astype(dst_ref.dtype)
            return run

        def rotary(c0, dst_ref, scale):
            def run():
                p = jnp.dot(st["hb"], w1_ref[:, c0:c0 + KW], preferred_element_type=F32)
                cos, sin_up, sin_dn = (tab_ref[n, rows, :] for n in range(3))
                for hd in range(HEADS):
                    ph = p[:, hd * DK:(hd + 1) * DK]
                    out = (ph * cos + pltpu.roll(ph, LANES - 32, 1) * sin_up
                           + pltpu.roll(ph, 32, 1) * sin_dn)
                    if scale is not None:
                        out = out * scale
                    dst_ref[0, rows, hd * DK:(hd + 1) * DK] = out.astype(dst_ref.dtype)
            return run

        if context:
            return prologue, [plain(w1_ref, _C_K, k_ref), plain(w1_ref, _C_V, v_ref)]
        return prologue, [
            rotary(_C_K, k_ref, None), plain(w1_ref, _C_V, v_ref), rotary(_C_Q, q_ref, DK ** -0.5),
            plain(w1_ref, _C_ZA, za_ref, _silu), plain(w2_ref, _C_UB, ub_ref),
            plain(w2_ref, _C_ZB, zb_ref, _silu), plain(w2_ref, _C_GA, ga_ref, _sigmoid),
            plain(w2_ref, _C_GB, gb_ref, _sigmoid)]

    def scan_steps(r0, state, b_ref, reverse, emit):
        steps = []
        starts = range(r0, r0 + sub, CHUNK)
        for c0 in (reversed(starts) if reverse else starts):
            for hd in range(HEADS):
                def run(c0=c0, hd=hd):
                    rows = slice(c0, c0 + CHUNK)
                    ks = slice(hd * DK, (hd + 1) * DK)
                    if emit:
                        sb_ref[0, c0 // CHUNK, ks, :] = state[hd].astype(BF16)
                    _state_step(state, hd, k_ref[0, rows, ks].astype(F32), b_ref[0, rows, ks],
                                v_ref[0, rows, hd * DV:(hd + 1) * DV], 0 if reverse else CHUNK - 1)
                steps.append(run)
        return steps

    starts = list(reversed(range(0, tm, sub)))
    tiles = [stages(r0) for r0 in starts]
    tiles[0][0]()
    for n, (_, matmuls) in enumerate(tiles):
        for step in matmuls[:2]:
            step()
        fill = scan_steps(starts[n], state_b, bb_ref, True, not context)
        if context:
            fill += scan_steps(starts[n], state_f, bf_ref, False, False)
        if n + 1 < len(tiles):
            fill.append(tiles[n + 1][0])
        _interleave(matmuls[2:], fill)
    dmax_ref[0] = jnp.broadcast_to(
        jnp.max(functools.reduce(jnp.maximum, worst), axis=1, keepdims=True), dmax_ref.shape[1:])
    if context:
        sf_ref[0] = state_f[...]
        sb_ref[0] = state_b[...]


def _interleave(*lists):
    order = sorted(((i + 0.5) / len(lst), which, i)
                   for which, lst in enumerate(lists) for i in range(len(lst)))
    for _, which, idx in order:
        lists[which][idx]()


def _resident(arr):
    return pl.BlockSpec(arr.shape, lambda *_: (0,) * arr.ndim, pipeline_mode=pl.Buffered(1))


def _project(x, mod3, mod_row, norm_g, tables, weights, s0, *, context):
    B, L, _ = x.shape
    tm = min(L, PROJ_TILE)
    nt = L // tm
    tile = (lambda i: i) if context else (lambda i: nt - 1 - i)
    tok = lambda width: pl.BlockSpec((1, tm, width), lambda b, i: (b, tile(i), 0))
    bf = lambda width: jax.ShapeDtypeStruct((B, L, width), BF16)
    st_spec = pl.BlockSpec((1, HEADS, DK, DV), lambda b, i: (b, 0, 0, 0))
    st_shape = jax.ShapeDtypeStruct((B, HEADS, DK, DV), F32)
    state = pltpu.VMEM((HEADS, DK, DV), F32)
    if context:
        assert nt == 1, "context stream is one tile per sample"
        widths = [KW, VW]
        extra = [weights[0]] + list(weights[2:])
        extra_specs = [_resident(w) for w in extra]
        state_specs, state_shapes, scratch = [st_spec, st_spec], [st_shape, st_shape], [state, state]
    else:
        widths = [KW, KW, VW, VW, D_MODEL, D_MODEL, D_MODEL, D_MODEL]
        extra = [tables] + list(weights) + [s0]
        extra_specs = ([pl.BlockSpec((3, tm, LANES), lambda b, i: (0, tile(i), 0))]
                       + [_resident(w) for w in weights] + [st_spec])
        state_specs = [pl.BlockSpec((1, tm // CHUNK, KW, DV), lambda b, i: (b, tile(i), 0, 0))]
        state_shapes = [jax.ShapeDtypeStruct((B, L // CHUNK, KW, DV), BF16)]
        scratch = [state]
    return pl.pallas_call(
        functools.partial(_proj_kernel, context=context),
        grid=(B, nt),
        in_specs=[
            tok(D_MODEL),
            pl.BlockSpec((1, 1, D_MODEL), lambda b, i: (mod_row(b), 0, 0)),
            pl.BlockSpec((1, 1, D_MODEL), lambda b, i: (mod_row(b), 0, 1)),
            _resident(norm_g),
        ] + extra_specs,
        out_specs=[tok(w) for w in widths] + [
            tok(KW), tok(KW),
            pl.BlockSpec((1, SUBLANES, LANES), lambda b, i: (b * nt + tile(i), 0, 0))] + state_specs,
        out_shape=[bf(w) for w in widths] + [
            jax.ShapeDtypeStruct((B, L, KW), F32), jax.ShapeDtypeStruct((B, L, KW), F32),
            jax.ShapeDtypeStruct((B * nt, SUBLANES, LANES), F32)] + state_shapes,
        scratch_shapes=scratch,
        compiler_params=pltpu.CompilerParams(
            dimension_semantics=("parallel", "arbitrary"), vmem_limit_bytes=VMEM_LIMIT),
    )(x, mod3, mod3, norm_g, *extra)


def _gla_kernel(flags_ref, q_ref, k_ref, v_ref, bf_ref, bb_ref, sb_ref, s0_ref, z_ref, g_ref,
                out_ref, state, k_rows, bf_rows, bb_rows, *, nblk, blk_per_flag):
    bi = pl.program_id(0)
    i = pl.program_id(1)
    nch = q_ref.shape[1] // CHUNK

    @pl.when(i == 0)
    def _():
        state[...] = s0_ref[0]

    exact_path = flags_ref[bi * (nblk // blk_per_flag) + i // blk_per_flag] != 0
    row = lax.broadcasted_iota(jnp.int32, (CHUNK, CHUNK), 0)
    col = lax.broadcasted_iota(jnp.int32, (CHUNK, CHUNK), 1)
    nt_dims = (((1,), (1,)), ((), ()))

    def head_chunk(c, hd, exact):
        rows = slice(c * CHUNK, (c + 1) * CHUNK)
        ks = slice(hd * DK, (hd + 1) * DK)
        vs = slice(hd * DV, (hd + 1) * DV)
        q = q_ref[0, rows, ks].astype(F32)
        k = k_ref[0, rows, ks].astype(F32)
        v = v_ref[0, rows, vs]
        bf = bf_ref[0, rows, ks]
        bb = bb_ref[0, rows, ks]
        qf = (q * jnp.exp(bf)).astype(BF16)
        qb = (q * jnp.exp(bb)).astype(BF16)
        if not exact:
            kf = (k * jnp.exp(-bf)).astype(BF16)
            kb = (k * jnp.exp(-bb)).astype(BF16)
            sc = (jnp.where(col <= row, lax.dot_general(qf, kf, nt_dims,
                                                        preferred_element_type=F32), 0.0)
                  + jnp.where(col >= row, lax.dot_general(qb, kb, nt_dims,
                                                          preferred_element_type=F32), 0.0))
        else:
            k_rows[...] = k
            bf_rows[...] = bf
            bb_rows[...] = bb
            tok = lax.broadcasted_iota(jnp.int32, (CHUNK, DK), 0)

            def col_body(j, acc):
                dec = (jnp.where(tok >= j, jnp.exp(jnp.minimum(bf - bf_rows[pl.ds(j, 1), :], 0.0)), 0.0)
                       + jnp.where(tok <= j, jnp.exp(jnp.minimum(bb - bb_rows[pl.ds(j, 1), :], 0.0)), 0.0))
                w = q * k_rows[pl.ds(j, 1), :] * dec
                return jnp.where(col == j, jnp.sum(w, axis=1, keepdims=True), acc)

            sc = lax.fori_loop(0, CHUNK, col_body, jnp.zeros((CHUNK, CHUNK), F32))
        s_cat = jnp.concatenate([state[hd].astype(BF16), sb_ref[0, c, ks, :]], axis=0)
        q_cat = jnp.concatenate([qf, qb], axis=1)
        o = (jnp.dot(sc.astype(BF16), v, preferred_element_type=F32)
             + jnp.dot(q_cat, s_cat, preferred_element_type=F32))
        _state_step(state, hd, k, bf, v, CHUNK - 1)
        ms = jnp.mean(o * o, axis=-1, keepdims=True)
        on = o * lax.rsqrt(ms + EPS) * g_ref[...]
        out_ref[0, rows, vs] = (on * z_ref[0, rows, vs].astype(F32)).astype(out_ref.dtype)

    for exact in (False, True):
        @pl.when(exact_path == exact)
        def _():
            for c in range(nch):
                for hd in range(HEADS):
                    head_chunk(c, hd, exact)


def _gla_fused(flags, q, k, v, bf, bb, sb, s0, z, g):
    B, L, _ = q.shape
    tb = GLA_TILE
    nblk = L // tb
    nch = tb // CHUNK
    tok = lambda width: pl.BlockSpec((1, tb, width), lambda bi, i, fl: (bi, i, 0))
    return pl.pallas_call(
        functools.partial(_gla_kernel, nblk=nblk, blk_per_flag=nblk * B // flags.shape[0]),
        grid_spec=pltpu.PrefetchScalarGridSpec(
            num_scalar_prefetch=1, grid=(B, nblk),
            in_specs=[tok(KW), tok(KW), tok(VW), tok(KW), tok(KW),
                      pl.BlockSpec((1, nch, KW, DV), lambda bi, i, fl: (bi, i, 0, 0)),
                      pl.BlockSpec((1, HEADS, DK, DV), lambda bi, i, fl: (bi, 0, 0, 0)),
                      tok(VW), pl.BlockSpec(g.shape, lambda bi, i, fl: (0, 0))],
            out_specs=tok(VW),
            scratch_shapes=[pltpu.VMEM((HEADS, DK, DV), F32)]
                           + [pltpu.VMEM((CHUNK, DK), F32)] * 3),
        out_shape=jax.ShapeDtypeStruct((B, L, VW), BF16),
        compiler_params=pltpu.CompilerParams(
            dimension_semantics=("parallel", "arbitrary"), vmem_limit_bytes=VMEM_LIMIT),
    )(flags, q, k, v, bf, bb, sb, s0, z, g)


def _out_kernel(a_ref, up_ref, uc_ref, un_ref, zb_ref, ga_ref, gb_ref, x_ref, gate_ref,
                wpool_ref, pscale_ref, wa_ref, wb_ref, wo_ref, fg_ref, o_ref, *, seq_len):
    i = pl.program_id(1)
    nt = pl.num_programs(1)
    tm = uc_ref.shape[1]
    sub = min(tm, SUB_TILE)
    ext = sub + 2 * HALO
    cw = POOL_GW
    ncol = D_MODEL // cw

    def stages(r0):
        rows = slice(r0, r0 + sub)
        st = {"bm": [None] * ncol, "ya": [None] * ncol, "mg": [None] * ncol, "ss": []}

        def pool_input(cs):
            if r0 == 0:
                before = jnp.where(i > 0, up_ref[0, :, cs].astype(F32), 0.0)
            else:
                before = uc_ref[0, r0 - HALO:r0, cs].astype(F32)
            if r0 + sub == tm:
                after = jnp.where(i < nt - 1, un_ref[0, :, cs].astype(F32), 0.0)
            else:
                after = uc_ref[0, r0 + sub:r0 + sub + HALO, cs].astype(F32)
            return jnp.concatenate([before, uc_ref[0, rows, cs].astype(F32), after], axis=0)

        def pool(gi):
            def run():
                w = POOL_WINDOWS[gi]
                cs = slice(gi * cw, (gi + 1) * cw)
                u_ext = pool_input(cs)
                acc = u_ext + pltpu.roll(u_ext, 1, 0)
                half = 1
                while 2 * half < w:
                    acc = pltpu.roll(acc, half, 0) + pltpu.roll(acc, ext - half, 0)
                    half *= 2
                t = i * tm + r0 + lax.broadcasted_iota(jnp.int32, (sub, LANES), 0)
                lo = jnp.maximum(t - w // 2, 0)
                hi = jnp.minimum(t + (w - 1 - w // 2), seq_len - 1)
                inv_cnt = 1.0 / (hi - lo + 1).astype(F32)
                inv_cnt = jnp.concatenate([inv_cnt] * (cw // LANES), axis=1)
                d = acc[HALO:HALO + sub] * inv_cnt - u_ext[HALO:HALO + sub]
                pooled = jnp.dot(d.astype(BF16), wpool_ref[gi], preferred_element_type=F32)
                st["bm"][gi] = (pooled * pscale_ref[:, cs]
                                * zb_ref[0, rows, cs].astype(F32)).astype(BF16)
            return run

        def branch_a(c):
            def run():
                cs = slice(c * cw, (c + 1) * cw)
                y_a = jnp.dot(a_ref[0, rows, :], wa_ref[:, cs], preferred_element_type=F32)
                st["ya"][c] = ga_ref[0, rows, cs].astype(F32) * y_a
            return run

        def branch_b(c):
            def run():
                cs = slice(c * cw, (c + 1) * cw)
                if c == 0:
                    st["bm"] = jnp.concatenate(st["bm"], axis=1)
                y_b = jnp.dot(st["bm"], wb_ref[:, cs], preferred_element_type=F32)
                st["mg"][c] = (st["ya"][c] + gb_ref[0, rows, cs].astype(F32) * y_b).astype(BF16)
            return run

        def output(c):
            def run():
                cs = slice(c * cw, (c + 1) * cw)
                if c == 0:
                    st["mg"] = jnp.concatenate(st["mg"], axis=1)
                y = jnp.dot(st["mg"], wo_ref[:, cs], preferred_element_type=F32)
                xn = x_ref[0, rows, cs] + gate_ref[0, :, cs] * y
                o_ref[0, rows, cs] = xn
                st["ss"].append(jnp.sum(xn * xn, axis=-1, keepdims=True))
            return run

        def normalise(c):
            def run():
                cs = slice(c * cw, (c + 1) * cw)
                ms = functools.reduce(jnp.add, st["ss"]) * (1.0 / D_MODEL)
                o_ref[0, rows, cs] = o_ref[0, rows, cs] * lax.rsqrt(ms + EPS) * fg_ref[:, cs]
            return run

        first = [f(c) for c in range(ncol) for f in (branch_a, pool)]
        return [first] + [[f(c) for c in range(ncol)] for f in (branch_b, output, normalise)]

    tiles = [stages(r0) for r0 in range(0, tm, sub)]
    nstage = len(tiles[0])
    for step in range(len(tiles) + nstage - 1):
        _interleave(*[tl[step - n] for n, tl in enumerate(tiles) if 0 <= step - n < nstage])


def _output_block(a, ub, zb, ga, gb, x, mod3, w_pool, pool_scale, w_a, w_b, w_o, final_g):
    B, L, _ = x.shape
    tm = OUT_TILE
    nt = L // tm
    per_halo = tm // HALO
    n_halo = L // HALO
    tok = pl.BlockSpec((1, tm, D_MODEL), lambda b, i: (b, i, 0))
    const = _resident
    return pl.pallas_call(
        functools.partial(_out_kernel, seq_len=L),
        grid=(B, nt),
        in_specs=[
            tok,
            pl.BlockSpec((1, HALO, D_MODEL), lambda b, i: (b, jnp.maximum(i * per_halo - 1, 0), 0)),
            tok,
            pl.BlockSpec((1, HALO, D_MODEL),
                         lambda b, i: (b, jnp.minimum((i + 1) * per_halo, n_halo - 1), 0)),
            tok, tok, tok, tok,
            pl.BlockSpec((1, 1, D_MODEL), lambda b, i: (b, 0, 2)),
            const(w_pool), const(pool_scale), const(w_a), const(w_b), const(w_o), const(final_g),
        ],
        out_specs=tok,
        out_shape=jax.ShapeDtypeStruct((B, L, D_MODEL), F32),
        compiler_params=pltpu.CompilerParams(
            dimension_semantics=("parallel", "arbitrary"), vmem_limit_bytes=VMEM_LIMIT),
    )(a, ub, ub, ub, zb, ga, gb, x, mod3, w_pool, pool_scale, w_a, w_b, w_o, final_g)


def _rope_tables(seq_len):
    nf = DK // 4
    t = np.arange(seq_len)
    freqs = ROPE_BASE ** (-np.arange(nf, dtype=np.float64) / nf)
    rowp, colp = (t // GRID_W)[:, None], (t % GRID_W)[:, None]
    ang = np.concatenate([rowp * freqs] * 2 + [colp * freqs] * 2, axis=1)
    cos, sin = np.cos(ang), np.sin(ang)
    first = (np.arange(LANES) % (2 * nf)) < nf
    sin_up = np.where(first, -sin, 0.0)
    sin_dn = np.where(first, 0.0, sin)
    return jnp.asarray(np.stack([cos, sin_up, sin_dn]).astype(np.float32))


def _flags(dmax):
    return (dmax[:, 0, 0] > SAFE_DECAY).astype(jnp.int32)


def kernel(x, c, ctx, c_ctx, w_mod, b_mod, norm_g, w_in, w_gate_up_f, b_gate_f, w_gate_up_b,
           b_gate_b, gla_norm_g, w_pool, pool_scale, w_branch_a, w_branch_b, w_out, final_norm_g):
    B, L, _ = x.shape
    assert w_mod.shape[0] == 1, "single-layer configuration"
    wi = w_in[0]
    w1 = wi[:, :W_IN_FIRST].astype(BF16)
    w2 = wi[:, W_IN_FIRST + 2 * RANK:].astype(BF16)
    w_lr = jnp.pad(wi[:, W_IN_FIRST:W_IN_FIRST + 2 * RANK],
                   ((0, 0), (0, LANES - 2 * RANK))).astype(BF16)
    w_up = jnp.zeros((LANES, 2 * KW), F32)
    w_up = w_up.at[:RANK, :KW].set(w_gate_up_f[0]).at[RANK:2 * RANK, KW:].set(w_gate_up_b[0])
    w_up = w_up.astype(BF16)
    b_up = jnp.concatenate([b_gate_f[0], b_gate_b[0]])[None, :]

    c_rows = jnp.zeros((16, D_MODEL), F32).at[:B].set(c).at[B].set(c_ctx)
    mod = _modulation(c_rows, w_mod[0], b_mod[0][None, :])
    mod3 = mod[:, None, :]

    ng = norm_g[0][None, :]
    weights = (w1, w2, w_lr, w_up, b_up)

    *_, s_f, s_b = _project(ctx, mod3, lambda b: B, ng, None, weights, None, context=True)

    q, k, v, za, ub, zb, ga, gb, bf, bb, dm, sb_states = _project(
        x, mod3, lambda b: b, ng, _rope_tables(L), weights, s_b, context=False)
    a = _gla_fused(_flags(dm), q, k, v, bf, bb, sb_states, s_f, za, gla_norm_g[0][None, :])

    return _output_block(a, ub, zb, ga, gb, x, mod3, w_pool[0].astype(BF16),
                         pool_scale[0][None, :], w_branch_a[0].astype(BF16),
                         w_branch_b[0].astype(BF16), w_out[0].astype(BF16),
                         final_norm_g[None, :])
```

```python
import functools

import numpy as np
import jax
import jax.numpy as jnp
from jax import lax
from jax.experimental import pallas as pl
from jax.experimental.pallas import tpu as pltpu

F32 = jnp.float32
BF16 = jnp.bfloat16

D_MODEL = 1024
HEADS = 4
DK = 128
DV = 256
KW = HEADS * DK
VW = HEADS * DV
RANK = 16
GATE_NORM = 16.0
CHUNK = 128
GRID_W = 64
ROPE_BASE = 10000.0
POOL_WINDOWS = (2, 4, 8, 16)
POOL_GW = D_MODEL // len(POOL_WINDOWS)
EPS = 1e-6

LANES = 128
SUBLANES = 8
HALO = 16
GLA_TILE = 512
PROJ_TILE = 512
OUT_TILE = 512
SUB_TILE = 256
SAFE_DECAY = 60.0
VMEM_LIMIT = 56 * 1024 * 1024

_C_Q, _C_K, _C_V, _C_ZA = 0, 512, 1024, 2048
_C_UB, _C_ZB, _C_GA, _C_GB = 0, 1024, 2048, 3072
W_IN_FIRST = 3072


def _sigmoid(x):
    return 0.5 * jnp.tanh(0.5 * x) + 0.5


def _silu(x):
    h = 0.5 * x
    return h + h * jnp.tanh(h)


def _mod_kernel(c_ref, w_ref, b_ref, o_ref):
    c = c_ref[...]
    s = c * jax.nn.sigmoid(c)
    o_ref[...] = jnp.dot(s, w_ref[...], preferred_element_type=F32,
                         precision=lax.Precision.HIGHEST) + b_ref[...]


def _modulation(c_rows, w_mod, b_mod):
    return pl.pallas_call(
        _mod_kernel,
        out_shape=jax.ShapeDtypeStruct((c_rows.shape[0], w_mod.shape[1]), F32),
        compiler_params=pltpu.CompilerParams(vmem_limit_bytes=VMEM_LIMIT),
    )(c_rows, w_mod, b_mod)


def _as_column(row):
    return jnp.broadcast_to(row, (SUBLANES, DK)).T[:, 0:1]


def _state_step(state, hd, k, b, v, edge):
    b_edge = b[edge:edge + 1, :]
    kd_t = (k * jnp.exp(b_edge - b)).T.astype(BF16)
    state[hd] = (jnp.exp(_as_column(b_edge)) * state[hd]
                 + jnp.dot(kd_t, v, preferred_element_type=F32))


def _proj_kernel(x_ref, shift_ref, scale_ref, g_ref, *rest, context, seq_len):
    if context:
        (w1_ref, wlr_ref, wup_ref, bup_ref, k_ref, v_ref, bf_ref, bb_ref, dmax_ref,
         sf_ref, sb_ref, state_f, state_b) = rest
    else:
        (xp_ref, xn_ref, tab_ref, w1_ref, w2_ref, wlr_ref, wup_ref, bup_ref, wpool_ref,
         pscale_ref, s0_ref,
         q_ref, k_ref, v_ref, za_ref, bm_ref, ga_ref, gb_ref,
         bf_ref, bb_ref, dmax_ref, sb_ref, state_b) = rest
    tm = x_ref.shape[1]
    sub = min(tm, SUB_TILE)
    ntile = seq_len // tm
    tile = ntile - 1 - pl.program_id(1)
    worst = []

    if context:
        state_f[...] = jnp.zeros_like(state_f)
        state_b[...] = jnp.zeros_like(state_b)
    else:
        @pl.when(pl.program_id(1) == 0)
        def _():
            state_b[...] = s0_ref[0]

    ri = lax.broadcasted_iota(jnp.int32, (sub, sub), 0)
    ci = lax.broadcasted_iota(jnp.int32, (sub, sub), 1)
    same_chunk = (ri // CHUNK) == (ci // CHUNK)
    tri_f = jnp.where(same_chunk & (ci <= ri), 1.0, 0.0).astype(BF16)
    tri_b = jnp.where(same_chunk & (ci >= ri), 1.0, 0.0).astype(BF16)
    tri_f = jnp.concatenate([tri_f] * 3, axis=1)
    tri_b = jnp.concatenate([tri_b] * 3, axis=1)

    def stages(r0):
        rows = slice(r0, r0 + sub)
        st = {}

        def prologue():
            if context:
                x = x_ref[0, rows, :]
            else:
                before = xp_ref[0] if r0 == 0 else x_ref[0, r0 - HALO:r0, :]
                after = xn_ref[0] if r0 + sub == tm else x_ref[0, r0 + sub:r0 + sub + HALO, :]
                x = jnp.concatenate([before, x_ref[0, rows, :], after], axis=0)
            ms = jnp.mean(x * x, axis=-1, keepdims=True)
            h = (x * lax.rsqrt(ms + EPS)) * g_ref[...]
            h = (h * (1.0 + scale_ref[0]) + shift_ref[0]).astype(BF16)
            st["hb_ext"] = h
            st["hb"] = h if context else h[HALO:HALO + sub]
            lr = jnp.dot(st["hb"], wlr_ref[...], preferred_element_type=F32).astype(BF16)
            gk = jnp.dot(lr, wup_ref[...], preferred_element_type=F32) + bup_ref[...]
            la = (jnp.minimum(gk, 0.0) - jnp.log(1.0 + jnp.exp(-jnp.abs(gk)))) * (1.0 / GATE_NORM)
            hi = la.astype(BF16)
            r1 = la - hi.astype(F32)
            mid = r1.astype(BF16)
            lo = (r1 - mid.astype(F32)).astype(BF16)
            pieces = jnp.concatenate([hi, mid, lo], axis=0)
            cf = jnp.dot(tri_f, pieces[:, :KW], preferred_element_type=F32)
            cb = jnp.dot(tri_b, pieces[:, KW:], preferred_element_type=F32)
            bf_ref[0, rows, :] = cf
            bb_ref[0, rows, :] = cb
            for c0 in range(0, sub, CHUNK):
                worst.append(jnp.maximum(-cf[c0 + CHUNK - 1:c0 + CHUNK, :], -cb[c0:c0 + 1, :]))

        def plain(w_ref, c0, dst_ref, act=None):
            def run():
                p = jnp.dot(st["hb"], w_ref[:, c0:c0 + dst_ref.shape[2]],
                            preferred_element_type=F32)
                dst_ref[0, rows, :] = (p if act is None else act(p)).astype(dst_ref.dtype)
            return run

        def rotary(c0, dst_ref, scale):
            def run():
                p = jnp.dot(st["hb"], w1_ref[:, c0:c0 + KW], preferred_element_type=F32)
                cos, sin_up, sin_dn = (tab_ref[n, rows, :] for n in range(3))
                for hd in range(HEADS):
                    ph = p[:, hd * DK:(hd + 1) * DK]
                    out = (ph * cos + pltpu.roll(ph, LANES - 32, 1) * sin_up
                           + pltpu.roll(ph, 32, 1) * sin_dn)
                    if scale is not None:
                        out = out * scale
                    dst_ref[0, rows, hd * DK:(hd + 1) * DK] = out.astype(dst_ref.dtype)
            return run

        def pool_input():
            u = jnp.dot(st["hb_ext"], w2_ref[:, _C_UB:_C_UB + D_MODEL], preferred_element_type=F32)
            parts = [u[:HALO], u[HALO:HALO + sub], u[HALO + sub:]]
            if r0 == 0:
                parts[0] = jnp.where(tile > 0, parts[0], 0.0)
            if r0 + sub == tm:
                parts[2] = jnp.where(tile < ntile - 1, parts[2], 0.0)
            st["u"] = jnp.concatenate(parts, axis=0)

        def pool_gate():
            st["szb"] = _silu(jnp.dot(st["hb"], w2_ref[:, _C_ZB:_C_ZB + D_MODEL],
                                      preferred_element_type=F32))

        def pool(gi):
            def run():
                w = POOL_WINDOWS[gi]
                cs = slice(gi * POOL_GW, (gi + 1) * POOL_GW)
                ext = sub + 2 * HALO
                u_ext = st["u"][:, cs]
                acc = u_ext + pltpu.roll(u_ext, 1, 0)
                half = 1
                while 2 * half < w:
                    acc = pltpu.roll(acc, half, 0) + pltpu.roll(acc, ext - half, 0)
                    half *= 2
                t = tile * tm + r0 + lax.broadcasted_iota(jnp.int32, (sub, LANES), 0)
                lo = jnp.maximum(t - w // 2, 0)
                hi = jnp.minimum(t + (w - 1 - w // 2), seq_len - 1)
                inv_cnt = 1.0 / (hi - lo + 1).astype(F32)
                inv_cnt = jnp.concatenate([inv_cnt] * (POOL_GW // LANES), axis=1)
                d = acc[HALO:HALO + sub] * inv_cnt - u_ext[HALO:HALO + sub]
                pooled = jnp.dot(d.astype(BF16), wpool_ref[gi], preferred_element_type=F32)
                bm_ref[0, rows, cs] = (pooled * pscale_ref[:, cs]
                                       * st["szb"][:, cs]).astype(bm_ref.dtype)
            return run

        if context:
            return prologue, [plain(w1_ref, _C_K, k_ref), plain(w1_ref, _C_V, v_ref)], []
        return prologue, [
            rotary(_C_K, k_ref, None), plain(w1_ref, _C_V, v_ref), pool_input, pool_gate,
            rotary(_C_Q, q_ref, DK ** -0.5), plain(w1_ref, _C_ZA, za_ref, _silu),
            plain(w2_ref, _C_GA, ga_ref, _sigmoid), plain(w2_ref, _C_GB, gb_ref, _sigmoid),
        ], [pool(gi) for gi in range(len(POOL_WINDOWS))]

    def scan_steps(r0, state, b_ref, reverse, emit):
        steps = []
        starts = range(r0, r0 + sub, CHUNK)
        for c0 in (reversed(starts) if reverse else starts):
            for hd in range(HEADS):
                def run(c0=c0, hd=hd):
                    rows = slice(c0, c0 + CHUNK)
                    ks = slice(hd * DK, (hd + 1) * DK)
                    if emit:
                        sb_ref[0, c0 // CHUNK, ks, :] = state[hd].astype(BF16)
                    _state_step(state, hd, k_ref[0, rows, ks].astype(F32), b_ref[0, rows, ks],
                                v_ref[0, rows, hd * DV:(hd + 1) * DV], 0 if reverse else CHUNK - 1)
                steps.append(run)
        return steps

    starts = list(reversed(range(0, tm, sub)))
    tiles = [stages(r0) for r0 in starts]
    tiles[0][0]()
    for n, (_, matmuls, pools) in enumerate(tiles):
        nhead = 2 if context else 4
        for step in matmuls[:nhead]:
            step()
        fill = scan_steps(starts[n], state_b, bb_ref, True, not context)
        if context:
            fill += scan_steps(starts[n], state_f, bf_ref, False, False)
        fill += pools
        if n + 1 < len(tiles):
            fill.append(tiles[n + 1][0])
        _interleave(matmuls[nhead:], fill)
    dmax_ref[0] = jnp.broadcast_to(
        jnp.max(functools.reduce(jnp.maximum, worst), axis=1, keepdims=True), dmax_ref.shape[1:])
    if context:
        sf_ref[0] = state_f[...]
        sb_ref[0] = state_b[...]


def _interleave(*lists):
    order = sorted(((i + 0.5) / len(lst), which, i)
                   for which, lst in enumerate(lists) for i in range(len(lst)))
    for _, which, idx in order:
        lists[which][idx]()


def _resident(arr):
    return pl.BlockSpec(arr.shape, lambda *_: (0,) * arr.ndim, pipeline_mode=pl.Buffered(1))


def _project(x, mod3, mod_row, norm_g, tables, weights, pool_w, s0, *, context):
    B, L, _ = x.shape
    tm = min(L, PROJ_TILE)
    nt = L // tm
    tile = (lambda i: i) if context else (lambda i: nt - 1 - i)
    tok = lambda width: pl.BlockSpec((1, tm, width), lambda b, i: (b, tile(i), 0))
    bf = lambda width: jax.ShapeDtypeStruct((B, L, width), BF16)
    st_spec = pl.BlockSpec((1, HEADS, DK, DV), lambda b, i: (b, 0, 0, 0))
    st_shape = jax.ShapeDtypeStruct((B, HEADS, DK, DV), F32)
    state = pltpu.VMEM((HEADS, DK, DV), F32)
    if context:
        assert nt == 1, "context stream is one tile per sample"
        widths = [KW, VW]
        extra = [weights[0]] + list(weights[2:])
        extra_specs = [_resident(w) for w in extra]
        state_specs, state_shapes, scratch = [st_spec, st_spec], [st_shape, st_shape], [state, state]
    else:
        widths = [KW, KW, VW, VW, D_MODEL, D_MODEL, D_MODEL]
        per_halo, n_halo = tm // HALO, L // HALO
        halo = lambda idx: pl.BlockSpec((1, HALO, D_MODEL), lambda b, i: (b, idx(tile(i)), 0))
        extra = [x, x, tables] + list(weights) + list(pool_w) + [s0]
        extra_specs = ([halo(lambda t: jnp.maximum(t * per_halo - 1, 0)),
                        halo(lambda t: jnp.minimum((t + 1) * per_halo, n_halo - 1)),
                        pl.BlockSpec((3, tm, LANES), lambda b, i: (0, tile(i), 0))]
                       + [_resident(w) for w in list(weights) + list(pool_w)] + [st_spec])
        state_specs = [pl.BlockSpec((1, tm // CHUNK, KW, DV), lambda b, i: (b, tile(i), 0, 0))]
        state_shapes = [jax.ShapeDtypeStruct((B, L // CHUNK, KW, DV), BF16)]
        scratch = [state]
    return pl.pallas_call(
        functools.partial(_proj_kernel, context=context, seq_len=L),
        grid=(B, nt),
        in_specs=[
            tok(D_MODEL),
            pl.BlockSpec((1, 1, D_MODEL), lambda b, i: (mod_row(b), 0, 0)),
            pl.BlockSpec((1, 1, D_MODEL), lambda b, i: (mod_row(b), 0, 1)),
            _resident(norm_g),
        ] + extra_specs,
        out_specs=[tok(w) for w in widths] + [
            tok(KW), tok(KW),
            pl.BlockSpec((1, SUBLANES, LANES), lambda b, i: (b * nt + tile(i), 0, 0))] + state_specs,
        out_shape=[bf(w) for w in widths] + [
            jax.ShapeDtypeStruct((B, L, KW), F32), jax.ShapeDtypeStruct((B, L, KW), F32),
            jax.ShapeDtypeStruct((B * nt, SUBLANES, LANES), F32)] + state_shapes,
        scratch_shapes=scratch,
        compiler_params=pltpu.CompilerParams(
            dimension_semantics=("parallel", "arbitrary"), vmem_limit_bytes=VMEM_LIMIT),
    )(x, mod3, mod3, norm_g, *extra)


def _gla_kernel(flags_ref, q_ref, k_ref, v_ref, bf_ref, bb_ref, sb_ref, s0_ref, z_ref, g_ref,
                out_ref, state, k_rows, bf_rows, bb_rows, *, nblk, blk_per_flag):
    bi = pl.program_id(0)
    i = pl.program_id(1)
    nch = q_ref.shape[1] // CHUNK

    @pl.when(i == 0)
    def _():
        state[...] = s0_ref[0]

    exact_path = flags_ref[bi * (nblk // blk_per_flag) + i // blk_per_flag] != 0
    row = lax.broadcasted_iota(jnp.int32, (CHUNK, CHUNK), 0)
    col = lax.broadcasted_iota(jnp.int32, (CHUNK, CHUNK), 1)
    nt_dims = (((1,), (1,)), ((), ()))

    def head_chunk(c, hd, exact):
        rows = slice(c * CHUNK, (c + 1) * CHUNK)
        ks = slice(hd * DK, (hd + 1) * DK)
        vs = slice(hd * DV, (hd + 1) * DV)
        q = q_ref[0, rows, ks].astype(F32)
        k = k_ref[0, rows, ks].astype(F32)
        v = v_ref[0, rows, vs]
        bf = bf_ref[0, rows, ks]
        bb = bb_ref[0, rows, ks]
        qf = (q * jnp.exp(bf)).astype(BF16)
        qb = (q * jnp.exp(bb)).astype(BF16)
        if not exact:
            kf = (k * jnp.exp(-bf)).astype(BF16)
            kb = (k * jnp.exp(-bb)).astype(BF16)
            sc = (jnp.where(col <= row, lax.dot_general(qf, kf, nt_dims,
                                                        preferred_element_type=F32), 0.0)
                  + jnp.where(col >= row, lax.dot_general(qb, kb, nt_dims,
                                                          preferred_element_type=F32), 0.0))
        else:
            k_rows[...] = k
            bf_rows[...] = bf
            bb_rows[...] = bb
            tok = lax.broadcasted_iota(jnp.int32, (CHUNK, DK), 0)

            def col_body(j, acc):
                dec = (jnp.where(tok >= j, jnp.exp(jnp.minimum(bf - bf_rows[pl.ds(j, 1), :], 0.0)), 0.0)
                       + jnp.where(tok <= j, jnp.exp(jnp.minimum(bb - bb_rows[pl.ds(j, 1), :], 0.0)), 0.0))
                w = q * k_rows[pl.ds(j, 1), :] * dec
                return jnp.where(col == j, jnp.sum(w, axis=1, keepdims=True), acc)

            sc = lax.fori_loop(0, CHUNK, col_body, jnp.zeros((CHUNK, CHUNK), F32))
        s_cat = jnp.concatenate([state[hd].astype(BF16), sb_ref[0, c, ks, :]], axis=0)
        q_cat = jnp.concatenate([qf, qb], axis=1)
        o = (jnp.dot(sc.astype(BF16), v, preferred_element_type=F32)
             + jnp.dot(q_cat, s_cat, preferred_element_type=F32))
        _state_step(state, hd, k, bf, v, CHUNK - 1)
        ms = jnp.mean(o * o, axis=-1, keepdims=True)
        on = o * lax.rsqrt(ms + EPS) * g_ref[...]
        out_ref[0, rows, vs] = (on * z_ref[0, rows, vs].astype(F32)).astype(out_ref.dtype)

    for exact in (False, True):
        @pl.when(exact_path == exact)
        def _():
            for c in range(nch):
                for hd in range(HEADS):
                    head_chunk(c, hd, exact)


def _gla_fused(flags, q, k, v, bf, bb, sb, s0, z, g):
    B, L, _ = q.shape
    tb = GLA_TILE
    nblk = L // tb
    nch = tb // CHUNK
    tok = lambda width: pl.BlockSpec((1, tb, width), lambda bi, i, fl: (bi, i, 0))
    return pl.pallas_call(
        functools.partial(_gla_kernel, nblk=nblk, blk_per_flag=nblk * B // flags.shape[0]),
        grid_spec=pltpu.PrefetchScalarGridSpec(
            num_scalar_prefetch=1, grid=(B, nblk),
            in_specs=[tok(KW), tok(KW), tok(VW), tok(KW), tok(KW),
                      pl.BlockSpec((1, nch, KW, DV), lambda bi, i, fl: (bi, i, 0, 0)),
                      pl.BlockSpec((1, HEADS, DK, DV), lambda bi, i, fl: (bi, 0, 0, 0)),
                      tok(VW), pl.BlockSpec(g.shape, lambda bi, i, fl: (0, 0))],
            out_specs=tok(VW),
            scratch_shapes=[pltpu.VMEM((HEADS, DK, DV), F32)]
                           + [pltpu.VMEM((CHUNK, DK), F32)] * 3),
        out_shape=jax.ShapeDtypeStruct((B, L, VW), BF16),
        compiler_params=pltpu.CompilerParams(
            dimension_semantics=("parallel", "arbitrary"), vmem_limit_bytes=VMEM_LIMIT),
    )(flags, q, k, v, bf, bb, sb, s0, z, g)


def _out_kernel(a_ref, bm_ref, ga_ref, gb_ref, x_ref, gate_ref,
                wa_ref, wb_ref, wo_ref, fg_ref, o_ref):
    tm = x_ref.shape[1]
    sub = min(tm, SUB_TILE)
    cw = POOL_GW
    ncol = D_MODEL // cw

    def stages(r0):
        rows = slice(r0, r0 + sub)
        st = {"ya": [None] * ncol, "mg": [None] * ncol, "ss": []}

        def branch_a(c):
            def run():
                cs = slice(c * cw, (c + 1) * cw)
                y_a = jnp.dot(a_ref[0, rows, :], wa_ref[:, cs], preferred_element_type=F32)
                st["ya"][c] = ga_ref[0, rows, cs].astype(F32) * y_a
            return run

        def branch_b(c):
            def run():
                cs = slice(c * cw, (c + 1) * cw)
                y_b = jnp.dot(bm_ref[0, rows, :], wb_ref[:, cs], preferred_element_type=F32)
                st["mg"][c] = (st["ya"][c] + gb_ref[0, rows, cs].astype(F32) * y_b).astype(BF16)
            return run

        def output(c):
            def run():
                cs = slice(c * cw, (c + 1) * cw)
                if c == 0:
                    st["mg"] = jnp.concatenate(st["mg"], axis=1)
                y = jnp.dot(st["mg"], wo_ref[:, cs], preferred_element_type=F32)
                xn = x_ref[0, rows, cs] + gate_ref[0, :, cs] * y
                o_ref[0, rows, cs] = xn
                st["ss"].append(jnp.sum(xn * xn, axis=-1, keepdims=True))
            return run

        def normalise(c):
            def run():
                cs = slice(c * cw, (c + 1) * cw)
                ms = functools.reduce(jnp.add, st["ss"]) * (1.0 / D_MODEL)
                o_ref[0, rows, cs] = o_ref[0, rows, cs] * lax.rsqrt(ms + EPS) * fg_ref[:, cs]
            return run

        return [[f(c) for c in range(ncol)] for f in (branch_a, branch_b, output, normalise)]

    tiles = [stages(r0) for r0 in range(0, tm, sub)]
    nstage = len(tiles[0])
    for step in range(len(tiles) + nstage - 1):
        _interleave(*[tl[step - n] for n, tl in enumerate(tiles) if 0 <= step - n < nstage])


def _output_block(a, bm, ga, gb, x, mod3, w_a, w_b, w_o, final_g):
    B, L, _ = x.shape
    tm = OUT_TILE
    tok = pl.BlockSpec((1, tm, D_MODEL), lambda b, i: (b, i, 0))
    return pl.pallas_call(
        _out_kernel,
        grid=(B, L // tm),
        in_specs=[
            tok, tok, tok, tok, tok,
            pl.BlockSpec((1, 1, D_MODEL), lambda b, i: (b, 0, 2)),
            _resident(w_a), _resident(w_b), _resident(w_o), _resident(final_g),
        ],
        out_specs=tok,
        out_shape=jax.ShapeDtypeStruct((B, L, D_MODEL), F32),
        compiler_params=pltpu.CompilerParams(
            dimension_semantics=("parallel", "arbitrary"), vmem_limit_bytes=VMEM_LIMIT),
    )(a, bm, ga, gb, x, mod3, w_a, w_b, w_o, final_g)


def _shift_cast_kernel(a_ref, b_ref, o_ref, *, shift):
    both = jnp.concatenate([a_ref[0], b_ref[0]], axis=1)
    o_ref[...] = both[:, shift:shift + o_ref.shape[1]].astype(o_ref.dtype)


def _second_weight_block(w_in):
    _, rows, total = w_in.shape
    first = W_IN_FIRST + 2 * RANK
    width = total - first
    shift = first % LANES
    blk = 512
    assert width % blk == 0 and (first - shift) % blk == 0 and shift > 0
    return pl.pallas_call(
        functools.partial(_shift_cast_kernel, shift=shift),
        grid=(width // blk,),
        in_specs=[pl.BlockSpec((1, rows, blk), lambda j: (0, 0, (first - shift) // blk + j)),
                  pl.BlockSpec((1, rows, LANES),
                               lambda j: (0, 0, (first - shift + blk * (j + 1)) // LANES))],
        out_specs=pl.BlockSpec((rows, blk), lambda j: (0, j)),
        out_shape=jax.ShapeDtypeStruct((rows, width), BF16),
        compiler_params=pltpu.CompilerParams(vmem_limit_bytes=VMEM_LIMIT),
    )(w_in, w_in)


def _rope_tables(seq_len):
    nf = DK // 4
    t = np.arange(seq_len)
    freqs = ROPE_BASE ** (-np.arange(nf, dtype=np.float64) / nf)
    rowp, colp = (t // GRID_W)[:, None], (t % GRID_W)[:, None]
    ang = np.concatenate([rowp * freqs] * 2 + [colp * freqs] * 2, axis=1)
    cos, sin = np.cos(ang), np.sin(ang)
    first = (np.arange(LANES) % (2 * nf)) < nf
    sin_up = np.where(first, -sin, 0.0)
    sin_dn = np.where(first, 0.0, sin)
    return jnp.asarray(np.stack([cos, sin_up, sin_dn]).astype(np.float32))


def _flags(dmax):
    return (dmax[:, 0, 0] > SAFE_DECAY).astype(jnp.int32)


def kernel(x, c, ctx, c_ctx, w_mod, b_mod, norm_g, w_in, w_gate_up_f, b_gate_f, w_gate_up_b,
           b_gate_b, gla_norm_g, w_pool, pool_scale, w_branch_a, w_branch_b, w_out, final_norm_g):
    B, L, _ = x.shape
    assert w_mod.shape[0] == 1, "single-layer configuration"
    wi = w_in[0]
    w1 = wi[:, :W_IN_FIRST].astype(BF16)
    w2 = _second_weight_block(w_in)
    w_lr = jnp.pad(wi[:, W_IN_FIRST:W_IN_FIRST + 2 * RANK],
                   ((0, 0), (0, LANES - 2 * RANK))).astype(BF16)
    w_up = jnp.zeros((LANES, 2 * KW), F32)
    w_up = w_up.at[:RANK, :KW].set(w_gate_up_f[0]).at[RANK:2 * RANK, KW:].set(w_gate_up_b[0])
    w_up = w_up.astype(BF16)
    b_up = jnp.concatenate([b_gate_f[0], b_gate_b[0]])[None, :]

    c_rows = jnp.zeros((16, D_MODEL), F32).at[:B].set(c).at[B].set(c_ctx)
    mod = _modulation(c_rows, w_mod[0], b_mod[0][None, :])
    mod3 = mod[:, None, :]

    ng = norm_g[0][None, :]
    weights = (w1, w2, w_lr, w_up, b_up)

    *_, s_f, s_b = _project(ctx, mod3, lambda b: B, ng, None, weights, None, None, context=True)

    pool_w = (w_pool[0].astype(BF16), pool_scale[0][None, :])
    q, k, v, za, bm, ga, gb, bf, bb, dm, sb_states = _project(
        x, mod3, lambda b: b, ng, _rope_tables(L), weights, pool_w, s_b, context=False)
    a = _gla_fused(_flags(dm), q, k, v, bf, bb, sb_states, s_f, za, gla_norm_g[0][None, :])

    return _output_block(a, bm, ga, gb, x, mod3, w_branch_a[0].astype(BF16),
                         w_branch_b[0].astype(BF16), w_out[0].astype(BF16),
                         final_norm_g[None, :])
```

```python
import functools

import numpy as np
import jax
import jax.numpy as jnp
from jax import lax
from jax.experimental import pallas as pl
from jax.experimental.pallas import tpu as pltpu

F32 = jnp.float32
BF16 = jnp.bfloat16

D_MODEL = 1024
HEADS = 4
DK = 128
DV = 256
KW = HEADS * DK
VW = HEADS * DV
RANK = 16
GATE_NORM = 16.0
CHUNK = 128
GRID_W = 64
ROPE_BASE = 10000.0
POOL_WINDOWS = (2, 4, 8, 16)
POOL_GW = D_MODEL // len(POOL_WINDOWS)
EPS = 1e-6

LANES = 128
SUBLANES = 8
HALO = 16
GLA_TILE = 1024
PROJ_TILE = 512
OUT_TILE = 1024
SUB_TILE = 256
SAFE_DECAY = 60.0
VMEM_LIMIT = 56 * 1024 * 1024

_C_Q, _C_K, _C_V, _C_ZA = 0, 512, 1024, 2048
_C_UB, _C_ZB, _C_GA, _C_GB = 0, 1024, 2048, 3072
W_IN_FIRST = 3072


def _sigmoid(x):
    return 0.5 * jnp.tanh(0.5 * x) + 0.5


def _silu(x):
    h = 0.5 * x
    return h + h * jnp.tanh(h)


def _mod_kernel(c_ref, w_ref, b_ref, o_ref):
    c = c_ref[...]
    s = c * jax.nn.sigmoid(c)
    o_ref[...] = jnp.dot(s, w_ref[...], preferred_element_type=F32,
                         precision=lax.Precision.HIGHEST) + b_ref[...]


def _modulation(c_rows, w_mod, b_mod):
    return pl.pallas_call(
        _mod_kernel,
        out_shape=jax.ShapeDtypeStruct((c_rows.shape[0], w_mod.shape[1]), F32),
        compiler_params=pltpu.CompilerParams(vmem_limit_bytes=VMEM_LIMIT),
    )(c_rows, w_mod, b_mod)


def _as_column(row):
    return jnp.broadcast_to(row, (SUBLANES, DK)).T[:, 0:1]


def _state_step(state, hd, k, b, v, edge):
    b_edge = b[edge:edge + 1, :]
    kd_t = (k * jnp.exp(b_edge - b)).T.astype(BF16)
    state[hd] = (jnp.exp(_as_column(b_edge)) * state[hd]
                 + jnp.dot(kd_t, v, preferred_element_type=F32))


def _proj_kernel(x_ref, shift_ref, scale_ref, g_ref, *rest, context):
    if context:
        (w1_ref, wlr_ref, wup_ref, bup_ref, k_ref, v_ref, bf_ref, bb_ref, dmax_ref,
         sf_ref, sb_ref, state_f, state_b) = rest
    else:
        (tab_ref, w1_ref, w2_ref, wlr_ref, wup_ref, bup_ref, s0_ref,
         q_ref, k_ref, v_ref, za_ref, ub_ref, zb_ref, ga_ref, gb_ref,
         bf_ref, bb_ref, dmax_ref, sb_ref, state_b) = rest
    tm = x_ref.shape[1]
    sub = min(tm, SUB_TILE)
    worst = []

    if context:
        state_f[...] = jnp.zeros_like(state_f)
        state_b[...] = jnp.zeros_like(state_b)
    else:
        @pl.when(pl.program_id(1) == 0)
        def _():
            state_b[...] = s0_ref[0]

    ri = lax.broadcasted_iota(jnp.int32, (sub, sub), 0)
    ci = lax.broadcasted_iota(jnp.int32, (sub, sub), 1)
    same_chunk = (ri // CHUNK) == (ci // CHUNK)
    tri_f = jnp.where(same_chunk & (ci <= ri), 1.0, 0.0).astype(BF16)
    tri_b = jnp.where(same_chunk & (ci >= ri), 1.0, 0.0).astype(BF16)
    tri_f = jnp.concatenate([tri_f] * 3, axis=1)
    tri_b = jnp.concatenate([tri_b] * 3, axis=1)

    def stages(r0):
        rows = slice(r0, r0 + sub)
        st = {}

        def prologue():
            x = x_ref[0, rows, :]
            ms = jnp.mean(x * x, axis=-1, keepdims=True)
            h = (x * lax.rsqrt(ms + EPS)) * g_ref[...]
            h = h * (1.0 + scale_ref[0]) + shift_ref[0]
            st["hb"] = h.astype(BF16)
            lr = jnp.dot(st["hb"], wlr_ref[...], preferred_element_type=F32).astype(BF16)
            gk = jnp.dot(lr, wup_ref[...], preferred_element_type=F32) + bup_ref[...]
            la = (jnp.minimum(gk, 0.0) - jnp.log(1.0 + jnp.exp(-jnp.abs(gk)))) * (1.0 / GATE_NORM)
            hi = la.astype(BF16)
            r1 = la - hi.astype(F32)
            mid = r1.astype(BF16)
            lo = (r1 - mid.astype(F32)).astype(BF16)
            pieces = jnp.concatenate([hi, mid, lo], axis=0)
            cf = jnp.dot(tri_f, pieces[:, :KW], preferred_element_type=F32)
            cb = jnp.dot(tri_b, pieces[:, KW:], preferred_element_type=F32)
            bf_ref[0, rows, :] = cf
            bb_ref[0, rows, :] = cb
            for c0 in range(0, sub, CHUNK):
                worst.append(jnp.maximum(-cf[c0 + CHUNK - 1:c0 + CHUNK, :], -cb[c0:c0 + 1, :]))

        def plain(w_ref, c0, dst_ref, act=None):
            def run():
                p = jnp.dot(st["hb"], w_ref[:, c0:c0 + dst_ref.shape[2]],
                            preferred_element_type=F32)
                dst_ref[0, rows, :] = (p if act is None else act(p)).astype(dst_ref.dtype)
            return run

        def rotary(c0, dst_ref, scale):
            def run():
                p = jnp.dot(st["hb"], w1_ref[:, c0:c0 + KW], preferred_element_type=F32)
                cos, sin_up, sin_dn = (tab_ref[n, rows, :] for n in range(3))
                for hd in range(HEADS):
                    ph = p[:, hd * DK:(hd + 1) * DK]
                    out = (ph * cos + pltpu.roll(ph, LANES - 32, 1) * sin_up
                           + pltpu.roll(ph, 32, 1) * sin_dn)
                    if scale is not None:
                        out = out * scale
                    dst_ref[0, rows, hd * DK:(hd + 1) * DK] = out.astype(dst_ref.dtype)
            return run

        if context:
            return prologue, [plain(w1_ref, _C_K, k_ref), plain(w1_ref, _C_V, v_ref)]
        return prologue, [
            rotary(_C_K, k_ref, None), plain(w1_ref, _C_V, v_ref), rotary(_C_Q, q_ref, DK ** -0.5),
            plain(w1_ref, _C_ZA, za_ref, _silu), plain(w2_ref, _C_UB, ub_ref),
            plain(w2_ref, _C_ZB, zb_ref, _silu), plain(w2_ref, _C_GA, ga_ref, _sigmoid),
            plain(w2_ref, _C_GB, gb_ref, _sigmoid)]

    def scan_steps(r0, state, b_ref, reverse, emit):
        steps = []
        starts = range(r0, r0 + sub, CHUNK)
        for c0 in (reversed(starts) if reverse else starts):
            for hd in range(HEADS):
                def run(c0=c0, hd=hd):
                    rows = slice(c0, c0 + CHUNK)
                    ks = slice(hd * DK, (hd + 1) * DK)
                    if emit:
                        sb_ref[0, c0 // CHUNK, ks, :] = state[hd].astype(BF16)
                    _state_step(state, hd, k_ref[0, rows, ks].astype(F32), b_ref[0, rows, ks],
                                v_ref[0, rows, hd * DV:(hd + 1) * DV], 0 if reverse else CHUNK - 1)
                steps.append(run)
        return steps

    starts = list(reversed(range(0, tm, sub)))
    tiles = [stages(r0) for r0 in starts]
    tiles[0][0]()
    for n, (_, matmuls) in enumerate(tiles):
        for step in matmuls[:2]:
            step()
        fill = scan_steps(starts[n], state_b, bb_ref, True, not context)
        if context:
            fill += scan_steps(starts[n], state_f, bf_ref, False, False)
        if n + 1 < len(tiles):
            fill.append(tiles[n + 1][0])
        _interleave(matmuls[2:], fill)
    dmax_ref[0] = jnp.broadcast_to(
        jnp.max(functools.reduce(jnp.maximum, worst), axis=1, keepdims=True), dmax_ref.shape[1:])
    if context:
        sf_ref[0] = state_f[...]
        sb_ref[0] = state_b[...]


def _interleave(*lists):
    order = sorted(((i + 0.5) / len(lst), which, i)
                   for which, lst in enumerate(lists) for i in range(len(lst)))
    for _, which, idx in order:
        lists[which][idx]()


def _resident(arr):
    return pl.BlockSpec(arr.shape, lambda *_: (0,) * arr.ndim, pipeline_mode=pl.Buffered(1))


def _project(x, mod3, mod_row, norm_g, tables, weights, s0, *, context):
    B, L, _ = x.shape
    tm = min(L, PROJ_TILE)
    nt = L // tm
    tile = (lambda i: i) if context else (lambda i: nt - 1 - i)
    tok = lambda width: pl.BlockSpec((1, tm, width), lambda b, i: (b, tile(i), 0))
    bf = lambda width: jax.ShapeDtypeStruct((B, L, width), BF16)
    st_spec = pl.BlockSpec((1, HEADS, DK, DV), lambda b, i: (b, 0, 0, 0))
    st_shape = jax.ShapeDtypeStruct((B, HEADS, DK, DV), F32)
    state = pltpu.VMEM((HEADS, DK, DV), F32)
    if context:
        assert nt == 1, "context stream is one tile per sample"
        widths = [KW, VW]
        extra = [weights[0]] + list(weights[2:])
        extra_specs = [_resident(w) for w in extra]
        state_specs, state_shapes, scratch = [st_spec, st_spec], [st_shape, st_shape], [state, state]
    else:
        widths = [KW, KW, VW, VW, D_MODEL, D_MODEL, D_MODEL, D_MODEL]
        extra = [tables] + list(weights) + [s0]
        extra_specs = ([pl.BlockSpec((3, tm, LANES), lambda b, i: (0, tile(i), 0))]
                       + [_resident(w) for w in weights] + [st_spec])
        state_specs = [pl.BlockSpec((1, tm // CHUNK, KW, DV), lambda b, i: (b, tile(i), 0, 0))]
        state_shapes = [jax.ShapeDtypeStruct((B, L // CHUNK, KW, DV), BF16)]
        scratch = [state]
    return pl.pallas_call(
        functools.partial(_proj_kernel, context=context),
        grid=(B, nt),
        in_specs=[
            tok(D_MODEL),
            pl.BlockSpec((1, 1, D_MODEL), lambda b, i: (mod_row(b), 0, 0)),
            pl.BlockSpec((1, 1, D_MODEL), lambda b, i: (mod_row(b), 0, 1)),
            _resident(norm_g),
        ] + extra_specs,
        out_specs=[tok(w) for w in widths] + [
            tok(KW), tok(KW),
            pl.BlockSpec((1, SUBLANES, LANES), lambda b, i: (b * nt + tile(i), 0, 0))] + state_specs,
        out_shape=[bf(w) for w in widths] + [
            jax.ShapeDtypeStruct((B, L, KW), F32), jax.ShapeDtypeStruct((B, L, KW), F32),
            jax.ShapeDtypeStruct((B * nt, SUBLANES, LANES), F32)] + state_shapes,
        scratch_shapes=scratch,
        compiler_params=pltpu.CompilerParams(
            dimension_semantics=("parallel", "arbitrary"), vmem_limit_bytes=VMEM_LIMIT),
    )(x, mod3, mod3, norm_g, *extra)


def _gla_kernel(flags_ref, q_ref, k_ref, v_ref, bf_ref, bb_ref, sb_ref, s0_ref, z_ref, g_ref,
                out_ref, state, k_rows, bf_rows, bb_rows, *, nblk):
    bi = pl.program_id(0)
    i = pl.program_id(1)
    nch = q_ref.shape[1] // CHUNK

    @pl.when(i == 0)
    def _():
        state[...] = s0_ref[0]

    exact_path = flags_ref[bi * nblk + i] != 0
    row = lax.broadcasted_iota(jnp.int32, (CHUNK, CHUNK), 0)
    col = lax.broadcasted_iota(jnp.int32, (CHUNK, CHUNK), 1)
    nt_dims = (((1,), (1,)), ((), ()))

    def head_chunk(c, hd, exact):
        rows = slice(c * CHUNK, (c + 1) * CHUNK)
        ks = slice(hd * DK, (hd + 1) * DK)
        vs = slice(hd * DV, (hd + 1) * DV)
        q = q_ref[0, rows, ks].astype(F32)
        k = k_ref[0, rows, ks].astype(F32)
        v = v_ref[0, rows, vs]
        bf = bf_ref[0, rows, ks]
        bb = bb_ref[0, rows, ks]
        qf = (q * jnp.exp(bf)).astype(BF16)
        qb = (q * jnp.exp(bb)).astype(BF16)
        if not exact:
            kf = (k * jnp.exp(-bf)).astype(BF16)
            kb = (k * jnp.exp(-bb)).astype(BF16)
            sc = (jnp.where(col <= row, lax.dot_general(qf, kf, nt_dims,
                                                        preferred_element_type=F32), 0.0)
                  + jnp.where(col >= row, lax.dot_general(qb, kb, nt_dims,
                                                          preferred_element_type=F32), 0.0))
        else:
            k_rows[...] = k
            bf_rows[...] = bf
            bb_rows[...] = bb
            tok = lax.broadcasted_iota(jnp.int32, (CHUNK, DK), 0)

            def col_body(j, acc):
                dec = (jnp.where(tok >= j, jnp.exp(jnp.minimum(bf - bf_rows[pl.ds(j, 1), :], 0.0)), 0.0)
                       + jnp.where(tok <= j, jnp.exp(jnp.minimum(bb - bb_rows[pl.ds(j, 1), :], 0.0)), 0.0))
                w = q * k_rows[pl.ds(j, 1), :] * dec
                return jnp.where(col == j, jnp.sum(w, axis=1, keepdims=True), acc)

            sc = lax.fori_loop(0, CHUNK, col_body, jnp.zeros((CHUNK, CHUNK), F32))
        s_cat = jnp.concatenate([state[hd].astype(BF16), sb_ref[0, c, ks, :]], axis=0)
        q_cat = jnp.concatenate([qf, qb], axis=1)
        o = (jnp.dot(sc.astype(BF16), v, preferred_element_type=F32)
             + jnp.dot(q_cat, s_cat, preferred_element_type=F32))
        _state_step(state, hd, k, bf, v, CHUNK - 1)
        ms = jnp.mean(o * o, axis=-1, keepdims=True)
        on = o * lax.rsqrt(ms + EPS) * g_ref[...]
        out_ref[0, rows, vs] = (on * z_ref[0, rows, vs].astype(F32)).astype(out_ref.dtype)

    for exact in (False, True):
        @pl.when(exact_path == exact)
        def _():
            for c in range(nch):
                for hd in range(HEADS):
                    head_chunk(c, hd, exact)


def _gla_fused(flags, q, k, v, bf, bb, sb, s0, z, g):
    B, L, _ = q.shape
    tb = GLA_TILE
    nblk = L // tb
    nch = tb // CHUNK
    tok = lambda width: pl.BlockSpec((1, tb, width), lambda bi, i, fl: (bi, i, 0))
    return pl.pallas_call(
        functools.partial(_gla_kernel, nblk=nblk),
        grid_spec=pltpu.PrefetchScalarGridSpec(
            num_scalar_prefetch=1, grid=(B, nblk),
            in_specs=[tok(KW), tok(KW), tok(VW), tok(KW), tok(KW),
                      pl.BlockSpec((1, nch, KW, DV), lambda bi, i, fl: (bi, i, 0, 0)),
                      pl.BlockSpec((1, HEADS, DK, DV), lambda bi, i, fl: (bi, 0, 0, 0)),
                      tok(VW), pl.BlockSpec(g.shape, lambda bi, i, fl: (0, 0))],
            out_specs=tok(VW),
            scratch_shapes=[pltpu.VMEM((HEADS, DK, DV), F32)]
                           + [pltpu.VMEM((CHUNK, DK), F32)] * 3),
        out_shape=jax.ShapeDtypeStruct((B, L, VW), BF16),
        compiler_params=pltpu.CompilerParams(
            dimension_semantics=("parallel", "arbitrary"), vmem_limit_bytes=VMEM_LIMIT),
    )(flags, q, k, v, bf, bb, sb, s0, z, g)


def _out_kernel(a_ref, up_ref, uc_ref, un_ref, zb_ref, ga_ref, gb_ref, x_ref, gate_ref,
                wpool_ref, pscale_ref, wa_ref, wb_ref, wo_ref, fg_ref, o_ref, *, seq_len):
    i = pl.program_id(1)
    nt = pl.num_programs(1)
    tm = uc_ref.shape[1]
    sub = min(tm, SUB_TILE)
    ext = sub + 2 * HALO
    cw = POOL_GW
    ncol = D_MODEL // cw

    def stages(r0):
        rows = slice(r0, r0 + sub)
        st = {"bm": [None] * ncol, "ya": [None] * ncol, "mg": [None] * ncol, "ss": []}

        def pool_input(cs):
            if r0 == 0:
                before = jnp.where(i > 0, up_ref[0, :, cs].astype(F32), 0.0)
            else:
                before = uc_ref[0, r0 - HALO:r0, cs].astype(F32)
            if r0 + sub == tm:
                after = jnp.where(i < nt - 1, un_ref[0, :, cs].astype(F32), 0.0)
            else:
                after = uc_ref[0, r0 + sub:r0 + sub + HALO, cs].astype(F32)
            return jnp.concatenate([before, uc_ref[0, rows, cs].astype(F32), after], axis=0)

        def pool(gi):
            def run():
                w = POOL_WINDOWS[gi]
                cs = slice(gi * cw, (gi + 1) * cw)
                u_ext = pool_input(cs)
                lead = w // 2 - 1
                acc = pltpu.roll(u_ext, ext - lead, 0) if lead else u_ext
                s = 1
                while s < w:
                    acc = acc + pltpu.roll(acc, s, 0)
                    s *= 2
                t = i * tm + r0 + lax.broadcasted_iota(jnp.int32, (sub, LANES), 0)
                lo = jnp.maximum(t - w // 2, 0)
                hi = jnp.minimum(t + (w - 1 - w // 2), seq_len - 1)
                inv_cnt = 1.0 / (hi - lo + 1).astype(F32)
                inv_cnt = jnp.concatenate([inv_cnt] * (cw // LANES), axis=1)
                d = acc[HALO:HALO + sub] * inv_cnt - u_ext[HALO:HALO + sub]
                pooled = jnp.dot(d.astype(BF16), wpool_ref[gi], preferred_element_type=F32)
                st["bm"][gi] = (pooled * pscale_ref[:, cs]
                                * zb_ref[0, rows, cs].astype(F32)).astype(BF16)
            return run

        def branch_a(c):
            def run():
                cs = slice(c * cw, (c + 1) * cw)
                y_a = jnp.dot(a_ref[0, rows, :], wa_ref[:, cs], preferred_element_type=F32)
                st["ya"][c] = ga_ref[0, rows, cs].astype(F32) * y_a
            return run

        def branch_b(c):
            def run():
                cs = slice(c * cw, (c + 1) * cw)
                if c == 0:
                    st["bm"] = jnp.concatenate(st["bm"], axis=1)
                y_b = jnp.dot(st["bm"], wb_ref[:, cs], preferred_element_type=F32)
                st["mg"][c] = (st["ya"][c] + gb_ref[0, rows, cs].astype(F32) * y_b).astype(BF16)
            return run

        def output(c):
            def run():
                cs = slice(c * cw, (c + 1) * cw)
                if c == 0:
                    st["mg"] = jnp.concatenate(st["mg"], axis=1)
                y = jnp.dot(st["mg"], wo_ref[:, cs], preferred_element_type=F32)
                xn = x_ref[0, rows, cs] + gate_ref[0, :, cs] * y
                o_ref[0, rows, cs] = xn
                st["ss"].append(jnp.sum(xn * xn, axis=-1, keepdims=True))
            return run

        def normalise(c):
            def run():
                cs = slice(c * cw, (c + 1) * cw)
                ms = functools.reduce(jnp.add, st["ss"]) * (1.0 / D_MODEL)
                o_ref[0, rows, cs] = o_ref[0, rows, cs] * lax.rsqrt(ms + EPS) * fg_ref[:, cs]
            return run

        first = [f(c) for c in range(ncol) for f in (branch_a, pool)]
        return [first] + [[f(c) for c in range(ncol)] for f in (branch_b, output, normalise)]

    tiles = [stages(r0) for r0 in range(0, tm, sub)]
    nstage = len(tiles[0])
    for step in range(len(tiles) + nstage - 1):
        _interleave(*[tl[step - n] for n, tl in enumerate(tiles) if 0 <= step - n < nstage])


def _output_block(a, ub, zb, ga, gb, x, mod3, w_pool, pool_scale, w_a, w_b, w_o, final_g):
    B, L, _ = x.shape
    tm = OUT_TILE
    nt = L // tm
    per_halo = tm // HALO
    n_halo = L // HALO
    tok = pl.BlockSpec((1, tm, D_MODEL), lambda b, i: (b, i, 0))
    const = _resident
    return pl.pallas_call(
        functools.partial(_out_kernel, seq_len=L),
        grid=(B, nt),
        in_specs=[
            tok,
            pl.BlockSpec((1, HALO, D_MODEL), lambda b, i: (b, jnp.maximum(i * per_halo - 1, 0), 0)),
            tok,
            pl.BlockSpec((1, HALO, D_MODEL),
                         lambda b, i: (b, jnp.minimum((i + 1) * per_halo, n_halo - 1), 0)),
            tok, tok, tok, tok,
            pl.BlockSpec((1, 1, D_MODEL), lambda b, i: (b, 0, 2)),
            const(w_pool), const(pool_scale), const(w_a), const(w_b), const(w_o), const(final_g),
        ],
        out_specs=tok,
        out_shape=jax.ShapeDtypeStruct((B, L, D_MODEL), F32),
        compiler_params=pltpu.CompilerParams(
            dimension_semantics=("parallel", "arbitrary"), vmem_limit_bytes=VMEM_LIMIT),
    )(a, ub, ub, ub, zb, ga, gb, x, mod3, w_pool, pool_scale, w_a, w_b, w_o, final_g)


def _rope_tables(seq_len):
    nf = DK // 4
    t = np.arange(seq_len)
    freqs = ROPE_BASE ** (-np.arange(nf, dtype=np.float64) / nf)
    rowp, colp = (t // GRID_W)[:, None], (t % GRID_W)[:, None]
    ang = np.concatenate([rowp * freqs] * 2 + [colp * freqs] * 2, axis=1)
    cos, sin = np.cos(ang), np.sin(ang)
    first = (np.arange(LANES) % (2 * nf)) < nf
    sin_up = np.where(first, -sin, 0.0)
    sin_dn = np.where(first, 0.0, sin)
    return jnp.asarray(np.stack([cos, sin_up, sin_dn]).astype(np.float32))


def _flags(dmax, tiles_per_block):
    worst = jnp.max(dmax[:, 0, 0].reshape(-1, tiles_per_block), axis=1)
    return (worst > SAFE_DECAY).astype(jnp.int32)


def kernel(x, c, ctx, c_ctx, w_mod, b_mod, norm_g, w_in, w_gate_up_f, b_gate_f, w_gate_up_b,
           b_gate_b, gla_norm_g, w_pool, pool_scale, w_branch_a, w_branch_b, w_out, final_norm_g):
    B, L, _ = x.shape
    assert w_mod.shape[0] == 1, "single-layer configuration"
    wi = w_in[0]
    w1 = wi[:, :W_IN_FIRST].astype(BF16)
    w2 = wi[:, W_IN_FIRST + 2 * RANK:].astype(BF16)
    w_lr = jnp.pad(wi[:, W_IN_FIRST:W_IN_FIRST + 2 * RANK],
                   ((0, 0), (0, LANES - 2 * RANK))).astype(BF16)
    w_up = jnp.zeros((LANES, 2 * KW), F32)
    w_up = w_up.at[:RANK, :KW].set(w_gate_up_f[0]).at[RANK:2 * RANK, KW:].set(w_gate_up_b[0])
    w_up = w_up.astype(BF16)
    b_up = jnp.concatenate([b_gate_f[0], b_gate_b[0]])[None, :]

    c_rows = jnp.zeros((16, D_MODEL), F32).at[:B].set(c).at[B].set(c_ctx)
    mod = _modulation(c_rows, w_mod[0], b_mod[0][None, :])
    mod3 = mod[:, None, :]

    ng = norm_g[0][None, :]
    weights = (w1, w2, w_lr, w_up, b_up)

    *_, s_f, s_b = _project(ctx, mod3, lambda b: B, ng, None, weights, None, context=True)

    q, k, v, za, ub, zb, ga, gb, bf, bb, dm, sb_states = _project(
        x, mod3, lambda b: b, ng, _rope_tables(L), weights, s_b, context=False)
    a = _gla_fused(_flags(dm, GLA_TILE // PROJ_TILE), q, k, v, bf, bb, sb_states, s_f, za,
                   gla_norm_g[0][None, :])

    return _output_block(a, ub, zb, ga, gb, x, mod3, w_pool[0].astype(BF16),
                         pool_scale[0][None, :], w_branch_a[0].astype(BF16),
                         w_branch_b[0].astype(BF16), w_out[0].astype(BF16),
                         final_norm_g[None, :])
```

```python
import functools

import numpy as np
import jax
import jax.numpy as jnp
from jax import lax
from jax.experimental import pallas as pl
from jax.experimental.pallas import tpu as pltpu

F32 = jnp.float32
BF16 = jnp.bfloat16

D_MODEL = 1024
HEADS = 4
DK = 128
DV = 256
KW = HEADS * DK
VW = HEADS * DV
RANK = 16
GATE_NORM = 16.0
CHUNK = 128
GRID_W = 64
ROPE_BASE = 10000.0
POOL_WINDOWS = (2, 4, 8, 16)
POOL_GW = D_MODEL // len(POOL_WINDOWS)
EPS = 1e-6

LANES = 128
SUBLANES = 8
HALO = 16
PROJ_TILE = 512
TAIL_TILE = 512
SUB_TILE = 256
SAFE_DECAY = 60.0
VMEM_LIMIT = 56 * 1024 * 1024

_C_Q, _C_K, _C_V, _C_ZA = 0, 512, 1024, 2048
_C_UB, _C_ZB, _C_GA, _C_GB = 0, 1024, 2048, 3072
W_IN_FIRST = 3072


def _sigmoid(x):
    return 0.5 * jnp.tanh(0.5 * x) + 0.5


def _silu(x):
    h = 0.5 * x
    return h + h * jnp.tanh(h)


def _mod_kernel(c_ref, w_ref, b_ref, o_ref):
    c = c_ref[...]
    s = c * jax.nn.sigmoid(c)
    o_ref[...] = jnp.dot(s, w_ref[...], preferred_element_type=F32,
                         precision=lax.Precision.HIGHEST) + b_ref[...]


def _modulation(c_rows, w_mod, b_mod):
    return pl.pallas_call(
        _mod_kernel,
        out_shape=jax.ShapeDtypeStruct((c_rows.shape[0], w_mod.shape[1]), F32),
        compiler_params=pltpu.CompilerParams(vmem_limit_bytes=VMEM_LIMIT),
    )(c_rows, w_mod, b_mod)


def _as_column(row):
    return jnp.broadcast_to(row, (SUBLANES, DK)).T[:, 0:1]


def _state_step(state, hd, k, b, v, edge):
    b_edge = b[edge:edge + 1, :]
    kd_t = (k * jnp.exp(b_edge - b)).T.astype(BF16)
    state[hd] = (jnp.exp(_as_column(b_edge)) * state[hd]
                 + jnp.dot(kd_t, v, preferred_element_type=F32))


def _proj_kernel(x_ref, shift_ref, scale_ref, g_ref, *rest, context):
    if context:
        (w1_ref, wlr_ref, wup_ref, bup_ref, k_ref, v_ref, bf_ref, bb_ref, dmax_ref,
         sf_ref, sb_ref, state_f, state_b) = rest
    else:
        (tab_ref, w1_ref, w2_ref, wlr_ref, wup_ref, bup_ref, s0_ref,
         q_ref, k_ref, v_ref, za_ref, ub_ref, zb_ref, ga_ref, gb_ref,
         bf_ref, bb_ref, dmax_ref, sb_ref, state_b) = rest
    tm = x_ref.shape[1]
    sub = min(tm, SUB_TILE)
    worst = []

    if context:
        state_f[...] = jnp.zeros_like(state_f)
        state_b[...] = jnp.zeros_like(state_b)
    else:
        @pl.when(pl.program_id(1) == 0)
        def _():
            state_b[...] = s0_ref[0]

    ri = lax.broadcasted_iota(jnp.int32, (sub, sub), 0)
    ci = lax.broadcasted_iota(jnp.int32, (sub, sub), 1)
    same_chunk = (ri // CHUNK) == (ci // CHUNK)
    tri_f = jnp.where(same_chunk & (ci <= ri), 1.0, 0.0).astype(BF16)
    tri_b = jnp.where(same_chunk & (ci >= ri), 1.0, 0.0).astype(BF16)
    tri_f = jnp.concatenate([tri_f] * 3, axis=1)
    tri_b = jnp.concatenate([tri_b] * 3, axis=1)

    def stages(r0):
        rows = slice(r0, r0 + sub)
        st = {}

        def prologue():
            x = x_ref[0, rows, :]
            ms = jnp.mean(x * x, axis=-1, keepdims=True)
            h = (x * lax.rsqrt(ms + EPS)) * g_ref[...]
            h = h * (1.0 + scale_ref[0]) + shift_ref[0]
            st["hb"] = h.astype(BF16)
            lr = jnp.dot(st["hb"], wlr_ref[...], preferred_element_type=F32).astype(BF16)
            gk = jnp.dot(lr, wup_ref[...], preferred_element_type=F32) + bup_ref[...]
            la = (jnp.minimum(gk, 0.0) - jnp.log(1.0 + jnp.exp(-jnp.abs(gk)))) * (1.0 / GATE_NORM)
            hi = la.astype(BF16)
            r1 = la - hi.astype(F32)
            mid = r1.astype(BF16)
            lo = (r1 - mid.astype(F32)).astype(BF16)
            pieces = jnp.concatenate([hi, mid, lo], axis=0)
            cf = jnp.dot(tri_f, pieces[:, :KW], preferred_element_type=F32)
            cb = jnp.dot(tri_b, pieces[:, KW:], preferred_element_type=F32)
            bf_ref[0, rows, :] = cf
            bb_ref[0, rows, :] = cb
            for c0 in range(0, sub, CHUNK):
                worst.append(jnp.maximum(-cf[c0 + CHUNK - 1:c0 + CHUNK, :], -cb[c0:c0 + 1, :]))

        def plain(w_ref, c0, dst_ref, act=None):
            def run():
                p = jnp.dot(st["hb"], w_ref[:, c0:c0 + dst_ref.shape[2]],
                            preferred_element_type=F32)
                dst_ref[0, rows, :] = (p if act is None else act(p)).astype(dst_ref.dtype)
            return run

        def rotary(c0, dst_ref, scale):
            def run():
                p = jnp.dot(st["hb"], w1_ref[:, c0:c0 + KW], preferred_element_type=F32)
                cos, sin_up, sin_dn = (tab_ref[n, rows, :] for n in range(3))
                for hd in range(HEADS):
                    ph = p[:, hd * DK:(hd + 1) * DK]
                    out = (ph * cos + pltpu.roll(ph, LANES - 32, 1) * sin_up
                           + pltpu.roll(ph, 32, 1) * sin_dn)
                    if scale is not None:
                        out = out * scale
                    dst_ref[0, rows, hd * DK:(hd + 1) * DK] = out.astype(dst_ref.dtype)
            return run

        if context:
            return prologue, [plain(w1_ref, _C_K, k_ref), plain(w1_ref, _C_V, v_ref)]
        return prologue, [
            rotary(_C_K, k_ref, None), plain(w1_ref, _C_V, v_ref), rotary(_C_Q, q_ref, DK ** -0.5),
            plain(w1_ref, _C_ZA, za_ref, _silu), plain(w2_ref, _C_UB, ub_ref),
            plain(w2_ref, _C_ZB, zb_ref, _silu), plain(w2_ref, _C_GA, ga_ref, _sigmoid),
            plain(w2_ref, _C_GB, gb_ref, _sigmoid)]

    def scan_steps(r0, state, b_ref, reverse, emit):
        steps = []
        starts = range(r0, r0 + sub, CHUNK)
        for c0 in (reversed(starts) if reverse else starts):
            for hd in range(HEADS):
                def run(c0=c0, hd=hd):
                    rows = slice(c0, c0 + CHUNK)
                    ks = slice(hd * DK, (hd + 1) * DK)
                    if emit:
                        sb_ref[0, c0 // CHUNK, ks, :] = state[hd].astype(BF16)
                    _state_step(state, hd, k_ref[0, rows, ks].astype(F32), b_ref[0, rows, ks],
                                v_ref[0, rows, hd * DV:(hd + 1) * DV], 0 if reverse else CHUNK - 1)
                steps.append(run)
        return steps

    starts = list(reversed(range(0, tm, sub)))
    tiles = [stages(r0) for r0 in starts]
    tiles[0][0]()
    for n, (_, matmuls) in enumerate(tiles):
        for step in matmuls[:2]:
            step()
        fill = scan_steps(starts[n], state_b, bb_ref, True, not context)
        if context:
            fill += scan_steps(starts[n], state_f, bf_ref, False, False)
        if n + 1 < len(tiles):
            fill.append(tiles[n + 1][0])
        _interleave(matmuls[2:], fill)
    dmax_ref[0] = jnp.broadcast_to(
        jnp.max(functools.reduce(jnp.maximum, worst), axis=1, keepdims=True), dmax_ref.shape[1:])
    if context:
        sf_ref[0] = state_f[...]
        sb_ref[0] = state_b[...]


def _interleave(*lists):
    order = sorted(((i + 0.5) / len(lst), which, i)
                   for which, lst in enumerate(lists) for i in range(len(lst)))
    for _, which, idx in order:
        lists[which][idx]()


def _resident(arr):
    return pl.BlockSpec(arr.shape, lambda *_: (0,) * arr.ndim, pipeline_mode=pl.Buffered(1))


def _project(x, mod3, mod_row, norm_g, tables, weights, s0, *, context):
    B, L, _ = x.shape
    tm = min(L, PROJ_TILE)
    nt = L // tm
    tile = (lambda i: i) if context else (lambda i: nt - 1 - i)
    tok = lambda width: pl.BlockSpec((1, tm, width), lambda b, i: (b, tile(i), 0))
    bf = lambda width: jax.ShapeDtypeStruct((B, L, width), BF16)
    st_spec = pl.BlockSpec((1, HEADS, DK, DV), lambda b, i: (b, 0, 0, 0))
    st_shape = jax.ShapeDtypeStruct((B, HEADS, DK, DV), F32)
    state = pltpu.VMEM((HEADS, DK, DV), F32)
    if context:
        assert nt == 1, "context stream is one tile per sample"
        widths = [KW, VW]
        extra = [weights[0]] + list(weights[2:])
        extra_specs = [_resident(w) for w in extra]
        state_specs, state_shapes, scratch = [st_spec, st_spec], [st_shape, st_shape], [state, state]
    else:
        widths = [KW, KW, VW, VW, D_MODEL, D_MODEL, D_MODEL, D_MODEL]
        extra = [tables] + list(weights) + [s0]
        extra_specs = ([pl.BlockSpec((3, tm, LANES), lambda b, i: (0, tile(i), 0))]
                       + [_resident(w) for w in weights] + [st_spec])
        state_specs = [pl.BlockSpec((1, tm // CHUNK, KW, DV), lambda b, i: (b, tile(i), 0, 0))]
        state_shapes = [jax.ShapeDtypeStruct((B, L // CHUNK, KW, DV), BF16)]
        scratch = [state]
    return pl.pallas_call(
        functools.partial(_proj_kernel, context=context),
        grid=(B, nt),
        in_specs=[
            tok(D_MODEL),
            pl.BlockSpec((1, 1, D_MODEL), lambda b, i: (mod_row(b), 0, 0)),
            pl.BlockSpec((1, 1, D_MODEL), lambda b, i: (mod_row(b), 0, 1)),
            _resident(norm_g),
        ] + extra_specs,
        out_specs=[tok(w) for w in widths] + [
            tok(KW), tok(KW),
            pl.BlockSpec((1, SUBLANES, LANES), lambda b, i: (b * nt + tile(i), 0, 0))] + state_specs,
        out_shape=[bf(w) for w in widths] + [
            jax.ShapeDtypeStruct((B, L, KW), F32), jax.ShapeDtypeStruct((B, L, KW), F32),
            jax.ShapeDtypeStruct((B * nt, SUBLANES, LANES), F32)] + state_shapes,
        scratch_shapes=scratch,
        compiler_params=pltpu.CompilerParams(
            dimension_semantics=("parallel", "arbitrary"), vmem_limit_bytes=VMEM_LIMIT),
    )(x, mod3, mod3, norm_g, *extra)


def _tail_kernel(flags_ref, q_ref, k_ref, v_ref, bf_ref, bb_ref, sb_ref, s0_ref, z_ref, g_ref,
                 up_ref, uc_ref, un_ref, zb_ref, ga_ref, gb_ref, x_ref, gate_ref,
                 wpool_ref, pscale_ref, wa_ref, wb_ref, wo_ref, fg_ref,
                 o_ref, state, a_scr, k_rows, bf_rows, bb_rows, *, nblk, seq_len):
    bi = pl.program_id(0)
    i = pl.program_id(1)
    tm = q_ref.shape[1]
    nch = tm // CHUNK
    sub = min(tm, SUB_TILE)
    ext = sub + 2 * HALO
    cw = POOL_GW
    ncol = D_MODEL // cw

    @pl.when(i == 0)
    def _():
        state[...] = s0_ref[0]

    exact_path = flags_ref[bi * nblk + i] != 0
    row = lax.broadcasted_iota(jnp.int32, (CHUNK, CHUNK), 0)
    col = lax.broadcasted_iota(jnp.int32, (CHUNK, CHUNK), 1)
    nt_dims = (((1,), (1,)), ((), ()))

    def head_chunk(c, hd, exact):
        rows = slice(c * CHUNK, (c + 1) * CHUNK)
        ks = slice(hd * DK, (hd + 1) * DK)
        vs = slice(hd * DV, (hd + 1) * DV)
        q = q_ref[0, rows, ks].astype(F32)
        k = k_ref[0, rows, ks].astype(F32)
        v = v_ref[0, rows, vs]
        bf = bf_ref[0, rows, ks]
        bb = bb_ref[0, rows, ks]
        qf = (q * jnp.exp(bf)).astype(BF16)
        qb = (q * jnp.exp(bb)).astype(BF16)
        if not exact:
            kf = (k * jnp.exp(-bf)).astype(BF16)
            kb = (k * jnp.exp(-bb)).astype(BF16)
            sc = (jnp.where(col <= row, lax.dot_general(qf, kf, nt_dims,
                                                        preferred_element_type=F32), 0.0)
                  + jnp.where(col >= row, lax.dot_general(qb, kb, nt_dims,
                                                          preferred_element_type=F32), 0.0))
        else:
            k_rows[...] = k
            bf_rows[...] = bf
            bb_rows[...] = bb
            tok = lax.broadcasted_iota(jnp.int32, (CHUNK, DK), 0)

            def col_body(j, acc):
                dec = (jnp.where(tok >= j, jnp.exp(jnp.minimum(bf - bf_rows[pl.ds(j, 1), :], 0.0)), 0.0)
                       + jnp.where(tok <= j, jnp.exp(jnp.minimum(bb - bb_rows[pl.ds(j, 1), :], 0.0)), 0.0))
                w = q * k_rows[pl.ds(j, 1), :] * dec
                return jnp.where(col == j, jnp.sum(w, axis=1, keepdims=True), acc)

            sc = lax.fori_loop(0, CHUNK, col_body, jnp.zeros((CHUNK, CHUNK), F32))
        s_cat = jnp.concatenate([state[hd].astype(BF16), sb_ref[0, c, ks, :]], axis=0)
        q_cat = jnp.concatenate([qf, qb], axis=1)
        o = (jnp.dot(sc.astype(BF16), v, preferred_element_type=F32)
             + jnp.dot(q_cat, s_cat, preferred_element_type=F32))
        _state_step(state, hd, k, bf, v, CHUNK - 1)
        ms = jnp.mean(o * o, axis=-1, keepdims=True)
        on = o * lax.rsqrt(ms + EPS) * g_ref[...]
        a_scr[rows, vs] = (on * z_ref[0, rows, vs].astype(F32)).astype(a_scr.dtype)

    def out_stages(r0):
        rows = slice(r0, r0 + sub)
        st = {"bm": [None] * ncol, "ya": [None] * ncol, "mg": [None] * ncol, "ss": []}

        def pool_input(cs):
            if r0 == 0:
                before = jnp.where(i > 0, up_ref[0, :, cs].astype(F32), 0.0)
            else:
                before = uc_ref[0, r0 - HALO:r0, cs].astype(F32)
            if r0 + sub == tm:
                after = jnp.where(i < nblk - 1, un_ref[0, :, cs].astype(F32), 0.0)
            else:
                after = uc_ref[0, r0 + sub:r0 + sub + HALO, cs].astype(F32)
            return jnp.concatenate([before, uc_ref[0, rows, cs].astype(F32), after], axis=0)

        def pool(gi):
            def run():
                w = POOL_WINDOWS[gi]
                cs = slice(gi * cw, (gi + 1) * cw)
                u_ext = pool_input(cs)
                lead = w // 2 - 1
                acc = pltpu.roll(u_ext, ext - lead, 0) if lead else u_ext
                s = 1
                while s < w:
                    acc = acc + pltpu.roll(acc, s, 0)
                    s *= 2
                t = i * tm + r0 + lax.broadcasted_iota(jnp.int32, (sub, LANES), 0)
                lo = jnp.maximum(t - w // 2, 0)
                hi = jnp.minimum(t + (w - 1 - w // 2), seq_len - 1)
                inv_cnt = 1.0 / (hi - lo + 1).astype(F32)
                inv_cnt = jnp.concatenate([inv_cnt] * (cw // LANES), axis=1)
                d = acc[HALO:HALO + sub] * inv_cnt - u_ext[HALO:HALO + sub]
                pooled = jnp.dot(d.astype(BF16), wpool_ref[gi], preferred_element_type=F32)
                st["bm"][gi] = (pooled * pscale_ref[:, cs]
                                * zb_ref[0, rows, cs].astype(F32)).astype(BF16)
            return run

        def branch_a(c):
            def run():
                cs = slice(c * cw, (c + 1) * cw)
                y_a = jnp.dot(a_scr[rows, :], wa_ref[:, cs], preferred_element_type=F32)
                st["ya"][c] = ga_ref[0, rows, cs].astype(F32) * y_a
            return run

        def branch_b(c):
            def run():
                cs = slice(c * cw, (c + 1) * cw)
                if c == 0:
                    st["bm"] = jnp.concatenate(st["bm"], axis=1)
                y_b = jnp.dot(st["bm"], wb_ref[:, cs], preferred_element_type=F32)
                st["mg"][c] = (st["ya"][c] + gb_ref[0, rows, cs].astype(F32) * y_b).astype(BF16)
            return run

        def output(c):
            def run():
                cs = slice(c * cw, (c + 1) * cw)
                if c == 0:
                    st["mg"] = jnp.concatenate(st["mg"], axis=1)
                y = jnp.dot(st["mg"], wo_ref[:, cs], preferred_element_type=F32)
                xn = x_ref[0, rows, cs] + gate_ref[0, :, cs] * y
                o_ref[0, rows, cs] = xn
                st["ss"].append(jnp.sum(xn * xn, axis=-1, keepdims=True))
            return run

        def normalise(c):
            def run():
                cs = slice(c * cw, (c + 1) * cw)
                ms = functools.reduce(jnp.add, st["ss"]) * (1.0 / D_MODEL)
                o_ref[0, rows, cs] = o_ref[0, rows, cs] * lax.rsqrt(ms + EPS) * fg_ref[:, cs]
            return run

        first = [f(c) for c in range(ncol) for f in (branch_a, pool)]
        return [first] + [[f(c) for c in range(ncol)] for f in (branch_b, output, normalise)]

    def body(exact):
        per_sub = sub // CHUNK
        gla = [[functools.partial(head_chunk, c, hd, exact)
                for c in range(n * per_sub, (n + 1) * per_sub) for hd in range(HEADS)]
               for n in range(tm // sub)]
        tiles = [out_stages(r0) for r0 in range(0, tm, sub)]
        nstage = len(tiles[0])
        for step in gla[0]:
            step()
        for step in range(len(tiles) + nstage - 1):
            lists = [tl[step - n] for n, tl in enumerate(tiles) if 0 <= step - n < nstage]
            if step + 1 < len(gla):
                lists.append(gla[step + 1])
            _interleave(*lists)

    for exact in (False, True):
        @pl.when(exact_path == exact)
        def _():
            body(exact)


def _tail(flags, q, k, v, bf, bb, sb, s0, z, g, ub, zb, ga, gb, x, mod3,
          w_pool, pool_scale, w_a, w_b, w_o, final_g):
    B, L, _ = x.shape
    tm = TAIL_TILE
    nblk = L // tm
    per_halo = tm // HALO
    n_halo = L // HALO
    tok = lambda width: pl.BlockSpec((1, tm, width), lambda bi, i, fl: (bi, i, 0))
    return pl.pallas_call(
        functools.partial(_tail_kernel, nblk=nblk, seq_len=L),
        grid_spec=pltpu.PrefetchScalarGridSpec(
            num_scalar_prefetch=1, grid=(B, nblk),
            in_specs=[
                tok(KW), tok(KW), tok(VW), tok(KW), tok(KW),
                pl.BlockSpec((1, tm // CHUNK, KW, DV), lambda bi, i, fl: (bi, i, 0, 0)),
                pl.BlockSpec((1, HEADS, DK, DV), lambda bi, i, fl: (bi, 0, 0, 0)),
                tok(VW), _resident(g),
                pl.BlockSpec((1, HALO, D_MODEL),
                             lambda bi, i, fl: (bi, jnp.maximum(i * per_halo - 1, 0), 0)),
                tok(D_MODEL),
                pl.BlockSpec((1, HALO, D_MODEL),
                             lambda bi, i, fl: (bi, jnp.minimum((i + 1) * per_halo, n_halo - 1), 0)),
                tok(D_MODEL), tok(D_MODEL), tok(D_MODEL), tok(D_MODEL),
                pl.BlockSpec((1, 1, D_MODEL), lambda bi, i, fl: (bi, 0, 2)),
                _resident(w_pool), _resident(pool_scale), _resident(w_a), _resident(w_b),
                _resident(w_o), _resident(final_g),
            ],
            out_specs=tok(D_MODEL),
            scratch_shapes=[pltpu.VMEM((HEADS, DK, DV), F32), pltpu.VMEM((tm, VW), BF16)]
                           + [pltpu.VMEM((CHUNK, DK), F32)] * 3),
        out_shape=jax.ShapeDtypeStruct((B, L, D_MODEL), F32),
        compiler_params=pltpu.CompilerParams(
            dimension_semantics=("parallel", "arbitrary"), vmem_limit_bytes=VMEM_LIMIT),
    )(flags, q, k, v, bf, bb, sb, s0, z, g, ub, ub, ub, zb, ga, gb, x, mod3,
      w_pool, pool_scale, w_a, w_b, w_o, final_g)


def _rope_tables(seq_len):
    nf = DK // 4
    t = np.arange(seq_len)
    freqs = ROPE_BASE ** (-np.arange(nf, dtype=np.float64) / nf)
    rowp, colp = (t // GRID_W)[:, None], (t % GRID_W)[:, None]
    ang = np.concatenate([rowp * freqs] * 2 + [colp * freqs] * 2, axis=1)
    cos, sin = np.cos(ang), np.sin(ang)
    first = (np.arange(LANES) % (2 * nf)) < nf
    sin_up = np.where(first, -sin, 0.0)
    sin_dn = np.where(first, 0.0, sin)
    return jnp.asarray(np.stack([cos, sin_up, sin_dn]).astype(np.float32))


def _flags(dmax, tiles_per_block):
    worst = jnp.max(dmax[:, 0, 0].reshape(-1, tiles_per_block), axis=1)
    return (worst > SAFE_DECAY).astype(jnp.int32)


def kernel(x, c, ctx, c_ctx, w_mod, b_mod, norm_g, w_in, w_gate_up_f, b_gate_f, w_gate_up_b,
           b_gate_b, gla_norm_g, w_pool, pool_scale, w_branch_a, w_branch_b, w_out, final_norm_g):
    B, L, _ = x.shape
    assert w_mod.shape[0] == 1, "single-layer configuration"
    wi = w_in[0]
    w1 = wi[:, :W_IN_FIRST].astype(BF16)
    w2 = wi[:, W_IN_FIRST + 2 * RANK:].astype(BF16)
    w_lr = jnp.pad(wi[:, W_IN_FIRST:W_IN_FIRST + 2 * RANK],
                   ((0, 0), (0, LANES - 2 * RANK))).astype(BF16)
    w_up = jnp.zeros((LANES, 2 * KW), F32)
    w_up = w_up.at[:RANK, :KW].set(w_gate_up_f[0]).at[RANK:2 * RANK, KW:].set(w_gate_up_b[0])
    w_up = w_up.astype(BF16)
    b_up = jnp.concatenate([b_gate_f[0], b_gate_b[0]])[None, :]

    c_rows = jnp.zeros((16, D_MODEL), F32).at[:B].set(c).at[B].set(c_ctx)
    mod = _modulation(c_rows, w_mod[0], b_mod[0][None, :])
    mod3 = mod[:, None, :]

    ng = norm_g[0][None, :]
    weights = (w1, w2, w_lr, w_up, b_up)

    *_, s_f, s_b = _project(ctx, mod3, lambda b: B, ng, None, weights, None, context=True)

    q, k, v, za, ub, zb, ga, gb, bf, bb, dm, sb_states = _project(
        x, mod3, lambda b: b, ng, _rope_tables(L), weights, s_b, context=False)
    return _tail(_flags(dm, TAIL_TILE // PROJ_TILE), q, k, v, bf, bb, sb_states, s_f, za,
                 gla_norm_g[0][None, :], ub, zb, ga, gb, x, mod3, w_pool[0].astype(BF16),
                 pool_scale[0][None, :], w_branch_a[0].astype(BF16), w_branch_b[0].astype(BF16),
                 w_out[0].astype(BF16), final_norm_g[None, :])
```

```python
import functools

import numpy as np
import jax
import jax.numpy as jnp
from jax import lax
from jax.experimental import pallas as pl
from jax.experimental.pallas import tpu as pltpu

F32 = jnp.float32
BF16 = jnp.bfloat16

D_MODEL = 1024
HEADS = 4
DK = 128
DV = 256
KW = HEADS * DK
VW = HEADS * DV
RANK = 16
GATE_NORM = 16.0
CHUNK = 128
GRID_W = 64
ROPE_BASE = 10000.0
POOL_WINDOWS = (2, 4, 8, 16)
POOL_GW = D_MODEL // len(POOL_WINDOWS)
EPS = 1e-6

LANES = 128
SUBLANES = 8
HALO = 16
PROJ_TILE = 512
TAIL_TILE = 512
SUB_TILE = 256
SAFE_DECAY = 60.0
VMEM_LIMIT = 56 * 1024 * 1024

_C_Q, _C_K, _C_V, _C_ZA = 0, 512, 1024, 2048
_C_UB, _C_ZB, _C_GA, _C_GB = 0, 1024, 2048, 3072
W_IN_FIRST = 3072


def _sigmoid(x):
    return 0.5 * jnp.tanh(0.5 * x) + 0.5


def _silu(x):
    h = 0.5 * x
    return h + h * jnp.tanh(h)


def _mod_kernel(c_ref, w_ref, b_ref, o_ref):
    c = c_ref[...]
    s = c * jax.nn.sigmoid(c)
    o_ref[...] = jnp.dot(s, w_ref[...], preferred_element_type=F32,
                         precision=lax.Precision.HIGHEST) + b_ref[...]


def _modulation(c_rows, w_mod, b_mod):
    return pl.pallas_call(
        _mod_kernel,
        out_shape=jax.ShapeDtypeStruct((c_rows.shape[0], w_mod.shape[1]), F32),
        compiler_params=pltpu.CompilerParams(vmem_limit_bytes=VMEM_LIMIT),
    )(c_rows, w_mod, b_mod)


def _as_column(row):
    return jnp.broadcast_to(row, (SUBLANES, DK)).T[:, 0:1]


def _state_step(state, hd, k, b, v, edge):
    b_edge = b[edge:edge + 1, :]
    kd_t = (k * jnp.exp(b_edge - b)).T.astype(BF16)
    state[hd] = (jnp.exp(_as_column(b_edge)) * state[hd]
                 + jnp.dot(kd_t, v, preferred_element_type=F32))


def _proj_kernel(x_ref, shift_ref, scale_ref, g_ref, *rest, context):
    if context:
        (w1_ref, wlr_ref, wup_ref, bup_ref, k_ref, v_ref, bf_ref, bb_ref, dmax_ref,
         sf_ref, sb_ref, state_f, state_b) = rest
    else:
        (tab_ref, w1_ref, w2_ref, wlr_ref, wup_ref, bup_ref, s0_ref,
         q_ref, k_ref, v_ref, za_ref, ub_ref, zb_ref, ga_ref, gb_ref,
         bf_ref, bb_ref, dmax_ref, sb_ref, state_b) = rest
    tm = x_ref.shape[1]
    sub = min(tm, SUB_TILE)
    worst = []

    if context:
        state_f[...] = jnp.zeros_like(state_f)
        state_b[...] = jnp.zeros_like(state_b)
    else:
        @pl.when(pl.program_id(1) == 0)
        def _():
            state_b[...] = s0_ref[0]

    ri = lax.broadcasted_iota(jnp.int32, (sub, sub), 0)
    ci = lax.broadcasted_iota(jnp.int32, (sub, sub), 1)
    same_chunk = (ri // CHUNK) == (ci // CHUNK)
    tri_f = jnp.where(same_chunk & (ci <= ri), 1.0, 0.0).astype(BF16)
    tri_b = jnp.where(same_chunk & (ci >= ri), 1.0, 0.0).astype(BF16)
    tri_f = jnp.concatenate([tri_f] * 3, axis=1)
    tri_b = jnp.concatenate([tri_b] * 3, axis=1)

    def stages(r0):
        rows = slice(r0, r0 + sub)
        st = {}

        def prologue():
            x = x_ref[0, rows, :]
            ms = jnp.mean(x * x, axis=-1, keepdims=True)
            h = (x * lax.rsqrt(ms + EPS)) * g_ref[...]
            h = h * (1.0 + scale_ref[0]) + shift_ref[0]
            st["hb"] = h.astype(BF16)
            lr = jnp.dot(st["hb"], wlr_ref[...], preferred_element_type=F32).astype(BF16)
            gk = jnp.dot(lr, wup_ref[...], preferred_element_type=F32) + bup_ref[...]
            la = (jnp.minimum(gk, 0.0) - jnp.log(1.0 + jnp.exp(-jnp.abs(gk)))) * (1.0 / GATE_NORM)
            hi = la.astype(BF16)
            r1 = la - hi.astype(F32)
            mid = r1.astype(BF16)
            lo = (r1 - mid.astype(F32)).astype(BF16)
            pieces = jnp.concatenate([hi, mid, lo], axis=0)
            cf = jnp.dot(tri_f, pieces[:, :KW], preferred_element_type=F32)
            cb = jnp.dot(tri_b, pieces[:, KW:], preferred_element_type=F32)
            bf_ref[0, rows, :] = cf
            bb_ref[0, rows, :] = cb
            for c0 in range(0, sub, CHUNK):
                worst.append(jnp.maximum(-cf[c0 + CHUNK - 1:c0 + CHUNK, :], -cb[c0:c0 + 1, :]))

        def plain(w_ref, c0, dst_ref, act=None):
            def run():
                p = jnp.dot(st["hb"], w_ref[:, c0:c0 + dst_ref.shape[2]],
                            preferred_element_type=F32)
                dst_ref[0, rows, :] = (p if act is None else act(p)).astype(dst_ref.dtype)
            return run

        def rotary(c0, dst_ref, scale):
            def run():
                p = jnp.dot(st["hb"], w1_ref[:, c0:c0 + KW], preferred_element_type=F32)
                cos, sin_up, sin_dn = (tab_ref[n, rows, :] for n in range(3))
                for hd in range(HEADS):
                    ph = p[:, hd * DK:(hd + 1) * DK]
                    out = (ph * cos + pltpu.roll(ph, LANES - 32, 1) * sin_up
                           + pltpu.roll(ph, 32, 1) * sin_dn)
                    if scale is not None:
                        out = out * scale
                    dst_ref[0, rows, hd * DK:(hd + 1) * DK] = out.astype(dst_ref.dtype)
            return run

        if context:
            return prologue, [plain(w1_ref, _C_K, k_ref), plain(w1_ref, _C_V, v_ref)]
        return prologue, [
            rotary(_C_K, k_ref, None), plain(w1_ref, _C_V, v_ref), rotary(_C_Q, q_ref, DK ** -0.5),
            plain(w1_ref, _C_ZA, za_ref, _silu), plain(w2_ref, _C_UB, ub_ref),
            plain(w2_ref, _C_ZB, zb_ref, _silu), plain(w2_ref, _C_GA, ga_ref, _sigmoid),
            plain(w2_ref, _C_GB, gb_ref, _sigmoid)]

    def scan_steps(r0, state, b_ref, reverse, emit):
        steps = []
        starts = range(r0, r0 + sub, CHUNK)
        for c0 in (reversed(starts) if reverse else starts):
            for hd in range(HEADS):
                def run(c0=c0, hd=hd):
                    rows = slice(c0, c0 + CHUNK)
                    ks = slice(hd * DK, (hd + 1) * DK)
                    if emit:
                        sb_ref[0, c0 // CHUNK, ks, :] = state[hd].astype(BF16)
                    _state_step(state, hd, k_ref[0, rows, ks].astype(F32), b_ref[0, rows, ks],
                                v_ref[0, rows, hd * DV:(hd + 1) * DV], 0 if reverse else CHUNK - 1)
                steps.append(run)
        return steps

    starts = list(reversed(range(0, tm, sub)))
    tiles = [stages(r0) for r0 in starts]
    tiles[0][0]()
    for n, (_, matmuls) in enumerate(tiles):
        for step in matmuls[:2]:
            step()
        fill = scan_steps(starts[n], state_b, bb_ref, True, not context)
        if context:
            fill += scan_steps(starts[n], state_f, bf_ref, False, False)
        if n + 1 < len(tiles):
            fill.append(tiles[n + 1][0])
        _interleave(matmuls[2:], fill)
    dmax_ref[0] = jnp.broadcast_to(
        jnp.max(functools.reduce(jnp.maximum, worst), axis=1, keepdims=True), dmax_ref.shape[1:])
    if context:
        sf_ref[0] = state_f[...]
        sb_ref[0] = state_b[...]


def _interleave(*lists):
    order = sorted(((i + 0.5) / len(lst), which, i)
                   for which, lst in enumerate(lists) for i in range(len(lst)))
    for _, which, idx in order:
        lists[which][idx]()


def _resident(arr):
    return pl.BlockSpec(arr.shape, lambda *_: (0,) * arr.ndim, pipeline_mode=pl.Buffered(1))


def _project(x, mod3, mod_row, norm_g, tables, weights, s0, *, context):
    B, L, _ = x.shape
    tm = min(L, PROJ_TILE)
    nt = L // tm
    tile = (lambda i: i) if context else (lambda i: nt - 1 - i)
    tok = lambda width: pl.BlockSpec((1, tm, width), lambda b, i: (b, tile(i), 0))
    bf = lambda width: jax.ShapeDtypeStruct((B, L, width), BF16)
    st_spec = pl.BlockSpec((1, HEADS, DK, DV), lambda b, i: (b, 0, 0, 0))
    st_shape = jax.ShapeDtypeStruct((B, HEADS, DK, DV), F32)
    state = pltpu.VMEM((HEADS, DK, DV), F32)
    if context:
        assert nt == 1, "context stream is one tile per sample"
        widths = [KW, VW]
        extra = [weights[0]] + list(weights[2:])
        extra_specs = [_resident(w) for w in extra]
        state_specs, state_shapes, scratch = [st_spec, st_spec], [st_shape, st_shape], [state, state]
    else:
        widths = [KW, KW, VW, VW, D_MODEL, D_MODEL, D_MODEL, D_MODEL]
        extra = [tables] + list(weights) + [s0]
        extra_specs = ([pl.BlockSpec((3, tm, LANES), lambda b, i: (0, tile(i), 0))]
                       + [_resident(w) for w in weights] + [st_spec])
        state_specs = [pl.BlockSpec((1, tm // CHUNK, KW, DV), lambda b, i: (b, tile(i), 0, 0))]
        state_shapes = [jax.ShapeDtypeStruct((B, L // CHUNK, KW, DV), BF16)]
        scratch = [state]
    return pl.pallas_call(
        functools.partial(_proj_kernel, context=context),
        grid=(B, nt),
        in_specs=[
            tok(D_MODEL),
            pl.BlockSpec((1, 1, D_MODEL), lambda b, i: (mod_row(b), 0, 0)),
            pl.BlockSpec((1, 1, D_MODEL), lambda b, i: (mod_row(b), 0, 1)),
            _resident(norm_g),
        ] + extra_specs,
        out_specs=[tok(w) for w in widths] + [
            tok(KW), tok(KW),
            pl.BlockSpec((1, SUBLANES, LANES), lambda b, i: (b * nt + tile(i), 0, 0))] + state_specs,
        out_shape=[bf(w) for w in widths] + [
            jax.ShapeDtypeStruct((B, L, KW), F32), jax.ShapeDtypeStruct((B, L, KW), F32),
            jax.ShapeDtypeStruct((B * nt, SUBLANES, LANES), F32)] + state_shapes,
        scratch_shapes=scratch,
        compiler_params=pltpu.CompilerParams(
            dimension_semantics=("parallel", "arbitrary"), vmem_limit_bytes=VMEM_LIMIT),
    )(x, mod3, mod3, norm_g, *extra)


def _tail_kernel(flags_ref, q_ref, k_ref, v_ref, bf_ref, bb_ref, sb_ref, s0_ref, z_ref, g_ref,
                 up_ref, uc_ref, un_ref, zb_ref, ga_ref, gb_ref, x_ref, gate_ref,
                 wpool_ref, pscale_ref, wa_ref, wb_ref, wo_ref, fg_ref,
                 o_ref, state, a_new, a_old, k_rows, bf_rows, bb_rows, *, nblk, ntile, seq_len):
    step_id = pl.program_id(0)
    t_gla = jnp.minimum(step_id, ntile - 1)
    i = jnp.maximum(step_id - 1, 0) % nblk
    tm = q_ref.shape[1]
    sub = min(tm, SUB_TILE)
    ext = sub + 2 * HALO
    cw = POOL_GW
    ncol = D_MODEL // cw

    @pl.when(t_gla % nblk == 0)
    def _():
        state[...] = s0_ref[0]

    @pl.when(step_id == 0)
    def _():
        a_old[...] = jnp.zeros_like(a_old)

    exact_path = flags_ref[t_gla] != 0
    row = lax.broadcasted_iota(jnp.int32, (CHUNK, CHUNK), 0)
    col = lax.broadcasted_iota(jnp.int32, (CHUNK, CHUNK), 1)
    nt_dims = (((1,), (1,)), ((), ()))

    def head_chunk(c, hd, exact):
        rows = slice(c * CHUNK, (c + 1) * CHUNK)
        ks = slice(hd * DK, (hd + 1) * DK)
        vs = slice(hd * DV, (hd + 1) * DV)
        q = q_ref[0, rows, ks].astype(F32)
        k = k_ref[0, rows, ks].astype(F32)
        v = v_ref[0, rows, vs]
        bf = bf_ref[0, rows, ks]
        bb = bb_ref[0, rows, ks]
        qf = (q * jnp.exp(bf)).astype(BF16)
        qb = (q * jnp.exp(bb)).astype(BF16)
        if not exact:
            kf = (k * jnp.exp(-bf)).astype(BF16)
            kb = (k * jnp.exp(-bb)).astype(BF16)
            sc = (jnp.where(col <= row, lax.dot_general(qf, kf, nt_dims,
                                                        preferred_element_type=F32), 0.0)
                  + jnp.where(col >= row, lax.dot_general(qb, kb, nt_dims,
                                                          preferred_element_type=F32), 0.0))
        else:
            k_rows[...] = k
            bf_rows[...] = bf
            bb_rows[...] = bb
            tok = lax.broadcasted_iota(jnp.int32, (CHUNK, DK), 0)

            def col_body(j, acc):
                dec = (jnp.where(tok >= j, jnp.exp(jnp.minimum(bf - bf_rows[pl.ds(j, 1), :], 0.0)), 0.0)
                       + jnp.where(tok <= j, jnp.exp(jnp.minimum(bb - bb_rows[pl.ds(j, 1), :], 0.0)), 0.0))
                w = q * k_rows[pl.ds(j, 1), :] * dec
                return jnp.where(col == j, jnp.sum(w, axis=1, keepdims=True), acc)

            sc = lax.fori_loop(0, CHUNK, col_body, jnp.zeros((CHUNK, CHUNK), F32))
        s_cat = jnp.concatenate([state[hd].astype(BF16), sb_ref[0, c, ks, :]], axis=0)
        q_cat = jnp.concatenate([qf, qb], axis=1)
        o = (jnp.dot(sc.astype(BF16), v, preferred_element_type=F32)
             + jnp.dot(q_cat, s_cat, preferred_element_type=F32))
        _state_step(state, hd, k, bf, v, CHUNK - 1)
        ms = jnp.mean(o * o, axis=-1, keepdims=True)
        on = o * lax.rsqrt(ms + EPS) * g_ref[...]
        a_new[rows, vs] = (on * z_ref[0, rows, vs].astype(F32)).astype(a_new.dtype)

    def out_stages(r0):
        rows = slice(r0, r0 + sub)
        st = {"bm": [None] * ncol, "ya": [None] * ncol, "mg": [None] * ncol, "ss": []}

        def pool_input(cs):
            if r0 == 0:
                before = jnp.where(i > 0, up_ref[0, :, cs].astype(F32), 0.0)
            else:
                before = uc_ref[0, r0 - HALO:r0, cs].astype(F32)
            if r0 + sub == tm:
                after = jnp.where(i < nblk - 1, un_ref[0, :, cs].astype(F32), 0.0)
            else:
                after = uc_ref[0, r0 + sub:r0 + sub + HALO, cs].astype(F32)
            return jnp.concatenate([before, uc_ref[0, rows, cs].astype(F32), after], axis=0)

        def pool(gi):
            def run():
                w = POOL_WINDOWS[gi]
                cs = slice(gi * cw, (gi + 1) * cw)
                u_ext = pool_input(cs)
                lead = w // 2 - 1
                acc = pltpu.roll(u_ext, ext - lead, 0) if lead else u_ext
                s = 1
                while s < w:
                    acc = acc + pltpu.roll(acc, s, 0)
                    s *= 2
                t = i * tm + r0 + lax.broadcasted_iota(jnp.int32, (sub, LANES), 0)
                lo = jnp.maximum(t - w // 2, 0)
                hi = jnp.minimum(t + (w - 1 - w // 2), seq_len - 1)
                inv_cnt = 1.0 / (hi - lo + 1).astype(F32)
                inv_cnt = jnp.concatenate([inv_cnt] * (cw // LANES), axis=1)
                d = acc[HALO:HALO + sub] * inv_cnt - u_ext[HALO:HALO + sub]
                pooled = jnp.dot(d.astype(BF16), wpool_ref[gi], preferred_element_type=F32)
                st["bm"][gi] = (pooled * pscale_ref[:, cs]
                                * zb_ref[0, rows, cs].astype(F32)).astype(BF16)
            return run

        def branch_a(c):
            def run():
                cs = slice(c * cw, (c + 1) * cw)
                y_a = jnp.dot(a_old[rows, :], wa_ref[:, cs], preferred_element_type=F32)
                st["ya"][c] = ga_ref[0, rows, cs].astype(F32) * y_a
            return run

        def branch_b(c):
            def run():
                cs = slice(c * cw, (c + 1) * cw)
                if c == 0:
                    st["bm"] = jnp.concatenate(st["bm"], axis=1)
                y_b = jnp.dot(st["bm"], wb_ref[:, cs], preferred_element_type=F32)
                st["mg"][c] = (st["ya"][c] + gb_ref[0, rows, cs].astype(F32) * y_b).astype(BF16)
            return run

        def output(c):
            def run():
                cs = slice(c * cw, (c + 1) * cw)
                if c == 0:
                    st["mg"] = jnp.concatenate(st["mg"], axis=1)
                y = jnp.dot(st["mg"], wo_ref[:, cs], preferred_element_type=F32)
                xn = x_ref[0, rows, cs] + gate_ref[0, :, cs] * y
                o_ref[0, rows, cs] = xn
                st["ss"].append(jnp.sum(xn * xn, axis=-1, keepdims=True))
            return run

        def normalise(c):
            def run():
                cs = slice(c * cw, (c + 1) * cw)
                ms = functools.reduce(jnp.add, st["ss"]) * (1.0 / D_MODEL)
                o_ref[0, rows, cs] = o_ref[0, rows, cs] * lax.rsqrt(ms + EPS) * fg_ref[:, cs]
            return run

        first = [f(c) for c in range(ncol) for f in (branch_a, pool)]
        return [first] + [[f(c) for c in range(ncol)] for f in (branch_b, output, normalise)]

    def body(exact):
        gla = [functools.partial(head_chunk, c, hd, exact)
               for c in range(tm // CHUNK) for hd in range(HEADS)]
        tiles = [out_stages(r0) for r0 in range(0, tm, sub)]
        nstage = len(tiles[0])
        nstep = len(tiles) + nstage - 1
        for step in range(nstep):
            lists = [tl[step - n] for n, tl in enumerate(tiles) if 0 <= step - n < nstage]
            lists.append(gla[step * len(gla) // nstep:(step + 1) * len(gla) // nstep])
            _interleave(*lists)
        a_old[...] = a_new[...]

    for exact in (False, True):
        @pl.when(exact_path == exact)
        def _():
            body(exact)


def _tail(flags, q, k, v, bf, bb, sb, s0, z, g, ub, zb, ga, gb, x, mod3,
          w_pool, pool_scale, w_a, w_b, w_o, final_g):
    B, L, _ = x.shape
    tm = TAIL_TILE
    nblk = L // tm
    ntile = B * nblk
    per_halo = tm // HALO
    n_halo = L // HALO
    gla_at = lambda s: divmod(jnp.minimum(s, ntile - 1), nblk)
    out_at = lambda s: divmod(jnp.maximum(s - 1, 0), nblk)
    gla_tok = lambda width: pl.BlockSpec((1, tm, width), lambda s, fl: (*gla_at(s), 0))
    out_tok = pl.BlockSpec((1, tm, D_MODEL), lambda s, fl: (*out_at(s), 0))

    def halo(edge):
        return pl.BlockSpec((1, HALO, D_MODEL),
                            lambda s, fl: (out_at(s)[0], edge(out_at(s)[1]), 0))

    return pl.pallas_call(
        functools.partial(_tail_kernel, nblk=nblk, ntile=ntile, seq_len=L),
        grid_spec=pltpu.PrefetchScalarGridSpec(
            num_scalar_prefetch=1, grid=(ntile + 1,),
            in_specs=[
                gla_tok(KW), gla_tok(KW), gla_tok(VW), gla_tok(KW), gla_tok(KW),
                pl.BlockSpec((1, tm // CHUNK, KW, DV), lambda s, fl: (*gla_at(s), 0, 0)),
                pl.BlockSpec((1, HEADS, DK, DV), lambda s, fl: (gla_at(s)[0], 0, 0, 0)),
                gla_tok(VW), _resident(g),
                halo(lambda i: jnp.maximum(i * per_halo - 1, 0)),
                out_tok,
                halo(lambda i: jnp.minimum((i + 1) * per_halo, n_halo - 1)),
                out_tok, out_tok, out_tok, out_tok,
                pl.BlockSpec((1, 1, D_MODEL), lambda s, fl: (out_at(s)[0], 0, 2)),
                _resident(w_pool), _resident(pool_scale), _resident(w_a), _resident(w_b),
                _resident(w_o), _resident(final_g),
            ],
            out_specs=out_tok,
            scratch_shapes=[pltpu.VMEM((HEADS, DK, DV), F32), pltpu.VMEM((tm, VW), BF16),
                            pltpu.VMEM((tm, VW), BF16)]
                           + [pltpu.VMEM((CHUNK, DK), F32)] * 3),
        out_shape=jax.ShapeDtypeStruct((B, L, D_MODEL), F32),
        compiler_params=pltpu.CompilerParams(
            dimension_semantics=("arbitrary",), vmem_limit_bytes=VMEM_LIMIT),
    )(flags, q, k, v, bf, bb, sb, s0, z, g, ub, ub, ub, zb, ga, gb, x, mod3,
      w_pool, pool_scale, w_a, w_b, w_o, final_g)


def _rope_tables(seq_len):
    nf = DK // 4
    t = np.arange(seq_len)
    freqs = ROPE_BASE ** (-np.arange(nf, dtype=np.float64) / nf)
    rowp, colp = (t // GRID_W)[:, None], (t % GRID_W)[:, None]
    ang = np.concatenate([rowp * freqs] * 2 + [colp * freqs] * 2, axis=1)
    cos, sin = np.cos(ang), np.sin(ang)
    first = (np.arange(LANES) % (2 * nf)) < nf
    sin_up = np.where(first, -sin, 0.0)
    sin_dn = np.where(first, 0.0, sin)
    return jnp.asarray(np.stack([cos, sin_up, sin_dn]).astype(np.float32))


def _flags(dmax, tiles_per_block):
    worst = jnp.max(dmax[:, 0, 0].reshape(-1, tiles_per_block), axis=1)
    return (worst > SAFE_DECAY).astype(jnp.int32)


def kernel(x, c, ctx, c_ctx, w_mod, b_mod, norm_g, w_in, w_gate_up_f, b_gate_f, w_gate_up_b,
           b_gate_b, gla_norm_g, w_pool, pool_scale, w_branch_a, w_branch_b, w_out, final_norm_g):
    B, L, _ = x.shape
    assert w_mod.shape[0] == 1, "single-layer configuration"
    wi = w_in[0]
    w1 = wi[:, :W_IN_FIRST].astype(BF16)
    w2 = wi[:, W_IN_FIRST + 2 * RANK:].astype(BF16)
    w_lr = jnp.pad(wi[:, W_IN_FIRST:W_IN_FIRST + 2 * RANK],
                   ((0, 0), (0, LANES - 2 * RANK))).astype(BF16)
    w_up = jnp.zeros((LANES, 2 * KW), F32)
    w_up = w_up.at[:RANK, :KW].set(w_gate_up_f[0]).at[RANK:2 * RANK, KW:].set(w_gate_up_b[0])
    w_up = w_up.astype(BF16)
    b_up = jnp.concatenate([b_gate_f[0], b_gate_b[0]])[None, :]

    c_rows = jnp.zeros((16, D_MODEL), F32).at[:B].set(c).at[B].set(c_ctx)
    mod = _modulation(c_rows, w_mod[0], b_mod[0][None, :])
    mod3 = mod[:, None, :]

    ng = norm_g[0][None, :]
    weights = (w1, w2, w_lr, w_up, b_up)

    *_, s_f, s_b = _project(ctx, mod3, lambda b: B, ng, None, weights, None, context=True)

    q, k, v, za, ub, zb, ga, gb, bf, bb, dm, sb_states = _project(
        x, mod3, lambda b: b, ng, _rope_tables(L), weights, s_b, context=False)
    return _tail(_flags(dm, TAIL_TILE // PROJ_TILE), q, k, v, bf, bb, sb_states, s_f, za,
                 gla_norm_g[0][None, :], ub, zb, ga, gb, x, mod3, w_pool[0].astype(BF16),
                 pool_scale[0][None, :], w_branch_a[0].astype(BF16), w_branch_b[0].astype(BF16),
                 w_out[0].astype(BF16), final_norm_g[None, :])
```

```python
import functools

import numpy as np
import jax
import jax.numpy as jnp
from jax import lax
from jax.experimental import pallas as pl
from jax.experimental.pallas import tpu as pltpu

F32 = jnp.float32
BF16 = jnp.bfloat16

D_MODEL = 1024
HEADS = 4
DK = 128
DV = 256
KW = HEADS * DK
VW = HEADS * DV
RANK = 16
GATE_NORM = 16.0
CHUNK = 128
GRID_W = 64
ROPE_BASE = 10000.0
POOL_WINDOWS = (2, 4, 8, 16)
POOL_GW = D_MODEL // len(POOL_WINDOWS)
EPS = 1e-6

LANES = 128
SUBLANES = 8
HALO = 16
PROJ_TILE = 512
TAIL_TILE = 512
SUB_TILE = 256
SAFE_DECAY = 60.0
VMEM_LIMIT = 56 * 1024 * 1024

_C_Q, _C_K, _C_V, _C_ZA = 0, 512, 1024, 2048
_C_UB, _C_ZB, _C_GA, _C_GB = 0, 1024, 2048, 3072
W_IN_FIRST = 3072


def _sigmoid(x):
    return 0.5 * jnp.tanh(0.5 * x) + 0.5


def _silu(x):
    h = 0.5 * x
    return h + h * jnp.tanh(h)


def _mod_kernel(c_ref, w_ref, b_ref, o_ref):
    c = c_ref[...]
    s = c * jax.nn.sigmoid(c)
    o_ref[...] = jnp.dot(s, w_ref[...], preferred_element_type=F32,
                         precision=lax.Precision.HIGHEST) + b_ref[...]


def _modulation(c_rows, w_mod, b_mod):
    return pl.pallas_call(
        _mod_kernel,
        out_shape=jax.ShapeDtypeStruct((c_rows.shape[0], w_mod.shape[1]), F32),
        compiler_params=pltpu.CompilerParams(vmem_limit_bytes=VMEM_LIMIT),
    )(c_rows, w_mod, b_mod)


def _as_column(row):
    return jnp.broadcast_to(row, (SUBLANES, DK)).T[:, 0:1]


def _state_step(state, hd, k, b, v, edge):
    b_edge = b[edge:edge + 1, :]
    kd_t = (k * jnp.exp(b_edge - b)).T.astype(BF16)
    state[hd] = (jnp.exp(_as_column(b_edge)) * state[hd]
                 + jnp.dot(kd_t, v, preferred_element_type=F32))


def _proj_kernel(x_ref, shift_ref, scale_ref, g_ref, *rest, context):
    if context:
        (w1_ref, wlr_ref, wup_ref, bup_ref, tri_ref, k_ref, v_ref, bf_ref, bb_ref, dmax_ref,
         sf_ref, sb_ref, state_f, state_b) = rest
    else:
        (tab_ref, w1_ref, w2_ref, wlr_ref, wup_ref, bup_ref, tri_ref, s0_ref,
         q_ref, k_ref, v_ref, za_ref, ub_ref, zb_ref, ga_ref, gb_ref,
         bf_ref, bb_ref, dmax_ref, sb_ref, state_b) = rest
    tm = x_ref.shape[1]
    sub = min(tm, SUB_TILE)
    worst = []

    if context:
        state_f[...] = jnp.zeros_like(state_f)
        state_b[...] = jnp.zeros_like(state_b)
    else:
        @pl.when(pl.program_id(1) == 0)
        def _():
            state_b[...] = s0_ref[0]

    def stages(r0):
        rows = slice(r0, r0 + sub)
        st = {}

        def prologue():
            x = x_ref[0, rows, :]
            ms = jnp.mean(x * x, axis=-1, keepdims=True)
            h = (x * lax.rsqrt(ms + EPS)) * g_ref[...]
            h = h * (1.0 + scale_ref[0]) + shift_ref[0]
            st["hb"] = h.astype(BF16)
            lr = jnp.dot(st["hb"], wlr_ref[...], preferred_element_type=F32).astype(BF16)
            gk = jnp.dot(lr, wup_ref[...], preferred_element_type=F32) + bup_ref[...]
            la = (jnp.minimum(gk, 0.0) - jnp.log(1.0 + jnp.exp(-jnp.abs(gk)))) * (1.0 / GATE_NORM)
            hi = la.astype(BF16)
            lo = (la - hi.astype(F32)).astype(BF16)
            pieces = jnp.concatenate([hi, lo], axis=0)
            cf = jnp.dot(tri_ref[0], pieces[:, :KW], preferred_element_type=F32)
            cb = jnp.dot(tri_ref[1], pieces[:, KW:], preferred_element_type=F32)
            bf_ref[0, rows, :] = cf
            bb_ref[0, rows, :] = cb
            for c0 in range(0, sub, CHUNK):
                worst.append(jnp.maximum(-cf[c0 + CHUNK - 1:c0 + CHUNK, :], -cb[c0:c0 + 1, :]))

        def plain(w_ref, c0, dst_ref, act=None):
            def run():
                p = jnp.dot(st["hb"], w_ref[:, c0:c0 + dst_ref.shape[2]],
                            preferred_element_type=F32)
                dst_ref[0, rows, :] = (p if act is None else act(p)).astype(dst_ref.dtype)
            return run

        def rotary(c0, dst_ref, scale):
            def run():
                p = jnp.dot(st["hb"], w1_ref[:, c0:c0 + KW], preferred_element_type=F32)
                cos, sin_up, sin_dn = (tab_ref[n, rows, :] for n in range(3))
                for hd in range(HEADS):
                    ph = p[:, hd * DK:(hd + 1) * DK]
                    out = (ph * cos + pltpu.roll(ph, LANES - 32, 1) * sin_up
                           + pltpu.roll(ph, 32, 1) * sin_dn)
                    if scale is not None:
                        out = out * scale
                    dst_ref[0, rows, hd * DK:(hd + 1) * DK] = out.astype(dst_ref.dtype)
            return run

        if context:
            return prologue, [plain(w1_ref, _C_K, k_ref), plain(w1_ref, _C_V, v_ref)]
        return prologue, [
            rotary(_C_K, k_ref, None), plain(w1_ref, _C_V, v_ref), rotary(_C_Q, q_ref, DK ** -0.5),
            plain(w1_ref, _C_ZA, za_ref, _silu), plain(w2_ref, _C_UB, ub_ref),
            plain(w2_ref, _C_ZB, zb_ref, _silu), plain(w2_ref, _C_GA, ga_ref, _sigmoid),
            plain(w2_ref, _C_GB, gb_ref, _sigmoid)]

    def scan_steps(r0, state, b_ref, reverse, emit):
        steps = []
        starts = range(r0, r0 + sub, CHUNK)
        for c0 in (reversed(starts) if reverse else starts):
            for hd in range(HEADS):
                def run(c0=c0, hd=hd):
                    rows = slice(c0, c0 + CHUNK)
                    ks = slice(hd * DK, (hd + 1) * DK)
                    if emit:
                        sb_ref[0, c0 // CHUNK, ks, :] = state[hd].astype(BF16)
                    _state_step(state, hd, k_ref[0, rows, ks].astype(F32), b_ref[0, rows, ks],
                                v_ref[0, rows, hd * DV:(hd + 1) * DV], 0 if reverse else CHUNK - 1)
                steps.append(run)
        return steps

    starts = list(reversed(range(0, tm, sub)))
    tiles = [stages(r0) for r0 in starts]
    tiles[0][0]()
    for n, (_, matmuls) in enumerate(tiles):
        for step in matmuls[:2]:
            step()
        fill = scan_steps(starts[n], state_b, bb_ref, True, not context)
        if context:
            fill += scan_steps(starts[n], state_f, bf_ref, False, False)
        if n + 1 < len(tiles):
            fill.append(tiles[n + 1][0])
        _interleave(matmuls[2:], fill)
    dmax_ref[0] = jnp.broadcast_to(
        jnp.max(functools.reduce(jnp.maximum, worst), axis=1, keepdims=True), dmax_ref.shape[1:])
    if context:
        sf_ref[0] = state_f[...]
        sb_ref[0] = state_b[...]


def _interleave(*lists):
    order = sorted(((i + 0.5) / len(lst), which, i)
                   for which, lst in enumerate(lists) for i in range(len(lst)))
    for _, which, idx in order:
        lists[which][idx]()


def _resident(arr):
    return pl.BlockSpec(arr.shape, lambda *_: (0,) * arr.ndim, pipeline_mode=pl.Buffered(1))


def _project(x, mod3, mod_row, norm_g, tables, weights, s0, *, context):
    B, L, _ = x.shape
    tm = min(L, PROJ_TILE)
    nt = L // tm
    tile = (lambda i: i) if context else (lambda i: nt - 1 - i)
    tok = lambda width: pl.BlockSpec((1, tm, width), lambda b, i: (b, tile(i), 0))
    bf = lambda width: jax.ShapeDtypeStruct((B, L, width), BF16)
    st_spec = pl.BlockSpec((1, HEADS, DK, DV), lambda b, i: (b, 0, 0, 0))
    st_shape = jax.ShapeDtypeStruct((B, HEADS, DK, DV), F32)
    state = pltpu.VMEM((HEADS, DK, DV), F32)
    if context:
        assert nt == 1, "context stream is one tile per sample"
        widths = [KW, VW]
        extra = [weights[0]] + list(weights[2:])
        extra_specs = [_resident(w) for w in extra]
        state_specs, state_shapes, scratch = [st_spec, st_spec], [st_shape, st_shape], [state, state]
    else:
        widths = [KW, KW, VW, VW, D_MODEL, D_MODEL, D_MODEL, D_MODEL]
        extra = [tables] + list(weights) + [s0]
        extra_specs = ([pl.BlockSpec((3, tm, LANES), lambda b, i: (0, tile(i), 0))]
                       + [_resident(w) for w in weights] + [st_spec])
        state_specs = [pl.BlockSpec((1, tm // CHUNK, KW, DV), lambda b, i: (b, tile(i), 0, 0))]
        state_shapes = [jax.ShapeDtypeStruct((B, L // CHUNK, KW, DV), BF16)]
        scratch = [state]
    return pl.pallas_call(
        functools.partial(_proj_kernel, context=context),
        grid=(B, nt),
        in_specs=[
            tok(D_MODEL),
            pl.BlockSpec((1, 1, D_MODEL), lambda b, i: (mod_row(b), 0, 0)),
            pl.BlockSpec((1, 1, D_MODEL), lambda b, i: (mod_row(b), 0, 1)),
            _resident(norm_g),
        ] + extra_specs,
        out_specs=[tok(w) for w in widths] + [
            tok(KW), tok(KW),
            pl.BlockSpec((1, SUBLANES, LANES), lambda b, i: (b * nt + tile(i), 0, 0))] + state_specs,
        out_shape=[bf(w) for w in widths] + [
            jax.ShapeDtypeStruct((B, L, KW), F32), jax.ShapeDtypeStruct((B, L, KW), F32),
            jax.ShapeDtypeStruct((B * nt, SUBLANES, LANES), F32)] + state_shapes,
        scratch_shapes=scratch,
        compiler_params=pltpu.CompilerParams(
            dimension_semantics=("parallel", "arbitrary"), vmem_limit_bytes=VMEM_LIMIT),
    )(x, mod3, mod3, norm_g, *extra)


def _tail_kernel(flags_ref, q_ref, k_ref, v_ref, bf_ref, bb_ref, sb_ref, s0_ref, z_ref, g_ref,
                 up_ref, uc_ref, un_ref, zb_ref, ga_ref, gb_ref, x_ref, gate_ref,
                 wpool_ref, pscale_ref, wa_ref, wb_ref, wo_ref, fg_ref,
                 o_ref, state, a_scr, k_rows, bf_rows, bb_rows, *, nblk, seq_len):
    bi = pl.program_id(0)
    i = pl.program_id(1)
    tm = q_ref.shape[1]
    nch = tm // CHUNK
    sub = min(tm, SUB_TILE)
    ext = sub + 2 * HALO
    cw = POOL_GW
    ncol = D_MODEL // cw

    @pl.when(i == 0)
    def _():
        state[...] = s0_ref[0]

    exact_path = flags_ref[bi * nblk + i] != 0
    row = lax.broadcasted_iota(jnp.int32, (CHUNK, CHUNK), 0)
    col = lax.broadcasted_iota(jnp.int32, (CHUNK, CHUNK), 1)
    nt_dims = (((1,), (1,)), ((), ()))

    def head_chunk(c, hd, exact):
        rows = slice(c * CHUNK, (c + 1) * CHUNK)
        ks = slice(hd * DK, (hd + 1) * DK)
        vs = slice(hd * DV, (hd + 1) * DV)
        q = q_ref[0, rows, ks].astype(F32)
        k = k_ref[0, rows, ks].astype(F32)
        v = v_ref[0, rows, vs]
        bf = bf_ref[0, rows, ks]
        bb = bb_ref[0, rows, ks]
        qf = (q * jnp.exp(bf)).astype(BF16)
        qb = (q * jnp.exp(bb)).astype(BF16)
        if not exact:
            kf = (k * jnp.exp(-bf)).astype(BF16)
            kb = (k * jnp.exp(-bb)).astype(BF16)
            sc = (jnp.where(col <= row, lax.dot_general(qf, kf, nt_dims,
                                                        preferred_element_type=F32), 0.0)
                  + jnp.where(col >= row, lax.dot_general(qb, kb, nt_dims,
                                                          preferred_element_type=F32), 0.0))
        else:
            k_rows[...] = k
            bf_rows[...] = bf
            bb_rows[...] = bb
            tok = lax.broadcasted_iota(jnp.int32, (CHUNK, DK), 0)

            def col_body(j, acc):
                dec = (jnp.where(tok >= j, jnp.exp(jnp.minimum(bf - bf_rows[pl.ds(j, 1), :], 0.0)), 0.0)
                       + jnp.where(tok <= j, jnp.exp(jnp.minimum(bb - bb_rows[pl.ds(j, 1), :], 0.0)), 0.0))
                w = q * k_rows[pl.ds(j, 1), :] * dec
                return jnp.where(col == j, jnp.sum(w, axis=1, keepdims=True), acc)

            sc = lax.fori_loop(0, CHUNK, col_body, jnp.zeros((CHUNK, CHUNK), F32))
        s_cat = jnp.concatenate([state[hd].astype(BF16), sb_ref[0, c, ks, :]], axis=0)
        q_cat = jnp.concatenate([qf, qb], axis=1)
        o = (jnp.dot(sc.astype(BF16), v, preferred_element_type=F32)
             + jnp.dot(q_cat, s_cat, preferred_element_type=F32))
        _state_step(state, hd, k, bf, v, CHUNK - 1)
        ms = jnp.mean(o * o, axis=-1, keepdims=True)
        on = o * lax.rsqrt(ms + EPS) * g_ref[...]
        a_scr[rows, vs] = (on * z_ref[0, rows, vs].astype(F32)).astype(a_scr.dtype)

    def out_stages(r0):
        rows = slice(r0, r0 + sub)
        st = {"bm": [None] * ncol, "ya": [None] * ncol, "mg": [None] * ncol, "ss": []}

        def pool_input(cs):
            if r0 == 0:
                before = jnp.where(i > 0, up_ref[0, :, cs].astype(F32), 0.0)
            else:
                before = uc_ref[0, r0 - HALO:r0, cs].astype(F32)
            if r0 + sub == tm:
                after = jnp.where(i < nblk - 1, un_ref[0, :, cs].astype(F32), 0.0)
            else:
                after = uc_ref[0, r0 + sub:r0 + sub + HALO, cs].astype(F32)
            return jnp.concatenate([before, uc_ref[0, rows, cs].astype(F32), after], axis=0)

        def pool(gi):
            def run():
                w = POOL_WINDOWS[gi]
                cs = slice(gi * cw, (gi + 1) * cw)
                u_ext = pool_input(cs)
                lead = w // 2 - 1
                acc = pltpu.roll(u_ext, ext - lead, 0) if lead else u_ext
                s = 1
                while s < w:
                    acc = acc + pltpu.roll(acc, s, 0)
                    s *= 2
                t = i * tm + r0 + lax.broadcasted_iota(jnp.int32, (sub, LANES), 0)
                lo = jnp.maximum(t - w // 2, 0)
                hi = jnp.minimum(t + (w - 1 - w // 2), seq_len - 1)
                inv_cnt = 1.0 / (hi - lo + 1).astype(F32)
                inv_cnt = jnp.concatenate([inv_cnt] * (cw // LANES), axis=1)
                d = acc[HALO:HALO + sub] * inv_cnt - u_ext[HALO:HALO + sub]
                pooled = jnp.dot(d.astype(BF16), wpool_ref[gi], preferred_element_type=F32)
                st["bm"][gi] = (pooled * pscale_ref[:, cs]
                                * zb_ref[0, rows, cs].astype(F32)).astype(BF16)
            return run

        def branch_a(c):
            def run():
                cs = slice(c * cw, (c + 1) * cw)
                y_a = jnp.dot(a_scr[rows, :], wa_ref[:, cs], preferred_element_type=F32)
                st["ya"][c] = ga_ref[0, rows, cs].astype(F32) * y_a
            return run

        def branch_b(c):
            def run():
                cs = slice(c * cw, (c + 1) * cw)
                if c == 0:
                    st["bm"] = jnp.concatenate(st["bm"], axis=1)
                y_b = jnp.dot(st["bm"], wb_ref[:, cs], preferred_element_type=F32)
                st["mg"][c] = (st["ya"][c] + gb_ref[0, rows, cs].astype(F32) * y_b).astype(BF16)
            return run

        def output(c):
            def run():
                cs = slice(c * cw, (c + 1) * cw)
                if c == 0:
                    st["mg"] = jnp.concatenate(st["mg"], axis=1)
                y = jnp.dot(st["mg"], wo_ref[:, cs], preferred_element_type=F32)
                xn = x_ref[0, rows, cs] + gate_ref[0, :, cs] * y
                o_ref[0, rows, cs] = xn
                st["ss"].append(jnp.sum(xn * xn, axis=-1, keepdims=True))
            return run

        def normalise(c):
            def run():
                cs = slice(c * cw, (c + 1) * cw)
                ms = functools.reduce(jnp.add, st["ss"]) * (1.0 / D_MODEL)
                o_ref[0, rows, cs] = o_ref[0, rows, cs] * lax.rsqrt(ms + EPS) * fg_ref[:, cs]
            return run

        first = [f(c) for c in range(ncol) for f in (branch_a, pool)]
        return [first] + [[f(c) for c in range(ncol)] for f in (branch_b, output, normalise)]

    def body(exact):
        per_sub = sub // CHUNK
        gla = [[functools.partial(head_chunk, c, hd, exact)
                for c in range(n * per_sub, (n + 1) * per_sub) for hd in range(HEADS)]
               for n in range(tm // sub)]
        tiles = [out_stages(r0) for r0 in range(0, tm, sub)]
        nstage = len(tiles[0])
        for step in gla[0]:
            step()
        for step in range(len(tiles) + nstage - 1):
            lists = [tl[step - n] for n, tl in enumerate(tiles) if 0 <= step - n < nstage]
            if step + 1 < len(gla):
                lists.append(gla[step + 1])
            _interleave(*lists)

    for exact in (False, True):
        @pl.when(exact_path == exact)
        def _():
            body(exact)


def _tail(flags, q, k, v, bf, bb, sb, s0, z, g, ub, zb, ga, gb, x, mod3,
          w_pool, pool_scale, w_a, w_b, w_o, final_g):
    B, L, _ = x.shape
    tm = TAIL_TILE
    nblk = L // tm
    per_halo = tm // HALO
    n_halo = L // HALO
    tok = lambda width: pl.BlockSpec((1, tm, width), lambda bi, i, fl: (bi, i, 0))
    return pl.pallas_call(
        functools.partial(_tail_kernel, nblk=nblk, seq_len=L),
        grid_spec=pltpu.PrefetchScalarGridSpec(
            num_scalar_prefetch=1, grid=(B, nblk),
            in_specs=[
                tok(KW), tok(KW), tok(VW), tok(KW), tok(KW),
                pl.BlockSpec((1, tm // CHUNK, KW, DV), lambda bi, i, fl: (bi, i, 0, 0)),
                pl.BlockSpec((1, HEADS, DK, DV), lambda bi, i, fl: (bi, 0, 0, 0)),
                tok(VW), _resident(g),
                pl.BlockSpec((1, HALO, D_MODEL),
                             lambda bi, i, fl: (bi, jnp.maximum(i * per_halo - 1, 0), 0)),
                tok(D_MODEL),
                pl.BlockSpec((1, HALO, D_MODEL),
                             lambda bi, i, fl: (bi, jnp.minimum((i + 1) * per_halo, n_halo - 1), 0)),
                tok(D_MODEL), tok(D_MODEL), tok(D_MODEL), tok(D_MODEL),
                pl.BlockSpec((1, 1, D_MODEL), lambda bi, i, fl: (bi, 0, 2)),
                _resident(w_pool), _resident(pool_scale), _resident(w_a), _resident(w_b),
                _resident(w_o), _resident(final_g),
            ],
            out_specs=tok(D_MODEL),
            scratch_shapes=[pltpu.VMEM((HEADS, DK, DV), F32), pltpu.VMEM((tm, VW), BF16)]
                           + [pltpu.VMEM((CHUNK, DK), F32)] * 3),
        out_shape=jax.ShapeDtypeStruct((B, L, D_MODEL), F32),
        compiler_params=pltpu.CompilerParams(
            dimension_semantics=("parallel", "arbitrary"), vmem_limit_bytes=VMEM_LIMIT),
    )(flags, q, k, v, bf, bb, sb, s0, z, g, ub, ub, ub, zb, ga, gb, x, mod3,
      w_pool, pool_scale, w_a, w_b, w_o, final_g)


def _rope_tables(seq_len):
    nf = DK // 4
    t = np.arange(seq_len)
    freqs = ROPE_BASE ** (-np.arange(nf, dtype=np.float64) / nf)
    rowp, colp = (t // GRID_W)[:, None], (t % GRID_W)[:, None]
    ang = np.concatenate([rowp * freqs] * 2 + [colp * freqs] * 2, axis=1)
    cos, sin = np.cos(ang), np.sin(ang)
    first = (np.arange(LANES) % (2 * nf)) < nf
    sin_up = np.where(first, -sin, 0.0)
    sin_dn = np.where(first, 0.0, sin)
    return jnp.asarray(np.stack([cos, sin_up, sin_dn]).astype(np.float32))


def _chunk_sum_matrices():
    ri = np.arange(SUB_TILE)[:, None]
    ci = np.arange(SUB_TILE)[None, :]
    same = (ri // CHUNK) == (ci // CHUNK)
    mats = [np.tile(same & cond, (1, 2)) for cond in (ci <= ri, ci >= ri)]
    return jnp.asarray(np.stack(mats).astype(np.float32), dtype=BF16)


def _flags(dmax, tiles_per_block):
    worst = jnp.max(dmax[:, 0, 0].reshape(-1, tiles_per_block), axis=1)
    return (worst > SAFE_DECAY).astype(jnp.int32)


def kernel(x, c, ctx, c_ctx, w_mod, b_mod, norm_g, w_in, w_gate_up_f, b_gate_f, w_gate_up_b,
           b_gate_b, gla_norm_g, w_pool, pool_scale, w_branch_a, w_branch_b, w_out, final_norm_g):
    B, L, _ = x.shape
    assert w_mod.shape[0] == 1, "single-layer configuration"
    wi = w_in[0]
    wi = wi.astype(BF16)
    w1 = wi[:, :W_IN_FIRST]
    w2 = wi[:, W_IN_FIRST + 2 * RANK:]
    w_lr = jnp.pad(wi[:, W_IN_FIRST:W_IN_FIRST + 2 * RANK], ((0, 0), (0, LANES - 2 * RANK)))
    w_up = jnp.zeros((LANES, 2 * KW), F32)
    w_up = w_up.at[:RANK, :KW].set(w_gate_up_f[0]).at[RANK:2 * RANK, KW:].set(w_gate_up_b[0])
    w_up = w_up.astype(BF16)
    b_up = jnp.concatenate([b_gate_f[0], b_gate_b[0]])[None, :]

    c_rows = jnp.zeros((16, D_MODEL), F32).at[:B].set(c).at[B].set(c_ctx)
    mod = _modulation(c_rows, w_mod[0], b_mod[0][None, :])
    mod3 = mod[:, None, :]

    ng = norm_g[0][None, :]
    weights = (w1, w2, w_lr, w_up, b_up, _chunk_sum_matrices())

    *_, s_f, s_b = _project(ctx, mod3, lambda b: B, ng, None, weights, None, context=True)

    q, k, v, za, ub, zb, ga, gb, bf, bb, dm, sb_states = _project(
        x, mod3, lambda b: b, ng, _rope_tables(L), weights, s_b, context=False)
    return _tail(_flags(dm, TAIL_TILE // PROJ_TILE), q, k, v, bf, bb, sb_states, s_f, za,
                 gla_norm_g[0][None, :], ub, zb, ga, gb, x, mod3, w_pool[0].astype(BF16),
                 pool_scale[0][None, :], w_branch_a[0].astype(BF16), w_branch_b[0].astype(BF16),
                 w_out[0].astype(BF16), final_norm_g[None, :])
```

```python
import functools

import numpy as np
import jax
import jax.numpy as jnp
from jax import lax
from jax.experimental import pallas as pl
from jax.experimental.pallas import tpu as pltpu

F32 = jnp.float32
BF16 = jnp.bfloat16

D_MODEL = 1024
HEADS = 4
DK = 128
DV = 256
KW = HEADS * DK
VW = HEADS * DV
RANK = 16
GATE_NORM = 16.0
CHUNK = 128
GRID_W = 64
ROPE_BASE = 10000.0
POOL_WINDOWS = (2, 4, 8, 16)
POOL_GW = D_MODEL // len(POOL_WINDOWS)
EPS = 1e-6

LANES = 128
SUBLANES = 8
V7X_VMEM_BYTES = 64 * 1024 * 1024
ROT_HALF = DK // 4
MOD_ROWS = 16
HALO = 16
PROJ_TILE = 512
TAIL_TILE = 512
SUB_TILE = 256
SAFE_DECAY = 60.0
VMEM_LIMIT = V7X_VMEM_BYTES * 7 // 8

_C_Q, _C_K, _C_V, _C_ZA = 0, KW, 2 * KW, 2 * KW + VW
_C_UB, _C_ZB, _C_GA, _C_GB = 0, D_MODEL, 2 * D_MODEL, 3 * D_MODEL
W_IN_FIRST = 2 * KW + 2 * VW


def _sigmoid(x):
    return 0.5 * jnp.tanh(0.5 * x) + 0.5


def _silu(x):
    h = 0.5 * x
    return h + h * jnp.tanh(h)


def _mod_kernel(c_ref, w_ref, b_ref, o_ref):
    c = c_ref[...]
    s = c * jax.nn.sigmoid(c)
    o_ref[...] = jnp.dot(s, w_ref[...], preferred_element_type=F32,
                         precision=lax.Precision.HIGHEST) + b_ref[...]


def _modulation(c_rows, w_mod, b_mod):
    return pl.pallas_call(
        _mod_kernel,
        out_shape=jax.ShapeDtypeStruct((c_rows.shape[0], w_mod.shape[1]), F32),
        compiler_params=pltpu.CompilerParams(vmem_limit_bytes=VMEM_LIMIT),
    )(c_rows, w_mod, b_mod)


def _as_column(row):
    return jnp.broadcast_to(row, (SUBLANES, DK)).T[:, 0:1]


def _state_step(state, hd, k, b, v, edge):
    b_edge = b[edge:edge + 1, :]
    kd_t = (k * jnp.exp(b_edge - b)).T.astype(BF16)
    state[hd] = (jnp.exp(_as_column(b_edge)) * state[hd]
                 + jnp.dot(kd_t, v, preferred_element_type=F32))


def _proj_kernel(x_ref, shift_ref, scale_ref, g_ref, *rest, context):
    if context:
        (w1_ref, wlr_ref, wup_ref, bup_ref, tri_ref, k_ref, v_ref, bf_ref, bb_ref, dmax_ref,
         sf_ref, sb_ref, state_f, state_b) = rest
    else:
        (tab_ref, w1_ref, w2_ref, wlr_ref, wup_ref, bup_ref, tri_ref, s0_ref,
         q_ref, k_ref, v_ref, za_ref, ub_ref, zb_ref, ga_ref, gb_ref,
         bf_ref, bb_ref, dmax_ref, sb_ref, state_b) = rest
    tm = x_ref.shape[1]
    sub = min(tm, SUB_TILE)
    worst = []

    if context:
        state_f[...] = jnp.zeros_like(state_f)
        state_b[...] = jnp.zeros_like(state_b)
    else:
        @pl.when(pl.program_id(1) == 0)
        def _():
            state_b[...] = s0_ref[0]

    def stages(r0):
        rows = slice(r0, r0 + sub)
        st = {}

        def prologue():
            x = x_ref[0, rows, :]
            ms = jnp.mean(x * x, axis=-1, keepdims=True)
            h = (x * lax.rsqrt(ms + EPS)) * g_ref[...]
            h = h * (1.0 + scale_ref[0]) + shift_ref[0]
            st["hb"] = h.astype(BF16)
            lr = jnp.dot(st["hb"], wlr_ref[...], preferred_element_type=F32).astype(BF16)
            gk = jnp.dot(lr, wup_ref[...], preferred_element_type=F32) + bup_ref[...]
            la = (jnp.minimum(gk, 0.0) - jnp.log(1.0 + jnp.exp(-jnp.abs(gk)))) * (1.0 / GATE_NORM)
            hi = la.astype(BF16)
            lo = (la - hi.astype(F32)).astype(BF16)
            pieces = jnp.concatenate([hi, lo], axis=0)
            cf = jnp.dot(tri_ref[0], pieces[:, :KW], preferred_element_type=F32)
            cb = jnp.dot(tri_ref[1], pieces[:, KW:], preferred_element_type=F32)
            bf_ref[0, rows, :] = cf
            bb_ref[0, rows, :] = cb
            for c0 in range(0, sub, CHUNK):
                worst.append(jnp.maximum(-cf[c0 + CHUNK - 1:c0 + CHUNK, :], -cb[c0:c0 + 1, :]))

        def plain(w_ref, c0, dst_ref, act=None):
            def run():
                p = jnp.dot(st["hb"], w_ref[:, c0:c0 + dst_ref.shape[2]],
                            preferred_element_type=F32)
                dst_ref[0, rows, :] = (p if act is None else act(p)).astype(dst_ref.dtype)
            return run

        def rotary(c0, dst_ref, scale):
            def run():
                p = jnp.dot(st["hb"], w1_ref[:, c0:c0 + KW], preferred_element_type=F32)
                cos, sin_up, sin_dn = (tab_ref[n, rows, :] for n in range(3))
                for hd in range(HEADS):
                    ph = p[:, hd * DK:(hd + 1) * DK]
                    out = (ph * cos + pltpu.roll(ph, LANES - ROT_HALF, 1) * sin_up
                           + pltpu.roll(ph, ROT_HALF, 1) * sin_dn)
                    if scale is not None:
                        out = out * scale
                    dst_ref[0, rows, hd * DK:(hd + 1) * DK] = out.astype(dst_ref.dtype)
            return run

        if context:
            return prologue, [plain(w1_ref, _C_K, k_ref), plain(w1_ref, _C_V, v_ref)]
        return prologue, [
            rotary(_C_K, k_ref, None), plain(w1_ref, _C_V, v_ref), rotary(_C_Q, q_ref, DK ** -0.5),
            plain(w1_ref, _C_ZA, za_ref, _silu), plain(w2_ref, _C_UB, ub_ref),
            plain(w2_ref, _C_ZB, zb_ref, _silu), plain(w2_ref, _C_GA, ga_ref, _sigmoid),
            plain(w2_ref, _C_GB, gb_ref, _sigmoid)]

    def scan_steps(r0, state, b_ref, reverse, emit):
        steps = []
        starts = range(r0, r0 + sub, CHUNK)
        for c0 in (reversed(starts) if reverse else starts):
            for hd in range(HEADS):
                def run(c0=c0, hd=hd):
                    rows = slice(c0, c0 + CHUNK)
                    ks = slice(hd * DK, (hd + 1) * DK)
                    if emit:
                        sb_ref[0, c0 // CHUNK, ks, :] = state[hd].astype(BF16)
                    _state_step(state, hd, k_ref[0, rows, ks].astype(F32), b_ref[0, rows, ks],
                                v_ref[0, rows, hd * DV:(hd + 1) * DV], 0 if reverse else CHUNK - 1)
                steps.append(run)
        return steps

    starts = list(reversed(range(0, tm, sub)))
    tiles = [stages(r0) for r0 in starts]
    tiles[0][0]()
    for n, (_, matmuls) in enumerate(tiles):
        for step in matmuls[:2]:
            step()
        fill = scan_steps(starts[n], state_b, bb_ref, True, not context)
        if context:
            fill += scan_steps(starts[n], state_f, bf_ref, False, False)
        if n + 1 < len(tiles):
            fill.append(tiles[n + 1][0])
        _interleave(matmuls[2:], fill)
    dmax_ref[0] = jnp.broadcast_to(
        jnp.max(functools.reduce(jnp.maximum, worst), axis=1, keepdims=True), dmax_ref.shape[1:])
    if context:
        sf_ref[0] = state_f[...]
        sb_ref[0] = state_b[...]


def _interleave(*lists):
    order = sorted(((i + 0.5) / len(lst), which, i)
                   for which, lst in enumerate(lists) for i in range(len(lst)))
    for _, which, idx in order:
        lists[which][idx]()


def _resident(arr):
    return pl.BlockSpec(arr.shape, lambda *_: (0,) * arr.ndim, pipeline_mode=pl.Buffered(1))


def _project(x, mod3, mod_row, norm_g, tables, weights, s0, *, context):
    B, L, _ = x.shape
    tm = min(L, PROJ_TILE)
    nt = L // tm
    tile = (lambda i: i) if context else (lambda i: nt - 1 - i)
    tok = lambda width: pl.BlockSpec((1, tm, width), lambda b, i: (b, tile(i), 0))
    bf = lambda width: jax.ShapeDtypeStruct((B, L, width), BF16)
    st_spec = pl.BlockSpec((1, HEADS, DK, DV), lambda b, i: (b, 0, 0, 0))
    st_shape = jax.ShapeDtypeStruct((B, HEADS, DK, DV), F32)
    state = pltpu.VMEM((HEADS, DK, DV), F32)
    if context:
        assert nt == 1, "context stream is one tile per sample"
        widths = [KW, VW]
        extra = [weights[0]] + list(weights[2:])
        extra_specs = [_resident(w) for w in extra]
        state_specs, state_shapes, scratch = [st_spec, st_spec], [st_shape, st_shape], [state, state]
    else:
        widths = [KW, KW, VW, VW, D_MODEL, D_MODEL, D_MODEL, D_MODEL]
        extra = [tables] + list(weights) + [s0]
        extra_specs = ([pl.BlockSpec((3, tm, LANES), lambda b, i: (0, tile(i), 0))]
                       + [_resident(w) for w in weights] + [st_spec])
        state_specs = [pl.BlockSpec((1, tm // CHUNK, KW, DV), lambda b, i: (b, tile(i), 0, 0))]
        state_shapes = [jax.ShapeDtypeStruct((B, L // CHUNK, KW, DV), BF16)]
        scratch = [state]
    return pl.pallas_call(
        functools.partial(_proj_kernel, context=context),
        grid=(B, nt),
        in_specs=[
            tok(D_MODEL),
            pl.BlockSpec((1, 1, D_MODEL), lambda b, i: (mod_row(b), 0, 0)),
            pl.BlockSpec((1, 1, D_MODEL), lambda b, i: (mod_row(b), 0, 1)),
            _resident(norm_g),
        ] + extra_specs,
        out_specs=[tok(w) for w in widths] + [
            tok(KW), tok(KW),
            pl.BlockSpec((1, SUBLANES, LANES), lambda b, i: (b * nt + tile(i), 0, 0))] + state_specs,
        out_shape=[bf(w) for w in widths] + [
            jax.ShapeDtypeStruct((B, L, KW), F32), jax.ShapeDtypeStruct((B, L, KW), F32),
            jax.ShapeDtypeStruct((B * nt, SUBLANES, LANES), F32)] + state_shapes,
        scratch_shapes=scratch,
        compiler_params=pltpu.CompilerParams(
            dimension_semantics=("parallel", "arbitrary"), vmem_limit_bytes=VMEM_LIMIT),
    )(x, mod3, mod3, norm_g, *extra)


def _tail_kernel(flags_ref, q_ref, k_ref, v_ref, bf_ref, bb_ref, sb_ref, s0_ref, z_ref, g_ref,
                 up_ref, uc_ref, un_ref, zb_ref, ga_ref, gb_ref, x_ref, gate_ref,
                 wpool_ref, pscale_ref, wa_ref, wb_ref, wo_ref, fg_ref,
                 o_ref, state, a_scr, k_rows, bf_rows, bb_rows, *, nblk, seq_len):
    bi = pl.program_id(0)
    i = pl.program_id(1)
    tm = q_ref.shape[1]
    sub = min(tm, SUB_TILE)
    ext = sub + 2 * HALO
    cw = POOL_GW
    ncol = D_MODEL // cw

    @pl.when(i == 0)
    def _():
        state[...] = s0_ref[0]

    exact_path = flags_ref[bi * nblk + i] != 0
    row = lax.broadcasted_iota(jnp.int32, (CHUNK, CHUNK), 0)
    col = lax.broadcasted_iota(jnp.int32, (CHUNK, CHUNK), 1)
    nt_dims = (((1,), (1,)), ((), ()))

    def head_chunk(c, hd, exact):
        rows = slice(c * CHUNK, (c + 1) * CHUNK)
        ks = slice(hd * DK, (hd + 1) * DK)
        vs = slice(hd * DV, (hd + 1) * DV)
        q = q_ref[0, rows, ks].astype(F32)
        k = k_ref[0, rows, ks].astype(F32)
        v = v_ref[0, rows, vs]
        bf = bf_ref[0, rows, ks]
        bb = bb_ref[0, rows, ks]
        qf = (q * jnp.exp(bf)).astype(BF16)
        qb = (q * jnp.exp(bb)).astype(BF16)
        if not exact:
            kf = (k * jnp.exp(-bf)).astype(BF16)
            kb = (k * jnp.exp(-bb)).astype(BF16)
            sc = (jnp.where(col <= row, lax.dot_general(qf, kf, nt_dims,
                                                        preferred_element_type=F32), 0.0)
                  + jnp.where(col >= row, lax.dot_general(qb, kb, nt_dims,
                                                          preferred_element_type=F32), 0.0))
        else:
            k_rows[...] = k
            bf_rows[...] = bf
            bb_rows[...] = bb
            tok = lax.broadcasted_iota(jnp.int32, (CHUNK, DK), 0)

            def col_body(j, acc):
                dec = (jnp.where(tok >= j, jnp.exp(jnp.minimum(bf - bf_rows[pl.ds(j, 1), :], 0.0)), 0.0)
                       + jnp.where(tok <= j, jnp.exp(jnp.minimum(bb - bb_rows[pl.ds(j, 1), :], 0.0)), 0.0))
                w = q * k_rows[pl.ds(j, 1), :] * dec
                return jnp.where(col == j, jnp.sum(w, axis=1, keepdims=True), acc)

            sc = lax.fori_loop(0, CHUNK, col_body, jnp.zeros((CHUNK, CHUNK), F32))
        s_cat = jnp.concatenate([state[hd].astype(BF16), sb_ref[0, c, ks, :]], axis=0)
        q_cat = jnp.concatenate([qf, qb], axis=1)
        o = (jnp.dot(sc.astype(BF16), v, preferred_element_type=F32)
             + jnp.dot(q_cat, s_cat, preferred_element_type=F32))
        _state_step(state, hd, k, bf, v, CHUNK - 1)
        ms = jnp.mean(o * o, axis=-1, keepdims=True)
        on = o * lax.rsqrt(ms + EPS) * g_ref[...]
        a_scr[rows, vs] = (on * z_ref[0, rows, vs].astype(F32)).astype(a_scr.dtype)

    def out_stages(r0):
        rows = slice(r0, r0 + sub)
        st = {"bm": [None] * ncol, "ya": [None] * ncol, "mg": [None] * ncol, "ss": []}

        def pool_input(cs):
            if r0 == 0:
                before = jnp.where(i > 0, up_ref[0, :, cs].astype(F32), 0.0)
            else:
                before = uc_ref[0, r0 - HALO:r0, cs].astype(F32)
            if r0 + sub == tm:
                after = jnp.where(i < nblk - 1, un_ref[0, :, cs].astype(F32), 0.0)
            else:
                after = uc_ref[0, r0 + sub:r0 + sub + HALO, cs].astype(F32)
            return jnp.concatenate([before, uc_ref[0, rows, cs].astype(F32), after], axis=0)

        def pool(gi):
            def run():
                w = POOL_WINDOWS[gi]
                cs = slice(gi * cw, (gi + 1) * cw)
                u_ext = pool_input(cs)
                lead = w // 2 - 1
                acc = pltpu.roll(u_ext, ext - lead, 0) if lead else u_ext
                s = 1
                while s < w:
                    acc = acc + pltpu.roll(acc, s, 0)
                    s *= 2
                t = i * tm + r0 + lax.broadcasted_iota(jnp.int32, (sub, LANES), 0)
                lo = jnp.maximum(t - w // 2, 0)
                hi = jnp.minimum(t + (w - 1 - w // 2), seq_len - 1)
                inv_cnt = 1.0 / (hi - lo + 1).astype(F32)
                inv_cnt = jnp.concatenate([inv_cnt] * (cw // LANES), axis=1)
                d = acc[HALO:HALO + sub] * inv_cnt - u_ext[HALO:HALO + sub]
                pooled = jnp.dot(d.astype(BF16), wpool_ref[gi], preferred_element_type=F32)
                st["bm"][gi] = (pooled * pscale_ref[:, cs]
                                * zb_ref[0, rows, cs].astype(F32)).astype(BF16)
            return run

        def branch_a(c):
            def run():
                cs = slice(c * cw, (c + 1) * cw)
                y_a = jnp.dot(a_scr[rows, :], wa_ref[:, cs], preferred_element_type=F32)
                st["ya"][c] = ga_ref[0, rows, cs].astype(F32) * y_a
            return run

        def branch_b(c):
            def run():
                cs = slice(c * cw, (c + 1) * cw)
                if c == 0:
                    st["bm"] = jnp.concatenate(st["bm"], axis=1)
                y_b = jnp.dot(st["bm"], wb_ref[:, cs], preferred_element_type=F32)
                st["mg"][c] = (st["ya"][c] + gb_ref[0, rows, cs].astype(F32) * y_b).astype(BF16)
            return run

        def output(c):
            def run():
                cs = slice(c * cw, (c + 1) * cw)
                if c == 0:
                    st["mg"] = jnp.concatenate(st["mg"], axis=1)
                y = jnp.dot(st["mg"], wo_ref[:, cs], preferred_element_type=F32)
                xn = x_ref[0, rows, cs] + gate_ref[0, :, cs] * y
                o_ref[0, rows, cs] = xn
                st["ss"].append(jnp.sum(xn * xn, axis=-1, keepdims=True))
            return run

        def normalise(c):
            def run():
                cs = slice(c * cw, (c + 1) * cw)
                ms = functools.reduce(jnp.add, st["ss"]) * (1.0 / D_MODEL)
                o_ref[0, rows, cs] = o_ref[0, rows, cs] * lax.rsqrt(ms + EPS) * fg_ref[:, cs]
            return run

        first = [f(c) for c in range(ncol) for f in (branch_a, pool)]
        return [first] + [[f(c) for c in range(ncol)] for f in (branch_b, output, normalise)]

    def body(exact):
        per_sub = sub // CHUNK
        gla = [[functools.partial(head_chunk, c, hd, exact)
                for c in range(n * per_sub, (n + 1) * per_sub) for hd in range(HEADS)]
               for n in range(tm // sub)]
        tiles = [out_stages(r0) for r0 in range(0, tm, sub)]
        nstage = len(tiles[0])
        for step in gla[0]:
            step()
        for step in range(len(tiles) + nstage - 1):
            lists = [tl[step - n] for n, tl in enumerate(tiles) if 0 <= step - n < nstage]
            if step + 1 < len(gla):
                lists.append(gla[step + 1])
            _interleave(*lists)

    for exact in (False, True):
        @pl.when(exact_path == exact)
        def _():
            body(exact)


def _tail(flags, q, k, v, bf, bb, sb, s0, z, g, ub, zb, ga, gb, x, mod3,
          w_pool, pool_scale, w_a, w_b, w_o, final_g):
    B, L, _ = x.shape
    tm = TAIL_TILE
    nblk = L // tm
    per_halo = tm // HALO
    n_halo = L // HALO
    tok = lambda width: pl.BlockSpec((1, tm, width), lambda bi, i, fl: (bi, i, 0))
    return pl.pallas_call(
        functools.partial(_tail_kernel, nblk=nblk, seq_len=L),
        grid_spec=pltpu.PrefetchScalarGridSpec(
            num_scalar_prefetch=1, grid=(B, nblk),
            in_specs=[
                tok(KW), tok(KW), tok(VW), tok(KW), tok(KW),
                pl.BlockSpec((1, tm // CHUNK, KW, DV), lambda bi, i, fl: (bi, i, 0, 0)),
                pl.BlockSpec((1, HEADS, DK, DV), lambda bi, i, fl: (bi, 0, 0, 0)),
                tok(VW), _resident(g),
                pl.BlockSpec((1, HALO, D_MODEL),
                             lambda bi, i, fl: (bi, jnp.maximum(i * per_halo - 1, 0), 0)),
                tok(D_MODEL),
                pl.BlockSpec((1, HALO, D_MODEL),
                             lambda bi, i, fl: (bi, jnp.minimum((i + 1) * per_halo, n_halo - 1), 0)),
                tok(D_MODEL), tok(D_MODEL), tok(D_MODEL), tok(D_MODEL),
                pl.BlockSpec((1, 1, D_MODEL), lambda bi, i, fl: (bi, 0, 2)),
                _resident(w_pool), _resident(pool_scale), _resident(w_a), _resident(w_b),
                _resident(w_o), _resident(final_g),
            ],
            out_specs=tok(D_MODEL),
            scratch_shapes=[pltpu.VMEM((HEADS, DK, DV), F32), pltpu.VMEM((tm, VW), BF16)]
                           + [pltpu.VMEM((CHUNK, DK), F32)] * 3),
        out_shape=jax.ShapeDtypeStruct((B, L, D_MODEL), F32),
        compiler_params=pltpu.CompilerParams(
            dimension_semantics=("parallel", "arbitrary"), vmem_limit_bytes=VMEM_LIMIT),
    )(flags, q, k, v, bf, bb, sb, s0, z, g, ub, ub, ub, zb, ga, gb, x, mod3,
      w_pool, pool_scale, w_a, w_b, w_o, final_g)


def _shift_cast_kernel(a_ref, b_ref, o_ref, *, shift):
    both = jnp.concatenate([a_ref[...], b_ref[...]], axis=1)
    o_ref[...] = both[:, shift:shift + o_ref.shape[1]].astype(o_ref.dtype)


def _unaligned_columns_bf16(w, first):
    rows, total = w.shape
    width = total - first
    shift = first % LANES
    blk = 4 * LANES
    assert width % blk == 0 and (first - shift) % blk == 0 and shift > 0
    return pl.pallas_call(
        functools.partial(_shift_cast_kernel, shift=shift),
        grid=(width // blk,),
        in_specs=[pl.BlockSpec((rows, blk), lambda j: (0, (first - shift) // blk + j)),
                  pl.BlockSpec((rows, LANES), lambda j: (0, (first - shift + blk * (j + 1)) // LANES))],
        out_specs=pl.BlockSpec((rows, blk), lambda j: (0, j)),
        out_shape=jax.ShapeDtypeStruct((rows, width), BF16),
        compiler_params=pltpu.CompilerParams(vmem_limit_bytes=VMEM_LIMIT),
    )(w, w)


def _rope_tables(seq_len):
    nf = ROT_HALF
    t = np.arange(seq_len)
    freqs = ROPE_BASE ** (-np.arange(nf, dtype=np.float64) / nf)
    rowp, colp = (t // GRID_W)[:, None], (t % GRID_W)[:, None]
    ang = np.concatenate([rowp * freqs] * 2 + [colp * freqs] * 2, axis=1)
    cos, sin = np.cos(ang), np.sin(ang)
    first = (np.arange(LANES) % (2 * nf)) < nf
    sin_up = np.where(first, -sin, 0.0)
    sin_dn = np.where(first, 0.0, sin)
    return jnp.asarray(np.stack([cos, sin_up, sin_dn]).astype(np.float32))


def _chunk_sum_matrices():
    ri = np.arange(SUB_TILE)[:, None]
    ci = np.arange(SUB_TILE)[None, :]
    same = (ri // CHUNK) == (ci // CHUNK)
    mats = [np.tile(same & cond, (1, 2)) for cond in (ci <= ri, ci >= ri)]
    return jnp.asarray(np.stack(mats).astype(np.float32), dtype=BF16)


def _flags(dmax, tiles_per_block):
    worst = jnp.max(dmax[:, 0, 0].reshape(-1, tiles_per_block), axis=1)
    return (worst > SAFE_DECAY).astype(jnp.int32)


def kernel(x, c, ctx, c_ctx, w_mod, b_mod, norm_g, w_in, w_gate_up_f, b_gate_f, w_gate_up_b,
           b_gate_b, gla_norm_g, w_pool, pool_scale, w_branch_a, w_branch_b, w_out, final_norm_g):
    B, L, _ = x.shape
    assert w_mod.shape[0] == 1, "single-layer configuration"
    wi = w_in[0]
    w1 = wi[:, :W_IN_FIRST].astype(BF16)
    w2 = _unaligned_columns_bf16(wi, W_IN_FIRST + 2 * RANK)
    w_lr = jnp.pad(wi[:, W_IN_FIRST:W_IN_FIRST + 2 * RANK],
                   ((0, 0), (0, LANES - 2 * RANK))).astype(BF16)
    w_up = jnp.zeros((LANES, 2 * KW), F32)
    w_up = w_up.at[:RANK, :KW].set(w_gate_up_f[0]).at[RANK:2 * RANK, KW:].set(w_gate_up_b[0])
    w_up = w_up.astype(BF16)
    b_up = jnp.concatenate([b_gate_f[0], b_gate_b[0]])[None, :]

    c_rows = jnp.zeros((MOD_ROWS, D_MODEL), F32).at[:B].set(c).at[B].set(c_ctx)
    mod = _modulation(c_rows, w_mod[0], b_mod[0][None, :])
    mod3 = mod[:, None, :]

    ng = norm_g[0][None, :]
    weights = (w1, w2, w_lr, w_up, b_up, _chunk_sum_matrices())

    *_, s_f, s_b = _project(ctx, mod3, lambda b: B, ng, None, weights, None, context=True)

    q, k, v, za, ub, zb, ga, gb, bf, bb, dm, sb_states = _project(
        x, mod3, lambda b: b, ng, _rope_tables(L), weights, s_b, context=False)
    return _tail(_flags(dm, TAIL_TILE // PROJ_TILE), q, k, v, bf, bb, sb_states, s_f, za,
                 gla_norm_g[0][None, :], ub, zb, ga, gb, x, mod3, w_pool[0].astype(BF16),
                 pool_scale[0][None, :], w_branch_a[0].astype(BF16), w_branch_b[0].astype(BF16),
                 w_out[0].astype(BF16), final_norm_g[None, :])
```

```python
import functools

import numpy as np
import jax
import jax.numpy as jnp
from jax import lax
from jax.experimental import pallas as pl
from jax.experimental.pallas import tpu as pltpu

F32 = jnp.float32
BF16 = jnp.bfloat16

D_MODEL = 1024
HEADS = 4
DK = 128
DV = 256
KW = HEADS * DK
VW = HEADS * DV
RANK = 16
GATE_NORM = 16.0
CHUNK = 256
GRID_W = 64
ROPE_BASE = 10000.0
POOL_WINDOWS = (2, 4, 8, 16)
POOL_GW = D_MODEL // len(POOL_WINDOWS)
EPS = 1e-6

LANES = 128
SUBLANES = 8
V7X_VMEM_BYTES = 64 * 1024 * 1024
ROT_HALF = DK // 4
MOD_ROWS = 16
HALO = 16
PROJ_TILE = 512
TAIL_TILE = 512
SUB_TILE = 256
SAFE_DECAY = 60.0
VMEM_LIMIT = V7X_VMEM_BYTES * 7 // 8

_C_Q, _C_K, _C_V, _C_ZA = 0, KW, 2 * KW, 2 * KW + VW
_C_UB, _C_ZB, _C_GA, _C_GB = 0, D_MODEL, 2 * D_MODEL, 3 * D_MODEL
W_IN_FIRST = 2 * KW + 2 * VW


def _sigmoid(x):
    return 0.5 * jnp.tanh(0.5 * x) + 0.5


def _silu(x):
    h = 0.5 * x
    return h + h * jnp.tanh(h)


def _mod_kernel(c_ref, w_ref, b_ref, o_ref):
    c = c_ref[...]
    s = c * jax.nn.sigmoid(c)
    o_ref[...] = jnp.dot(s, w_ref[...], preferred_element_type=F32,
                         precision=lax.Precision.HIGHEST) + b_ref[...]


def _modulation(c_rows, w_mod, b_mod):
    return pl.pallas_call(
        _mod_kernel,
        out_shape=jax.ShapeDtypeStruct((c_rows.shape[0], w_mod.shape[1]), F32),
        compiler_params=pltpu.CompilerParams(vmem_limit_bytes=VMEM_LIMIT),
    )(c_rows, w_mod, b_mod)


def _as_column(row):
    return jnp.broadcast_to(row, (SUBLANES, DK)).T[:, 0:1]


def _state_step(state, hd, k, b, v, edge):
    b_edge = b[edge:edge + 1, :]
    kd_t = (k * jnp.exp(b_edge - b)).T.astype(BF16)
    state[hd] = (jnp.exp(_as_column(b_edge)) * state[hd]
                 + jnp.dot(kd_t, v, preferred_element_type=F32))


def _proj_kernel(x_ref, shift_ref, scale_ref, g_ref, *rest, context):
    if context:
        (w1_ref, wlr_ref, wup_ref, bup_ref, tri_ref, k_ref, v_ref, bf_ref, bb_ref, dmax_ref,
         sf_ref, sb_ref, state_f, state_b) = rest
    else:
        (tab_ref, w1_ref, w2_ref, wlr_ref, wup_ref, bup_ref, tri_ref, s0_ref,
         q_ref, k_ref, v_ref, za_ref, ub_ref, zb_ref, ga_ref, gb_ref,
         bf_ref, bb_ref, dmax_ref, sb_ref, state_b) = rest
    tm = x_ref.shape[1]
    sub = min(tm, SUB_TILE)
    worst = []

    if context:
        state_f[...] = jnp.zeros_like(state_f)
        state_b[...] = jnp.zeros_like(state_b)
    else:
        @pl.when(pl.program_id(1) == 0)
        def _():
            state_b[...] = s0_ref[0]

    def stages(r0):
        rows = slice(r0, r0 + sub)
        st = {}

        def prologue():
            x = x_ref[0, rows, :]
            ms = jnp.mean(x * x, axis=-1, keepdims=True)
            h = (x * lax.rsqrt(ms + EPS)) * g_ref[...]
            h = h * (1.0 + scale_ref[0]) + shift_ref[0]
            st["hb"] = h.astype(BF16)
            lr = jnp.dot(st["hb"], wlr_ref[...], preferred_element_type=F32).astype(BF16)
            gk = jnp.dot(lr, wup_ref[...], preferred_element_type=F32) + bup_ref[...]
            la = (jnp.minimum(gk, 0.0) - jnp.log(1.0 + jnp.exp(-jnp.abs(gk)))) * (1.0 / GATE_NORM)
            hi = la.astype(BF16)
            lo = (la - hi.astype(F32)).astype(BF16)
            pieces = jnp.concatenate([hi, lo], axis=0)
            cf = jnp.dot(tri_ref[0], pieces[:, :KW], preferred_element_type=F32)
            cb = jnp.dot(tri_ref[1], pieces[:, KW:], preferred_element_type=F32)
            bf_ref[0, rows, :] = cf
            bb_ref[0, rows, :] = cb
            for c0 in range(0, sub, CHUNK):
                worst.append(jnp.maximum(-cf[c0 + CHUNK - 1:c0 + CHUNK, :], -cb[c0:c0 + 1, :]))

        def plain(w_ref, c0, dst_ref, act=None):
            def run():
                p = jnp.dot(st["hb"], w_ref[:, c0:c0 + dst_ref.shape[2]],
                            preferred_element_type=F32)
                dst_ref[0, rows, :] = (p if act is None else act(p)).astype(dst_ref.dtype)
            return run

        def rotary(c0, dst_ref, scale):
            def run():
                p = jnp.dot(st["hb"], w1_ref[:, c0:c0 + KW], preferred_element_type=F32)
                cos, sin_up, sin_dn = (tab_ref[n, rows, :] for n in range(3))
                for hd in range(HEADS):
                    ph = p[:, hd * DK:(hd + 1) * DK]
                    out = (ph * cos + pltpu.roll(ph, LANES - ROT_HALF, 1) * sin_up
                           + pltpu.roll(ph, ROT_HALF, 1) * sin_dn)
                    if scale is not None:
                        out = out * scale
                    dst_ref[0, rows, hd * DK:(hd + 1) * DK] = out.astype(dst_ref.dtype)
            return run

        if context:
            return prologue, [plain(w1_ref, _C_K, k_ref), plain(w1_ref, _C_V, v_ref)]
        return prologue, [
            rotary(_C_K, k_ref, None), plain(w1_ref, _C_V, v_ref), rotary(_C_Q, q_ref, DK ** -0.5),
            plain(w1_ref, _C_ZA, za_ref, _silu), plain(w2_ref, _C_UB, ub_ref),
            plain(w2_ref, _C_ZB, zb_ref, _silu), plain(w2_ref, _C_GA, ga_ref, _sigmoid),
            plain(w2_ref, _C_GB, gb_ref, _sigmoid)]

    def scan_steps(r0, state, b_ref, reverse, emit):
        steps = []
        starts = range(r0, r0 + sub, CHUNK)
        for c0 in (reversed(starts) if reverse else starts):
            for hd in range(HEADS):
                def run(c0=c0, hd=hd):
                    rows = slice(c0, c0 + CHUNK)
                    ks = slice(hd * DK, (hd + 1) * DK)
                    if emit:
                        sb_ref[0, c0 // CHUNK, ks, :] = state[hd].astype(BF16)
                    _state_step(state, hd, k_ref[0, rows, ks].astype(F32), b_ref[0, rows, ks],
                                v_ref[0, rows, hd * DV:(hd + 1) * DV], 0 if reverse else CHUNK - 1)
                steps.append(run)
        return steps

    starts = list(reversed(range(0, tm, sub)))
    tiles = [stages(r0) for r0 in starts]
    tiles[0][0]()
    for n, (_, matmuls) in enumerate(tiles):
        for step in matmuls[:2]:
            step()
        fill = scan_steps(starts[n], state_b, bb_ref, True, not context)
        if context:
            fill += scan_steps(starts[n], state_f, bf_ref, False, False)
        if n + 1 < len(tiles):
            fill.append(tiles[n + 1][0])
        _interleave(matmuls[2:], fill)
    dmax_ref[0] = jnp.broadcast_to(
        jnp.max(functools.reduce(jnp.maximum, worst), axis=1, keepdims=True), dmax_ref.shape[1:])
    if context:
        sf_ref[0] = state_f[...]
        sb_ref[0] = state_b[...]


def _interleave(*lists):
    order = sorted(((i + 0.5) / len(lst), which, i)
                   for which, lst in enumerate(lists) for i in range(len(lst)))
    for _, which, idx in order:
        lists[which][idx]()


def _resident(arr):
    return pl.BlockSpec(arr.shape, lambda *_: (0,) * arr.ndim, pipeline_mode=pl.Buffered(1))


def _project(x, mod3, mod_row, norm_g, tables, weights, s0, *, context):
    B, L, _ = x.shape
    tm = min(L, PROJ_TILE)
    nt = L // tm
    tile = (lambda i: i) if context else (lambda i: nt - 1 - i)
    tok = lambda width: pl.BlockSpec((1, tm, width), lambda b, i: (b, tile(i), 0))
    bf = lambda width: jax.ShapeDtypeStruct((B, L, width), BF16)
    st_spec = pl.BlockSpec((1, HEADS, DK, DV), lambda b, i: (b, 0, 0, 0))
    st_shape = jax.ShapeDtypeStruct((B, HEADS, DK, DV), F32)
    state = pltpu.VMEM((HEADS, DK, DV), F32)
    if context:
        assert nt == 1, "context stream is one tile per sample"
        widths = [KW, VW]
        extra = [weights[0]] + list(weights[2:])
        extra_specs = [_resident(w) for w in extra]
        state_specs, state_shapes, scratch = [st_spec, st_spec], [st_shape, st_shape], [state, state]
    else:
        widths = [KW, KW, VW, VW, D_MODEL, D_MODEL, D_MODEL, D_MODEL]
        extra = [tables] + list(weights) + [s0]
        extra_specs = ([pl.BlockSpec((3, tm, LANES), lambda b, i: (0, tile(i), 0))]
                       + [_resident(w) for w in weights] + [st_spec])
        state_specs = [pl.BlockSpec((1, tm // CHUNK, KW, DV), lambda b, i: (b, tile(i), 0, 0))]
        state_shapes = [jax.ShapeDtypeStruct((B, L // CHUNK, KW, DV), BF16)]
        scratch = [state]
    return pl.pallas_call(
        functools.partial(_proj_kernel, context=context),
        grid=(B, nt),
        in_specs=[
            tok(D_MODEL),
            pl.BlockSpec((1, 1, D_MODEL), lambda b, i: (mod_row(b), 0, 0)),
            pl.BlockSpec((1, 1, D_MODEL), lambda b, i: (mod_row(b), 0, 1)),
            _resident(norm_g),
        ] + extra_specs,
        out_specs=[tok(w) for w in widths] + [
            tok(KW), tok(KW),
            pl.BlockSpec((1, SUBLANES, LANES), lambda b, i: (b * nt + tile(i), 0, 0))] + state_specs,
        out_shape=[bf(w) for w in widths] + [
            jax.ShapeDtypeStruct((B, L, KW), F32), jax.ShapeDtypeStruct((B, L, KW), F32),
            jax.ShapeDtypeStruct((B * nt, SUBLANES, LANES), F32)] + state_shapes,
        scratch_shapes=scratch,
        compiler_params=pltpu.CompilerParams(
            dimension_semantics=("parallel", "arbitrary"), vmem_limit_bytes=VMEM_LIMIT),
    )(x, mod3, mod3, norm_g, *extra)


def _tail_kernel(flags_ref, q_ref, k_ref, v_ref, bf_ref, bb_ref, sb_ref, s0_ref, z_ref, g_ref,
                 up_ref, uc_ref, un_ref, zb_ref, ga_ref, gb_ref, x_ref, gate_ref,
                 wpool_ref, pscale_ref, wa_ref, wb_ref, wo_ref, fg_ref,
                 o_ref, state, a_scr, k_rows, bf_rows, bb_rows, *, nblk, seq_len):
    bi = pl.program_id(0)
    i = pl.program_id(1)
    tm = q_ref.shape[1]
    sub = min(tm, SUB_TILE)
    ext = sub + 2 * HALO
    cw = POOL_GW
    ncol = D_MODEL // cw

    @pl.when(i == 0)
    def _():
        state[...] = s0_ref[0]

    exact_path = flags_ref[bi * nblk + i] != 0
    row = lax.broadcasted_iota(jnp.int32, (CHUNK, CHUNK), 0)
    col = lax.broadcasted_iota(jnp.int32, (CHUNK, CHUNK), 1)
    nt_dims = (((1,), (1,)), ((), ()))

    def head_chunk(c, hd, exact):
        rows = slice(c * CHUNK, (c + 1) * CHUNK)
        ks = slice(hd * DK, (hd + 1) * DK)
        vs = slice(hd * DV, (hd + 1) * DV)
        q = q_ref[0, rows, ks].astype(F32)
        k = k_ref[0, rows, ks].astype(F32)
        v = v_ref[0, rows, vs]
        bf = bf_ref[0, rows, ks]
        bb = bb_ref[0, rows, ks]
        qf = (q * jnp.exp(bf)).astype(BF16)
        qb = (q * jnp.exp(bb)).astype(BF16)
        if not exact:
            kf = (k * jnp.exp(-bf)).astype(BF16)
            kb = (k * jnp.exp(-bb)).astype(BF16)
            sc = (jnp.where(col <= row, lax.dot_general(qf, kf, nt_dims,
                                                        preferred_element_type=F32), 0.0)
                  + jnp.where(col >= row, lax.dot_general(qb, kb, nt_dims,
                                                          preferred_element_type=F32), 0.0))
        else:
            k_rows[...] = k
            bf_rows[...] = bf
            bb_rows[...] = bb
            tok = lax.broadcasted_iota(jnp.int32, (CHUNK, DK), 0)

            def col_body(j, acc):
                dec = (jnp.where(tok >= j, jnp.exp(jnp.minimum(bf - bf_rows[pl.ds(j, 1), :], 0.0)), 0.0)
                       + jnp.where(tok <= j, jnp.exp(jnp.minimum(bb - bb_rows[pl.ds(j, 1), :], 0.0)), 0.0))
                w = q * k_rows[pl.ds(j, 1), :] * dec
                return jnp.where(col == j, jnp.sum(w, axis=1, keepdims=True), acc)

            sc = lax.fori_loop(0, CHUNK, col_body, jnp.zeros((CHUNK, CHUNK), F32))
        s_cat = jnp.concatenate([state[hd].astype(BF16), sb_ref[0, c, ks, :]], axis=0)
        q_cat = jnp.concatenate([qf, qb], axis=1)
        o = (jnp.dot(sc.astype(BF16), v, preferred_element_type=F32)
             + jnp.dot(q_cat, s_cat, preferred_element_type=F32))
        _state_step(state, hd, k, bf, v, CHUNK - 1)
        ms = jnp.mean(o * o, axis=-1, keepdims=True)
        on = o * lax.rsqrt(ms + EPS) * g_ref[...]
        a_scr[rows, vs] = (on * z_ref[0, rows, vs].astype(F32)).astype(a_scr.dtype)

    def out_stages(r0):
        rows = slice(r0, r0 + sub)
        st = {"bm": [None] * ncol, "ya": [None] * ncol, "mg": [None] * ncol, "ss": []}

        def pool_input(cs):
            if r0 == 0:
                before = jnp.where(i > 0, up_ref[0, :, cs].astype(F32), 0.0)
            else:
                before = uc_ref[0, r0 - HALO:r0, cs].astype(F32)
            if r0 + sub == tm:
                after = jnp.where(i < nblk - 1, un_ref[0, :, cs].astype(F32), 0.0)
            else:
                after = uc_ref[0, r0 + sub:r0 + sub + HALO, cs].astype(F32)
            return jnp.concatenate([before, uc_ref[0, rows, cs].astype(F32), after], axis=0)

        def pool(gi):
            def run():
                w = POOL_WINDOWS[gi]
                cs = slice(gi * cw, (gi + 1) * cw)
                u_ext = pool_input(cs)
                lead = w // 2 - 1
                acc = pltpu.roll(u_ext, ext - lead, 0) if lead else u_ext
                s = 1
                while s < w:
                    acc = acc + pltpu.roll(acc, s, 0)
                    s *= 2
                t = i * tm + r0 + lax.broadcasted_iota(jnp.int32, (sub, LANES), 0)
                lo = jnp.maximum(t - w // 2, 0)
                hi = jnp.minimum(t + (w - 1 - w // 2), seq_len - 1)
                inv_cnt = 1.0 / (hi - lo + 1).astype(F32)
                inv_cnt = jnp.concatenate([inv_cnt] * (cw // LANES), axis=1)
                d = acc[HALO:HALO + sub] * inv_cnt - u_ext[HALO:HALO + sub]
                pooled = jnp.dot(d.astype(BF16), wpool_ref[gi], preferred_element_type=F32)
                st["bm"][gi] = (pooled * pscale_ref[:, cs]
                                * zb_ref[0, rows, cs].astype(F32)).astype(BF16)
            return run

        def branch_a(c):
            def run():
                cs = slice(c * cw, (c + 1) * cw)
                y_a = jnp.dot(a_scr[rows, :], wa_ref[:, cs], preferred_element_type=F32)
                st["ya"][c] = ga_ref[0, rows, cs].astype(F32) * y_a
            return run

        def branch_b(c):
            def run():
                cs = slice(c * cw, (c + 1) * cw)
                if c == 0:
                    st["bm"] = jnp.concatenate(st["bm"], axis=1)
                y_b = jnp.dot(st["bm"], wb_ref[:, cs], preferred_element_type=F32)
                st["mg"][c] = (st["ya"][c] + gb_ref[0, rows, cs].astype(F32) * y_b).astype(BF16)
            return run

        def output(c):
            def run():
                cs = slice(c * cw, (c + 1) * cw)
                if c == 0:
                    st["mg"] = jnp.concatenate(st["mg"], axis=1)
                y = jnp.dot(st["mg"], wo_ref[:, cs], preferred_element_type=F32)
                xn = x_ref[0, rows, cs] + gate_ref[0, :, cs] * y
                o_ref[0, rows, cs] = xn
                st["ss"].append(jnp.sum(xn * xn, axis=-1, keepdims=True))
            return run

        def normalise(c):
            def run():
                cs = slice(c * cw, (c + 1) * cw)
                ms = functools.reduce(jnp.add, st["ss"]) * (1.0 / D_MODEL)
                o_ref[0, rows, cs] = o_ref[0, rows, cs] * lax.rsqrt(ms + EPS) * fg_ref[:, cs]
            return run

        first = [f(c) for c in range(ncol) for f in (branch_a, pool)]
        return [first] + [[f(c) for c in range(ncol)] for f in (branch_b, output, normalise)]

    def body(exact):
        per_sub = sub // CHUNK
        gla = [[functools.partial(head_chunk, c, hd, exact)
                for c in range(n * per_sub, (n + 1) * per_sub) for hd in range(HEADS)]
               for n in range(tm // sub)]
        tiles = [out_stages(r0) for r0 in range(0, tm, sub)]
        nstage = len(tiles[0])
        for step in gla[0]:
            step()
        for step in range(len(tiles) + nstage - 1):
            lists = [tl[step - n] for n, tl in enumerate(tiles) if 0 <= step - n < nstage]
            if step + 1 < len(gla):
                lists.append(gla[step + 1])
            _interleave(*lists)

    for exact in (False, True):
        @pl.when(exact_path == exact)
        def _():
            body(exact)


def _tail(flags, q, k, v, bf, bb, sb, s0, z, g, ub, zb, ga, gb, x, mod3,
          w_pool, pool_scale, w_a, w_b, w_o, final_g):
    B, L, _ = x.shape
    tm = TAIL_TILE
    nblk = L // tm
    per_halo = tm // HALO
    n_halo = L // HALO
    tok = lambda width: pl.BlockSpec((1, tm, width), lambda bi, i, fl: (bi, i, 0))
    return pl.pallas_call(
        functools.partial(_tail_kernel, nblk=nblk, seq_len=L),
        grid_spec=pltpu.PrefetchScalarGridSpec(
            num_scalar_prefetch=1, grid=(B, nblk),
            in_specs=[
                tok(KW), tok(KW), tok(VW), tok(KW), tok(KW),
                pl.BlockSpec((1, tm // CHUNK, KW, DV), lambda bi, i, fl: (bi, i, 0, 0)),
                pl.BlockSpec((1, HEADS, DK, DV), lambda bi, i, fl: (bi, 0, 0, 0)),
                tok(VW), _resident(g),
                pl.BlockSpec((1, HALO, D_MODEL),
                             lambda bi, i, fl: (bi, jnp.maximum(i * per_halo - 1, 0), 0)),
                tok(D_MODEL),
                pl.BlockSpec((1, HALO, D_MODEL),
                             lambda bi, i, fl: (bi, jnp.minimum((i + 1) * per_halo, n_halo - 1), 0)),
                tok(D_MODEL), tok(D_MODEL), tok(D_MODEL), tok(D_MODEL),
                pl.BlockSpec((1, 1, D_MODEL), lambda bi, i, fl: (bi, 0, 2)),
                _resident(w_pool), _resident(pool_scale), _resident(w_a), _resident(w_b),
                _resident(w_o), _resident(final_g),
            ],
            out_specs=tok(D_MODEL),
            scratch_shapes=[pltpu.VMEM((HEADS, DK, DV), F32), pltpu.VMEM((tm, VW), BF16)]
                           + [pltpu.VMEM((CHUNK, DK), F32)] * 3),
        out_shape=jax.ShapeDtypeStruct((B, L, D_MODEL), F32),
        compiler_params=pltpu.CompilerParams(
            dimension_semantics=("parallel", "arbitrary"), vmem_limit_bytes=VMEM_LIMIT),
    )(flags, q, k, v, bf, bb, sb, s0, z, g, ub, ub, ub, zb, ga, gb, x, mod3,
      w_pool, pool_scale, w_a, w_b, w_o, final_g)


def _rope_tables(seq_len):
    nf = ROT_HALF
    t = np.arange(seq_len)
    freqs = ROPE_BASE ** (-np.arange(nf, dtype=np.float64) / nf)
    rowp, colp = (t // GRID_W)[:, None], (t % GRID_W)[:, None]
    ang = np.concatenate([rowp * freqs] * 2 + [colp * freqs] * 2, axis=1)
    cos, sin = np.cos(ang), np.sin(ang)
    first = (np.arange(LANES) % (2 * nf)) < nf
    sin_up = np.where(first, -sin, 0.0)
    sin_dn = np.where(first, 0.0, sin)
    return jnp.asarray(np.stack([cos, sin_up, sin_dn]).astype(np.float32))


def _chunk_sum_matrices():
    ri = np.arange(SUB_TILE)[:, None]
    ci = np.arange(SUB_TILE)[None, :]
    same = (ri // CHUNK) == (ci // CHUNK)
    mats = [np.tile(same & cond, (1, 2)) for cond in (ci <= ri, ci >= ri)]
    return jnp.asarray(np.stack(mats).astype(np.float32), dtype=BF16)


def _flags(dmax, tiles_per_block):
    worst = jnp.max(dmax[:, 0, 0].reshape(-1, tiles_per_block), axis=1)
    return (worst > SAFE_DECAY).astype(jnp.int32)


def kernel(x, c, ctx, c_ctx, w_mod, b_mod, norm_g, w_in, w_gate_up_f, b_gate_f, w_gate_up_b,
           b_gate_b, gla_norm_g, w_pool, pool_scale, w_branch_a, w_branch_b, w_out, final_norm_g):
    B, L, _ = x.shape
    assert w_mod.shape[0] == 1, "single-layer configuration"
    wi = w_in[0]
    w1 = wi[:, :W_IN_FIRST].astype(BF16)
    w2 = wi[:, W_IN_FIRST + 2 * RANK:].astype(BF16)
    w_lr = jnp.pad(wi[:, W_IN_FIRST:W_IN_FIRST + 2 * RANK],
                   ((0, 0), (0, LANES - 2 * RANK))).astype(BF16)
    w_up = jnp.zeros((LANES, 2 * KW), F32)
    w_up = w_up.at[:RANK, :KW].set(w_gate_up_f[0]).at[RANK:2 * RANK, KW:].set(w_gate_up_b[0])
    w_up = w_up.astype(BF16)
    b_up = jnp.concatenate([b_gate_f[0], b_gate_b[0]])[None, :]

    c_rows = jnp.zeros((MOD_ROWS, D_MODEL), F32).at[:B].set(c).at[B].set(c_ctx)
    mod = _modulation(c_rows, w_mod[0], b_mod[0][None, :])
    mod3 = mod[:, None, :]

    ng = norm_g[0][None, :]
    weights = (w1, w2, w_lr, w_up, b_up, _chunk_sum_matrices())

    *_, s_f, s_b = _project(ctx, mod3, lambda b: B, ng, None, weights, None, context=True)

    q, k, v, za, ub, zb, ga, gb, bf, bb, dm, sb_states = _project(
        x, mod3, lambda b: b, ng, _rope_tables(L), weights, s_b, context=False)
    return _tail(_flags(dm, TAIL_TILE // PROJ_TILE), q, k, v, bf, bb, sb_states, s_f, za,
                 gla_norm_g[0][None, :], ub, zb, ga, gb, x, mod3, w_pool[0].astype(BF16),
                 pool_scale[0][None, :], w_branch_a[0].astype(BF16), w_branch_b[0].astype(BF16),
                 w_out[0].astype(BF16), final_norm_g[None, :])
```

```python
import functools

import numpy as np
import jax
import jax.numpy as jnp
from jax import lax
from jax.experimental import pallas as pl
from jax.experimental.pallas import tpu as pltpu

F32 = jnp.float32
BF16 = jnp.bfloat16

D_MODEL = 1024
HEADS = 4
DK = 128
DV = 256
KW = HEADS * DK
VW = HEADS * DV
RANK = 16
GATE_NORM = 16.0
CHUNK = 256
GRID_W = 64
ROPE_BASE = 10000.0
POOL_WINDOWS = (2, 4, 8, 16)
POOL_GW = D_MODEL // len(POOL_WINDOWS)
EPS = 1e-6

LANES = 128
SUBLANES = 8
V7X_VMEM_BYTES = 64 * 1024 * 1024
ROT_HALF = DK // 4
MOD_ROWS = 16
HALO = 16
PROJ_TILE = 512
TAIL_TILE = 512
SUB_TILE = 256
OUT_COLS = 512
SAFE_DECAY = 60.0
VMEM_LIMIT = V7X_VMEM_BYTES * 7 // 8

_C_Q, _C_K, _C_V, _C_ZA = 0, KW, 2 * KW, 2 * KW + VW
_C_UB, _C_ZB, _C_GA, _C_GB = 0, D_MODEL, 2 * D_MODEL, 3 * D_MODEL
W_IN_FIRST = 2 * KW + 2 * VW


def _sigmoid(x):
    return 0.5 * jnp.tanh(0.5 * x) + 0.5


def _silu(x):
    h = 0.5 * x
    return h + h * jnp.tanh(h)


def _mod_kernel(c_ref, w_ref, b_ref, o_ref):
    c = c_ref[...]
    s = c * jax.nn.sigmoid(c)
    o_ref[...] = jnp.dot(s, w_ref[...], preferred_element_type=F32,
                         precision=lax.Precision.HIGHEST) + b_ref[...]


def _modulation(c_rows, w_mod, b_mod):
    return pl.pallas_call(
        _mod_kernel,
        out_shape=jax.ShapeDtypeStruct((c_rows.shape[0], w_mod.shape[1]), F32),
        compiler_params=pltpu.CompilerParams(vmem_limit_bytes=VMEM_LIMIT),
    )(c_rows, w_mod, b_mod)


def _as_column(row):
    return jnp.broadcast_to(row, (SUBLANES, DK)).T[:, 0:1]


def _state_step(state, hd, k, b, v, edge):
    b_edge = b[edge:edge + 1, :]
    kd_t = (k * jnp.exp(b_edge - b)).T.astype(BF16)
    state[hd] = (jnp.exp(_as_column(b_edge)) * state[hd]
                 + jnp.dot(kd_t, v, preferred_element_type=F32))


def _proj_kernel(x_ref, shift_ref, scale_ref, g_ref, *rest, context):
    if context:
        (w1_ref, wlr_ref, wup_ref, bup_ref, tri_ref, k_ref, v_ref, bf_ref, bb_ref, dmax_ref,
         sf_ref, sb_ref, state_f, state_b) = rest
    else:
        (tab_ref, w1_ref, w2_ref, wlr_ref, wup_ref, bup_ref, tri_ref, s0_ref,
         q_ref, k_ref, v_ref, za_ref, ub_ref, zb_ref, ga_ref, gb_ref,
         bf_ref, bb_ref, dmax_ref, sb_ref, state_b) = rest
    tm = x_ref.shape[1]
    sub = min(tm, SUB_TILE)
    worst = []

    if context:
        state_f[...] = jnp.zeros_like(state_f)
        state_b[...] = jnp.zeros_like(state_b)
    else:
        @pl.when(pl.program_id(1) == 0)
        def _():
            state_b[...] = s0_ref[0]

    def stages(r0):
        rows = slice(r0, r0 + sub)
        st = {}

        def prologue():
            x = x_ref[0, rows, :]
            ms = jnp.mean(x * x, axis=-1, keepdims=True)
            h = (x * lax.rsqrt(ms + EPS)) * g_ref[...]
            h = h * (1.0 + scale_ref[0]) + shift_ref[0]
            st["hb"] = h.astype(BF16)
            lr = jnp.dot(st["hb"], wlr_ref[...], preferred_element_type=F32).astype(BF16)
            gk = jnp.dot(lr, wup_ref[...], preferred_element_type=F32) + bup_ref[...]
            la = (jnp.minimum(gk, 0.0) - jnp.log(1.0 + jnp.exp(-jnp.abs(gk)))) * (1.0 / GATE_NORM)
            hi = la.astype(BF16)
            lo = (la - hi.astype(F32)).astype(BF16)
            pieces = jnp.concatenate([hi, lo], axis=0)
            cf = jnp.dot(tri_ref[0], pieces[:, :KW], preferred_element_type=F32)
            cb = jnp.dot(tri_ref[1], pieces[:, KW:], preferred_element_type=F32)
            bf_ref[0, rows, :] = cf
            bb_ref[0, rows, :] = cb
            for c0 in range(0, sub, CHUNK):
                worst.append(jnp.maximum(-cf[c0 + CHUNK - 1:c0 + CHUNK, :], -cb[c0:c0 + 1, :]))

        def plain(w_ref, c0, dst_ref, act=None):
            def run():
                p = jnp.dot(st["hb"], w_ref[:, c0:c0 + dst_ref.shape[2]],
                            preferred_element_type=F32)
                dst_ref[0, rows, :] = (p if act is None else act(p)).astype(dst_ref.dtype)
            return run

        def rotary(c0, dst_ref, scale):
            def run():
                p = jnp.dot(st["hb"], w1_ref[:, c0:c0 + KW], preferred_element_type=F32)
                cos, sin_up, sin_dn = (tab_ref[n, rows, :] for n in range(3))
                for hd in range(HEADS):
                    ph = p[:, hd * DK:(hd + 1) * DK]
                    out = (ph * cos + pltpu.roll(ph, LANES - ROT_HALF, 1) * sin_up
                           + pltpu.roll(ph, ROT_HALF, 1) * sin_dn)
                    if scale is not None:
                        out = out * scale
                    dst_ref[0, rows, hd * DK:(hd + 1) * DK] = out.astype(dst_ref.dtype)
            return run

        if context:
            return prologue, [plain(w1_ref, _C_K, k_ref), plain(w1_ref, _C_V, v_ref)]
        return prologue, [
            rotary(_C_K, k_ref, None), plain(w1_ref, _C_V, v_ref), rotary(_C_Q, q_ref, DK ** -0.5),
            plain(w1_ref, _C_ZA, za_ref, _silu), plain(w2_ref, _C_UB, ub_ref),
            plain(w2_ref, _C_ZB, zb_ref, _silu), plain(w2_ref, _C_GA, ga_ref, _sigmoid),
            plain(w2_ref, _C_GB, gb_ref, _sigmoid)]

    def scan_steps(r0, state, b_ref, reverse, emit):
        steps = []
        starts = range(r0, r0 + sub, CHUNK)
        for c0 in (reversed(starts) if reverse else starts):
            for hd in range(HEADS):
                def run(c0=c0, hd=hd):
                    rows = slice(c0, c0 + CHUNK)
                    ks = slice(hd * DK, (hd + 1) * DK)
                    if emit:
                        sb_ref[0, c0 // CHUNK, ks, :] = state[hd].astype(BF16)
                    _state_step(state, hd, k_ref[0, rows, ks].astype(F32), b_ref[0, rows, ks],
                                v_ref[0, rows, hd * DV:(hd + 1) * DV], 0 if reverse else CHUNK - 1)
                steps.append(run)
        return steps

    starts = list(reversed(range(0, tm, sub)))
    tiles = [stages(r0) for r0 in starts]
    tiles[0][0]()
    for n, (_, matmuls) in enumerate(tiles):
        for step in matmuls[:2]:
            step()
        fill = scan_steps(starts[n], state_b, bb_ref, True, not context)
        if context:
            fill += scan_steps(starts[n], state_f, bf_ref, False, False)
        if n + 1 < len(tiles):
            fill.append(tiles[n + 1][0])
        _interleave(matmuls[2:], fill)
    dmax_ref[0] = jnp.broadcast_to(
        jnp.max(functools.reduce(jnp.maximum, worst), axis=1, keepdims=True), dmax_ref.shape[1:])
    if context:
        sf_ref[0] = state_f[...]
        sb_ref[0] = state_b[...]


def _interleave(*lists):
    order = sorted(((i + 0.5) / len(lst), which, i)
                   for which, lst in enumerate(lists) for i in range(len(lst)))
    for _, which, idx in order:
        lists[which][idx]()


def _resident(arr):
    return pl.BlockSpec(arr.shape, lambda *_: (0,) * arr.ndim, pipeline_mode=pl.Buffered(1))


def _project(x, mod3, mod_row, norm_g, tables, weights, s0, *, context):
    B, L, _ = x.shape
    tm = min(L, PROJ_TILE)
    nt = L // tm
    tile = (lambda i: i) if context else (lambda i: nt - 1 - i)
    tok = lambda width: pl.BlockSpec((1, tm, width), lambda b, i: (b, tile(i), 0))
    bf = lambda width: jax.ShapeDtypeStruct((B, L, width), BF16)
    st_spec = pl.BlockSpec((1, HEADS, DK, DV), lambda b, i: (b, 0, 0, 0))
    st_shape = jax.ShapeDtypeStruct((B, HEADS, DK, DV), F32)
    state = pltpu.VMEM((HEADS, DK, DV), F32)
    if context:
        assert nt == 1, "context stream is one tile per sample"
        widths = [KW, VW]
        extra = [weights[0]] + list(weights[2:])
        extra_specs = [_resident(w) for w in extra]
        state_specs, state_shapes, scratch = [st_spec, st_spec], [st_shape, st_shape], [state, state]
    else:
        widths = [KW, KW, VW, VW, D_MODEL, D_MODEL, D_MODEL, D_MODEL]
        extra = [tables] + list(weights) + [s0]
        extra_specs = ([pl.BlockSpec((3, tm, LANES), lambda b, i: (0, tile(i), 0))]
                       + [_resident(w) for w in weights] + [st_spec])
        state_specs = [pl.BlockSpec((1, tm // CHUNK, KW, DV), lambda b, i: (b, tile(i), 0, 0))]
        state_shapes = [jax.ShapeDtypeStruct((B, L // CHUNK, KW, DV), BF16)]
        scratch = [state]
    return pl.pallas_call(
        functools.partial(_proj_kernel, context=context),
        grid=(B, nt),
        in_specs=[
            tok(D_MODEL),
            pl.BlockSpec((1, 1, D_MODEL), lambda b, i: (mod_row(b), 0, 0)),
            pl.BlockSpec((1, 1, D_MODEL), lambda b, i: (mod_row(b), 0, 1)),
            _resident(norm_g),
        ] + extra_specs,
        out_specs=[tok(w) for w in widths] + [
            tok(KW), tok(KW),
            pl.BlockSpec((1, SUBLANES, LANES), lambda b, i: (b * nt + tile(i), 0, 0))] + state_specs,
        out_shape=[bf(w) for w in widths] + [
            jax.ShapeDtypeStruct((B, L, KW), F32), jax.ShapeDtypeStruct((B, L, KW), F32),
            jax.ShapeDtypeStruct((B * nt, SUBLANES, LANES), F32)] + state_shapes,
        scratch_shapes=scratch,
        compiler_params=pltpu.CompilerParams(
            dimension_semantics=("parallel", "arbitrary"), vmem_limit_bytes=VMEM_LIMIT),
    )(x, mod3, mod3, norm_g, *extra)


def _tail_kernel(flags_ref, q_ref, k_ref, v_ref, bf_ref, bb_ref, sb_ref, s0_ref, z_ref, g_ref,
                 up_ref, uc_ref, un_ref, zb_ref, ga_ref, gb_ref, x_ref, gate_ref,
                 wpool_ref, pscale_ref, wa_ref, wb_ref, wo_ref, fg_ref,
                 o_ref, state, a_scr, k_rows, bf_rows, bb_rows, *, nblk, seq_len):
    bi = pl.program_id(0)
    i = pl.program_id(1)
    tm = q_ref.shape[1]
    sub = min(tm, SUB_TILE)
    ext = sub + 2 * HALO
    cw = OUT_COLS
    ncol = D_MODEL // cw
    ngrp = len(POOL_WINDOWS)

    @pl.when(i == 0)
    def _():
        state[...] = s0_ref[0]

    exact_path = flags_ref[bi * nblk + i] != 0
    row = lax.broadcasted_iota(jnp.int32, (CHUNK, CHUNK), 0)
    col = lax.broadcasted_iota(jnp.int32, (CHUNK, CHUNK), 1)
    nt_dims = (((1,), (1,)), ((), ()))

    def head_chunk(c, hd, exact):
        rows = slice(c * CHUNK, (c + 1) * CHUNK)
        ks = slice(hd * DK, (hd + 1) * DK)
        vs = slice(hd * DV, (hd + 1) * DV)
        q = q_ref[0, rows, ks].astype(F32)
        k = k_ref[0, rows, ks].astype(F32)
        v = v_ref[0, rows, vs]
        bf = bf_ref[0, rows, ks]
        bb = bb_ref[0, rows, ks]
        qf = (q * jnp.exp(bf)).astype(BF16)
        qb = (q * jnp.exp(bb)).astype(BF16)
        if not exact:
            kf = (k * jnp.exp(-bf)).astype(BF16)
            kb = (k * jnp.exp(-bb)).astype(BF16)
            sc = (jnp.where(col <= row, lax.dot_general(qf, kf, nt_dims,
                                                        preferred_element_type=F32), 0.0)
                  + jnp.where(col >= row, lax.dot_general(qb, kb, nt_dims,
                                                          preferred_element_type=F32), 0.0))
        else:
            k_rows[...] = k
            bf_rows[...] = bf
            bb_rows[...] = bb
            tok = lax.broadcasted_iota(jnp.int32, (CHUNK, DK), 0)

            def col_body(j, acc):
                dec = (jnp.where(tok >= j, jnp.exp(jnp.minimum(bf - bf_rows[pl.ds(j, 1), :], 0.0)), 0.0)
                       + jnp.where(tok <= j, jnp.exp(jnp.minimum(bb - bb_rows[pl.ds(j, 1), :], 0.0)), 0.0))
                w = q * k_rows[pl.ds(j, 1), :] * dec
                return jnp.where(col == j, jnp.sum(w, axis=1, keepdims=True), acc)

            sc = lax.fori_loop(0, CHUNK, col_body, jnp.zeros((CHUNK, CHUNK), F32))
        s_cat = jnp.concatenate([state[hd].astype(BF16), sb_ref[0, c, ks, :]], axis=0)
        q_cat = jnp.concatenate([qf, qb], axis=1)
        o = (jnp.dot(sc.astype(BF16), v, preferred_element_type=F32)
             + jnp.dot(q_cat, s_cat, preferred_element_type=F32))
        _state_step(state, hd, k, bf, v, CHUNK - 1)
        ms = jnp.mean(o * o, axis=-1, keepdims=True)
        on = o * lax.rsqrt(ms + EPS) * g_ref[...]
        a_scr[rows, vs] = (on * z_ref[0, rows, vs].astype(F32)).astype(a_scr.dtype)

    def out_stages(r0):
        rows = slice(r0, r0 + sub)
        st = {"bm": [None] * ngrp, "ya": [None] * ncol, "mg": [None] * ncol, "ss": []}

        def pool_input(cs):
            if r0 == 0:
                before = jnp.where(i > 0, up_ref[0, :, cs].astype(F32), 0.0)
            else:
                before = uc_ref[0, r0 - HALO:r0, cs].astype(F32)
            if r0 + sub == tm:
                after = jnp.where(i < nblk - 1, un_ref[0, :, cs].astype(F32), 0.0)
            else:
                after = uc_ref[0, r0 + sub:r0 + sub + HALO, cs].astype(F32)
            return jnp.concatenate([before, uc_ref[0, rows, cs].astype(F32), after], axis=0)

        def pool(gi):
            def run():
                w = POOL_WINDOWS[gi]
                cs = slice(gi * POOL_GW, (gi + 1) * POOL_GW)
                u_ext = pool_input(cs)
                lead = w // 2 - 1
                acc = pltpu.roll(u_ext, ext - lead, 0) if lead else u_ext
                s = 1
                while s < w:
                    acc = acc + pltpu.roll(acc, s, 0)
                    s *= 2
                t = i * tm + r0 + lax.broadcasted_iota(jnp.int32, (sub, LANES), 0)
                lo = jnp.maximum(t - w // 2, 0)
                hi = jnp.minimum(t + (w - 1 - w // 2), seq_len - 1)
                inv_cnt = 1.0 / (hi - lo + 1).astype(F32)
                inv_cnt = jnp.concatenate([inv_cnt] * (POOL_GW // LANES), axis=1)
                d = acc[HALO:HALO + sub] * inv_cnt - u_ext[HALO:HALO + sub]
                pooled = jnp.dot(d.astype(BF16), wpool_ref[gi], preferred_element_type=F32)
                st["bm"][gi] = (pooled * pscale_ref[:, cs]
                                * zb_ref[0, rows, cs].astype(F32)).astype(BF16)
            return run

        def branch_a(c):
            def run():
                cs = slice(c * cw, (c + 1) * cw)
                y_a = jnp.dot(a_scr[rows, :], wa_ref[:, cs], preferred_element_type=F32)
                st["ya"][c] = ga_ref[0, rows, cs].astype(F32) * y_a
            return run

        def branch_b(c):
            def run():
                cs = slice(c * cw, (c + 1) * cw)
                if c == 0:
                    st["bm"] = jnp.concatenate(st["bm"], axis=1)
                y_b = jnp.dot(st["bm"], wb_ref[:, cs], preferred_element_type=F32)
                st["mg"][c] = (st["ya"][c] + gb_ref[0, rows, cs].astype(F32) * y_b).astype(BF16)
            return run

        def output(c):
            def run():
                cs = slice(c * cw, (c + 1) * cw)
                if c == 0:
                    st["mg"] = jnp.concatenate(st["mg"], axis=1)
                y = jnp.dot(st["mg"], wo_ref[:, cs], preferred_element_type=F32)
                xn = x_ref[0, rows, cs] + gate_ref[0, :, cs] * y
                o_ref[0, rows, cs] = xn
                st["ss"].append(jnp.sum(xn * xn, axis=-1, keepdims=True))
            return run

        def normalise(c):
            def run():
                cs = slice(c * cw, (c + 1) * cw)
                ms = functools.reduce(jnp.add, st["ss"]) * (1.0 / D_MODEL)
                o_ref[0, rows, cs] = o_ref[0, rows, cs] * lax.rsqrt(ms + EPS) * fg_ref[:, cs]
            return run

        first = sorted([((c + 0.5) / ncol, 0, branch_a(c)) for c in range(ncol)]
                       + [((gi + 0.5) / ngrp, 1, pool(gi)) for gi in range(ngrp)],
                       key=lambda item: item[:2])
        return [[item[2] for item in first]] + [[f(c) for c in range(ncol)] for f in (branch_b, output, normalise)]

    def body(exact):
        per_sub = sub // CHUNK
        gla = [[functools.partial(head_chunk, c, hd, exact)
                for c in range(n * per_sub, (n + 1) * per_sub) for hd in range(HEADS)]
               for n in range(tm // sub)]
        tiles = [out_stages(r0) for r0 in range(0, tm, sub)]
        nstage = len(tiles[0])
        for step in gla[0]:
            step()
        for step in range(len(tiles) + nstage - 1):
            lists = [tl[step - n] for n, tl in enumerate(tiles) if 0 <= step - n < nstage]
            if step + 1 < len(gla):
                lists.append(gla[step + 1])
            _interleave(*lists)

    for exact in (False, True):
        @pl.when(exact_path == exact)
        def _():
            body(exact)


def _tail(flags, q, k, v, bf, bb, sb, s0, z, g, ub, zb, ga, gb, x, mod3,
          w_pool, pool_scale, w_a, w_b, w_o, final_g):
    B, L, _ = x.shape
    tm = TAIL_TILE
    nblk = L // tm
    per_halo = tm // HALO
    n_halo = L // HALO
    tok = lambda width: pl.BlockSpec((1, tm, width), lambda bi, i, fl: (bi, i, 0))
    return pl.pallas_call(
        functools.partial(_tail_kernel, nblk=nblk, seq_len=L),
        grid_spec=pltpu.PrefetchScalarGridSpec(
            num_scalar_prefetch=1, grid=(B, nblk),
            in_specs=[
                tok(KW), tok(KW), tok(VW), tok(KW), tok(KW),
                pl.BlockSpec((1, tm // CHUNK, KW, DV), lambda bi, i, fl: (bi, i, 0, 0)),
                pl.BlockSpec((1, HEADS, DK, DV), lambda bi, i, fl: (bi, 0, 0, 0)),
                tok(VW), _resident(g),
                pl.BlockSpec((1, HALO, D_MODEL),
                             lambda bi, i, fl: (bi, jnp.maximum(i * per_halo - 1, 0), 0)),
                tok(D_MODEL),
                pl.BlockSpec((1, HALO, D_MODEL),
                             lambda bi, i, fl: (bi, jnp.minimum((i + 1) * per_halo, n_halo - 1), 0)),
                tok(D_MODEL), tok(D_MODEL), tok(D_MODEL), tok(D_MODEL),
                pl.BlockSpec((1, 1, D_MODEL), lambda bi, i, fl: (bi, 0, 2)),
                _resident(w_pool), _resident(pool_scale), _resident(w_a), _resident(w_b),
                _resident(w_o), _resident(final_g),
            ],
            out_specs=tok(D_MODEL),
            scratch_shapes=[pltpu.VMEM((HEADS, DK, DV), F32), pltpu.VMEM((tm, VW), BF16)]
                           + [pltpu.VMEM((CHUNK, DK), F32)] * 3),
        out_shape=jax.ShapeDtypeStruct((B, L, D_MODEL), F32),
        compiler_params=pltpu.CompilerParams(
            dimension_semantics=("parallel", "arbitrary"), vmem_limit_bytes=VMEM_LIMIT),
    )(flags, q, k, v, bf, bb, sb, s0, z, g, ub, ub, ub, zb, ga, gb, x, mod3,
      w_pool, pool_scale, w_a, w_b, w_o, final_g)


def _rope_tables(seq_len):
    nf = ROT_HALF
    t = np.arange(seq_len)
    freqs = ROPE_BASE ** (-np.arange(nf, dtype=np.float64) / nf)
    rowp, colp = (t // GRID_W)[:, None], (t % GRID_W)[:, None]
    ang = np.concatenate([rowp * freqs] * 2 + [colp * freqs] * 2, axis=1)
    cos, sin = np.cos(ang), np.sin(ang)
    first = (np.arange(LANES) % (2 * nf)) < nf
    sin_up = np.where(first, -sin, 0.0)
    sin_dn = np.where(first, 0.0, sin)
    return jnp.asarray(np.stack([cos, sin_up, sin_dn]).astype(np.float32))


def _chunk_sum_matrices():
    ri = np.arange(SUB_TILE)[:, None]
    ci = np.arange(SUB_TILE)[None, :]
    same = (ri // CHUNK) == (ci // CHUNK)
    mats = [np.tile(same & cond, (1, 2)) for cond in (ci <= ri, ci >= ri)]
    return jnp.asarray(np.stack(mats).astype(np.float32), dtype=BF16)


def _flags(dmax, tiles_per_block):
    worst = jnp.max(dmax[:, 0, 0].reshape(-1, tiles_per_block), axis=1)
    return (worst > SAFE_DECAY).astype(jnp.int32)


def kernel(x, c, ctx, c_ctx, w_mod, b_mod, norm_g, w_in, w_gate_up_f, b_gate_f, w_gate_up_b,
           b_gate_b, gla_norm_g, w_pool, pool_scale, w_branch_a, w_branch_b, w_out, final_norm_g):
    B, L, _ = x.shape
    assert w_mod.shape[0] == 1, "single-layer configuration"
    wi = w_in[0]
    w1 = wi[:, :W_IN_FIRST].astype(BF16)
    w2 = wi[:, W_IN_FIRST + 2 * RANK:].astype(BF16)
    w_lr = jnp.pad(wi[:, W_IN_FIRST:W_IN_FIRST + 2 * RANK],
                   ((0, 0), (0, LANES - 2 * RANK))).astype(BF16)
    w_up = jnp.zeros((LANES, 2 * KW), F32)
    w_up = w_up.at[:RANK, :KW].set(w_gate_up_f[0]).at[RANK:2 * RANK, KW:].set(w_gate_up_b[0])
    w_up = w_up.astype(BF16)
    b_up = jnp.concatenate([b_gate_f[0], b_gate_b[0]])[None, :]

    c_rows = jnp.zeros((MOD_ROWS, D_MODEL), F32).at[:B].set(c).at[B].set(c_ctx)
    mod = _modulation(c_rows, w_mod[0], b_mod[0][None, :])
    mod3 = mod[:, None, :]

    ng = norm_g[0][None, :]
    weights = (w1, w2, w_lr, w_up, b_up, _chunk_sum_matrices())

    *_, s_f, s_b = _project(ctx, mod3, lambda b: B, ng, None, weights, None, context=True)

    q, k, v, za, ub, zb, ga, gb, bf, bb, dm, sb_states = _project(
        x, mod3, lambda b: b, ng, _rope_tables(L), weights, s_b, context=False)
    return _tail(_flags(dm, TAIL_TILE // PROJ_TILE), q, k, v, bf, bb, sb_states, s_f, za,
                 gla_norm_g[0][None, :], ub, zb, ga, gb, x, mod3, w_pool[0].astype(BF16),
                 pool_scale[0][None, :], w_branch_a[0].astype(BF16), w_branch_b[0].astype(BF16),
                 w_out[0].astype(BF16), final_norm_g[None, :])
```

```python
import functools

import numpy as np
import jax
import jax.numpy as jnp
from jax import lax
from jax.experimental import pallas as pl
from jax.experimental.pallas import tpu as pltpu

F32 = jnp.float32
BF16 = jnp.bfloat16

D_MODEL = 1024
HEADS = 4
DK = 128
DV = 256
KW = HEADS * DK
VW = HEADS * DV
RANK = 16
GATE_NORM = 16.0
CHUNK = 256
GRID_W = 64
ROPE_BASE = 10000.0
POOL_WINDOWS = (2, 4, 8, 16)
POOL_GW = D_MODEL // len(POOL_WINDOWS)
EPS = 1e-6

LANES = 128
SUBLANES = 8
V7X_VMEM_BYTES = 64 * 1024 * 1024
ROT_HALF = DK // 4
MOD_ROWS = 16
HALO = 16
PROJ_TILE = 512
TAIL_TILE = 512
SUB_TILE = 256
OUT_COLS = 512
SAFE_DECAY = 60.0
VMEM_LIMIT = V7X_VMEM_BYTES * 7 // 8

_C_Q, _C_K, _C_V, _C_ZA = 0, KW, 2 * KW, 2 * KW + VW
_C_UB, _C_ZB, _C_GA, _C_GB = 0, D_MODEL, 2 * D_MODEL, 3 * D_MODEL
W_IN_FIRST = 2 * KW + 2 * VW


def _sigmoid(x):
    return 0.5 * jnp.tanh(0.5 * x) + 0.5


def _silu(x):
    h = 0.5 * x
    return h + h * jnp.tanh(h)


def _mod_kernel(c_ref, w_ref, b_ref, o_ref):
    c = c_ref[...]
    s = c * jax.nn.sigmoid(c)
    o_ref[...] = jnp.dot(s, w_ref[...], preferred_element_type=F32,
                         precision=lax.Precision.HIGHEST) + b_ref[...]


def _modulation(c_rows, w_mod, b_mod):
    return pl.pallas_call(
        _mod_kernel,
        out_shape=jax.ShapeDtypeStruct((c_rows.shape[0], w_mod.shape[1]), F32),
        compiler_params=pltpu.CompilerParams(vmem_limit_bytes=VMEM_LIMIT),
    )(c_rows, w_mod, b_mod)


def _as_column(row):
    return jnp.broadcast_to(row, (SUBLANES, DK)).T[:, 0:1]


def _state_step(state, hd, k, b, v, edge):
    b_edge = b[edge:edge + 1, :]
    kd_t = (k * jnp.exp(b_edge - b)).T.astype(BF16)
    state[hd] = (jnp.exp(_as_column(b_edge)) * state[hd]
                 + jnp.dot(kd_t, v, preferred_element_type=F32))


def _proj_kernel(x_ref, shift_ref, scale_ref, g_ref, *rest, context):
    if context:
        (w1_ref, wlr_ref, wup_ref, bup_ref, tri_ref, k_ref, v_ref, bf_ref, bb_ref, dmax_ref,
         sf_ref, sb_ref, state_f, state_b) = rest
    else:
        (tab_ref, w1_ref, w2_ref, wlr_ref, wup_ref, bup_ref, tri_ref, s0_ref,
         q_ref, k_ref, v_ref, za_ref, ub_ref, zb_ref, ga_ref, gb_ref,
         bf_ref, bb_ref, dmax_ref, sb_ref, state_b) = rest
    tm = x_ref.shape[1]
    sub = min(tm, SUB_TILE)
    worst = []

    if context:
        state_f[...] = jnp.zeros_like(state_f)
        state_b[...] = jnp.zeros_like(state_b)
    else:
        @pl.when(pl.program_id(1) == 0)
        def _():
            state_b[...] = s0_ref[0]

    def stages(r0):
        rows = slice(r0, r0 + sub)
        st = {}

        def prologue():
            x = x_ref[0, rows, :]
            ms = jnp.mean(x * x, axis=-1, keepdims=True)
            h = (x * lax.rsqrt(ms + EPS)) * g_ref[...]
            h = h * (1.0 + scale_ref[0]) + shift_ref[0]
            st["hb"] = h.astype(BF16)
            lr = jnp.dot(st["hb"], wlr_ref[...], preferred_element_type=F32).astype(BF16)
            gk = jnp.dot(lr, wup_ref[...], preferred_element_type=F32) + bup_ref[...]
            la = (jnp.minimum(gk, 0.0) - jnp.log(1.0 + jnp.exp(-jnp.abs(gk)))) * (1.0 / GATE_NORM)
            hi = la.astype(BF16)
            lo = (la - hi.astype(F32)).astype(BF16)
            pieces = jnp.concatenate([hi, lo], axis=0)
            cf = jnp.dot(tri_ref[0], pieces[:, :KW], preferred_element_type=F32)
            cb = jnp.dot(tri_ref[1], pieces[:, KW:], preferred_element_type=F32)
            bf_ref[0, rows, :] = cf
            bb_ref[0, rows, :] = cb
            for c0 in range(0, sub, CHUNK):
                worst.append(jnp.maximum(-cf[c0 + CHUNK - 1:c0 + CHUNK, :], -cb[c0:c0 + 1, :]))

        def plain(w_ref, c0, dst_ref, act=None):
            def run():
                p = jnp.dot(st["hb"], w_ref[:, c0:c0 + dst_ref.shape[2]],
                            preferred_element_type=F32)
                dst_ref[0, rows, :] = (p if act is None else act(p)).astype(dst_ref.dtype)
            return run

        def rotary(c0, dst_ref, scale):
            def run():
                p = jnp.dot(st["hb"], w1_ref[:, c0:c0 + KW], preferred_element_type=F32)
                cos, sin_up, sin_dn = (tab_ref[n, rows, :] for n in range(3))
                for hd in range(HEADS):
                    ph = p[:, hd * DK:(hd + 1) * DK]
                    out = (ph * cos + pltpu.roll(ph, LANES - ROT_HALF, 1) * sin_up
                           + pltpu.roll(ph, ROT_HALF, 1) * sin_dn)
                    if scale is not None:
                        out = out * scale
                    dst_ref[0, rows, hd * DK:(hd + 1) * DK] = out.astype(dst_ref.dtype)
            return run

        if context:
            return prologue, [plain(w1_ref, _C_K, k_ref), plain(w1_ref, _C_V, v_ref)]
        return prologue, [
            rotary(_C_K, k_ref, None), plain(w1_ref, _C_V, v_ref), rotary(_C_Q, q_ref, DK ** -0.5),
            plain(w1_ref, _C_ZA, za_ref, _silu), plain(w2_ref, _C_UB, ub_ref),
            plain(w2_ref, _C_ZB, zb_ref, _silu), plain(w2_ref, _C_GA, ga_ref, _sigmoid),
            plain(w2_ref, _C_GB, gb_ref, _sigmoid)]

    def scan_steps(r0, state, b_ref, reverse, emit):
        steps = []
        starts = range(r0, r0 + sub, CHUNK)
        for c0 in (reversed(starts) if reverse else starts):
            for hd in range(HEADS):
                def run(c0=c0, hd=hd):
                    rows = slice(c0, c0 + CHUNK)
                    ks = slice(hd * DK, (hd + 1) * DK)
                    if emit:
                        sb_ref[0, c0 // CHUNK, ks, :] = state[hd].astype(BF16)
                    _state_step(state, hd, k_ref[0, rows, ks].astype(F32), b_ref[0, rows, ks],
                                v_ref[0, rows, hd * DV:(hd + 1) * DV], 0 if reverse else CHUNK - 1)
                steps.append(run)
        return steps

    starts = list(reversed(range(0, tm, sub)))
    tiles = [stages(r0) for r0 in starts]
    tiles[0][0]()
    for n, (_, matmuls) in enumerate(tiles):
        for step in matmuls[:2]:
            step()
        fill = scan_steps(starts[n], state_b, bb_ref, True, not context)
        if context:
            fill += scan_steps(starts[n], state_f, bf_ref, False, False)
        if n + 1 < len(tiles):
            fill.append(tiles[n + 1][0])
        _interleave(matmuls[2:], fill)
    dmax_ref[0] = jnp.broadcast_to(
        jnp.max(functools.reduce(jnp.maximum, worst), axis=1, keepdims=True), dmax_ref.shape[1:])
    if context:
        sf_ref[0] = state_f[...]
        sb_ref[0] = state_b[...]


def _interleave(*lists):
    order = sorted(((i + 0.5) / len(lst), which, i)
                   for which, lst in enumerate(lists) for i in range(len(lst)))
    for _, which, idx in order:
        lists[which][idx]()


def _resident(arr):
    return pl.BlockSpec(arr.shape, lambda *_: (0,) * arr.ndim, pipeline_mode=pl.Buffered(1))


def _project(x, mod3, mod_row, norm_g, tables, weights, s0, *, context):
    B, L, _ = x.shape
    tm = min(L, PROJ_TILE)
    nt = L // tm
    tile = (lambda i: i) if context else (lambda i: nt - 1 - i)
    tok = lambda width: pl.BlockSpec((1, tm, width), lambda b, i: (b, tile(i), 0))
    bf = lambda width: jax.ShapeDtypeStruct((B, L, width), BF16)
    st_spec = pl.BlockSpec((1, HEADS, DK, DV), lambda b, i: (b, 0, 0, 0))
    st_shape = jax.ShapeDtypeStruct((B, HEADS, DK, DV), F32)
    state = pltpu.VMEM((HEADS, DK, DV), F32)
    if context:
        assert nt == 1, "context stream is one tile per sample"
        widths = [KW, VW]
        extra = [weights[0]] + list(weights[2:])
        extra_specs = [_resident(w) for w in extra]
        state_specs, state_shapes, scratch = [st_spec, st_spec], [st_shape, st_shape], [state, state]
    else:
        widths = [KW, KW, VW, VW, D_MODEL, D_MODEL, D_MODEL, D_MODEL]
        extra = [tables] + list(weights) + [s0]
        extra_specs = ([pl.BlockSpec((3, tm, LANES), lambda b, i: (0, tile(i), 0))]
                       + [_resident(w) for w in weights] + [st_spec])
        state_specs = [pl.BlockSpec((1, tm // CHUNK, KW, DV), lambda b, i: (b, tile(i), 0, 0))]
        state_shapes = [jax.ShapeDtypeStruct((B, L // CHUNK, KW, DV), BF16)]
        scratch = [state]
    return pl.pallas_call(
        functools.partial(_proj_kernel, context=context),
        grid=(B, nt),
        in_specs=[
            tok(D_MODEL),
            pl.BlockSpec((1, 1, D_MODEL), lambda b, i: (mod_row(b), 0, 0)),
            pl.BlockSpec((1, 1, D_MODEL), lambda b, i: (mod_row(b), 0, 1)),
            _resident(norm_g),
        ] + extra_specs,
        out_specs=[tok(w) for w in widths] + [
            tok(KW), tok(KW),
            pl.BlockSpec((1, SUBLANES, LANES), lambda b, i: (b * nt + tile(i), 0, 0))] + state_specs,
        out_shape=[bf(w) for w in widths] + [
            jax.ShapeDtypeStruct((B, L, KW), F32), jax.ShapeDtypeStruct((B, L, KW), F32),
            jax.ShapeDtypeStruct((B * nt, SUBLANES, LANES), F32)] + state_shapes,
        scratch_shapes=scratch,
        compiler_params=pltpu.CompilerParams(
            dimension_semantics=("parallel", "arbitrary"), vmem_limit_bytes=VMEM_LIMIT),
    )(x, mod3, mod3, norm_g, *extra)


def _tail_kernel(flags_ref, q_ref, k_ref, v_ref, bf_ref, bb_ref, sb_ref, s0_ref, z_ref, g_ref,
                 up_ref, uc_ref, un_ref, zb_ref, ga_ref, gb_ref, x_ref, gate_ref,
                 wpool_ref, pscale_ref, wa_ref, wb_ref, wo_ref, fg_ref,
                 o_ref, state, a_scr, k_rows, bf_rows, bb_rows, *, nblk, seq_len):
    bi = pl.program_id(0)
    i = pl.program_id(1)
    tm = q_ref.shape[1]
    sub = min(tm, SUB_TILE)
    ext = sub + 2 * HALO
    cw = OUT_COLS
    ncol = D_MODEL // cw
    ngrp = len(POOL_WINDOWS)

    @pl.when(i == 0)
    def _():
        state[...] = s0_ref[0]

    exact_path = flags_ref[bi * nblk + i] != 0
    row = lax.broadcasted_iota(jnp.int32, (CHUNK, CHUNK), 0)
    col = lax.broadcasted_iota(jnp.int32, (CHUNK, CHUNK), 1)
    nt_dims = (((1,), (1,)), ((), ()))

    def head_chunk(c, hd, exact):
        rows = slice(c * CHUNK, (c + 1) * CHUNK)
        ks = slice(hd * DK, (hd + 1) * DK)
        vs = slice(hd * DV, (hd + 1) * DV)
        q = q_ref[0, rows, ks].astype(F32)
        k = k_ref[0, rows, ks].astype(F32)
        v = v_ref[0, rows, vs]
        bf = bf_ref[0, rows, ks]
        bb = bb_ref[0, rows, ks]
        qf = (q * jnp.exp(bf)).astype(BF16)
        qb = (q * jnp.exp(bb)).astype(BF16)
        if not exact:
            kf = (k * jnp.exp(-bf)).astype(BF16)
            kb = (k * jnp.exp(-bb)).astype(BF16)
            sc = (jnp.where(col <= row, lax.dot_general(qf, kf, nt_dims,
                                                        preferred_element_type=F32), 0.0)
                  + jnp.where(col >= row, lax.dot_general(qb, kb, nt_dims,
                                                          preferred_element_type=F32), 0.0))
        else:
            k_rows[...] = k
            bf_rows[...] = bf
            bb_rows[...] = bb
            tok = lax.broadcasted_iota(jnp.int32, (CHUNK, DK), 0)

            def col_body(j, acc):
                dec = (jnp.where(tok >= j, jnp.exp(jnp.minimum(bf - bf_rows[pl.ds(j, 1), :], 0.0)), 0.0)
                       + jnp.where(tok <= j, jnp.exp(jnp.minimum(bb - bb_rows[pl.ds(j, 1), :], 0.0)), 0.0))
                w = q * k_rows[pl.ds(j, 1), :] * dec
                return jnp.where(col == j, jnp.sum(w, axis=1, keepdims=True), acc)

            sc = lax.fori_loop(0, CHUNK, col_body, jnp.zeros((CHUNK, CHUNK), F32))
        s_cat = jnp.concatenate([state[hd].astype(BF16), sb_ref[0, c, ks, :]], axis=0)
        q_cat = jnp.concatenate([qf, qb], axis=1)
        o = (jnp.dot(sc.astype(BF16), v, preferred_element_type=F32)
             + jnp.dot(q_cat, s_cat, preferred_element_type=F32))
        _state_step(state, hd, k, bf, v, CHUNK - 1)
        ms = jnp.mean(o * o, axis=-1, keepdims=True)
        on = o * lax.rsqrt(ms + EPS) * g_ref[...]
        a_scr[rows, vs] = (on * z_ref[0, rows, vs].astype(F32)).astype(a_scr.dtype)

    def out_stages(r0):
        rows = slice(r0, r0 + sub)
        st = {"bm": [None] * ngrp, "ya": [None] * ncol, "mg": [None] * ncol, "ss": []}

        def pool_input(cs):
            if r0 == 0:
                before = jnp.where(i > 0, up_ref[0, :, cs].astype(F32), 0.0)
            else:
                before = uc_ref[0, r0 - HALO:r0, cs].astype(F32)
            if r0 + sub == tm:
                after = jnp.where(i < nblk - 1, un_ref[0, :, cs].astype(F32), 0.0)
            else:
                after = uc_ref[0, r0 + sub:r0 + sub + HALO, cs].astype(F32)
            return jnp.concatenate([before, uc_ref[0, rows, cs].astype(F32), after], axis=0)

        def pool(gi):
            def run():
                w = POOL_WINDOWS[gi]
                cs = slice(gi * POOL_GW, (gi + 1) * POOL_GW)
                u_ext = pool_input(cs)
                lead = w // 2 - 1
                acc = pltpu.roll(u_ext, ext - lead, 0) if lead else u_ext
                s = 1
                while s < w:
                    acc = acc + pltpu.roll(acc, s, 0)
                    s *= 2
                t = i * tm + r0 + lax.broadcasted_iota(jnp.int32, (sub, LANES), 0)
                lo = jnp.maximum(t - w // 2, 0)
                hi = jnp.minimum(t + (w - 1 - w // 2), seq_len - 1)
                inv_cnt = 1.0 / (hi - lo + 1).astype(F32)
                inv_cnt = jnp.concatenate([inv_cnt] * (POOL_GW // LANES), axis=1)
                d = acc[HALO:HALO + sub] * inv_cnt - u_ext[HALO:HALO + sub]
                pooled = jnp.dot(d.astype(BF16), wpool_ref[gi], preferred_element_type=F32)
                st["bm"][gi] = (pooled * pscale_ref[:, cs]
                                * zb_ref[0, rows, cs].astype(F32)).astype(BF16)
            return run

        def branch_a(c):
            def run():
                cs = slice(c * cw, (c + 1) * cw)
                y_a = jnp.dot(a_scr[rows, :], wa_ref[:, cs], preferred_element_type=F32)
                st["ya"][c] = ga_ref[0, rows, cs].astype(F32) * y_a
            return run

        def branch_b(c):
            def run():
                cs = slice(c * cw, (c + 1) * cw)
                if c == 0:
                    st["bm"] = jnp.concatenate(st["bm"], axis=1)
                y_b = jnp.dot(st["bm"], wb_ref[:, cs], preferred_element_type=F32)
                st["mg"][c] = (st["ya"][c] + gb_ref[0, rows, cs].astype(F32) * y_b).astype(BF16)
            return run

        def output(c):
            def run():
                cs = slice(c * cw, (c + 1) * cw)
                if c == 0:
                    st["mg"] = jnp.concatenate(st["mg"], axis=1)
                y = jnp.dot(st["mg"], wo_ref[:, cs], preferred_element_type=F32)
                xn = x_ref[0, rows, cs] + gate_ref[0, :, cs] * y
                o_ref[0, rows, cs] = xn
                st["ss"].append(jnp.sum(xn * xn, axis=-1, keepdims=True))
            return run

        def normalise(c):
            def run():
                cs = slice(c * cw, (c + 1) * cw)
                ms = functools.reduce(jnp.add, st["ss"]) * (1.0 / D_MODEL)
                o_ref[0, rows, cs] = o_ref[0, rows, cs] * lax.rsqrt(ms + EPS) * fg_ref[:, cs]
            return run

        first = sorted([((c + 0.5) / ncol, 0, branch_a(c)) for c in range(ncol)]
                       + [((gi + 0.5) / ngrp, 1, pool(gi)) for gi in range(ngrp)],
                       key=lambda item: item[:2])
        return ([[item[2] for item in first]]
                + [[f(c) for c in range(ncol)] for f in (branch_b, output, normalise)])

    def body(exact):
        per_sub = sub // CHUNK
        gla = [[functools.partial(head_chunk, c, hd, exact)
                for c in range(n * per_sub, (n + 1) * per_sub) for hd in range(HEADS)]
               for n in range(tm // sub)]
        tiles = [out_stages(r0) for r0 in range(0, tm, sub)]
        nstage = len(tiles[0])
        for step in gla[0]:
            step()
        for step in range(len(tiles) + nstage - 1):
            lists = [tl[step - n] for n, tl in enumerate(tiles) if 0 <= step - n < nstage]
            if step + 1 < len(gla):
                lists.append(gla[step + 1])
            _interleave(*lists)

    for exact in (False, True):
        @pl.when(exact_path == exact)
        def _():
            body(exact)


def _tail(flags, q, k, v, bf, bb, sb, s0, z, g, ub, zb, ga, gb, x, mod3,
          w_pool, pool_scale, w_a, w_b, w_o, final_g):
    B, L, _ = x.shape
    tm = TAIL_TILE
    nblk = L // tm
    per_halo = tm // HALO
    n_halo = L // HALO
    tok = lambda width: pl.BlockSpec((1, tm, width), lambda bi, i, fl: (bi, i, 0))
    return pl.pallas_call(
        functools.partial(_tail_kernel, nblk=nblk, seq_len=L),
        grid_spec=pltpu.PrefetchScalarGridSpec(
            num_scalar_prefetch=1, grid=(B, nblk),
            in_specs=[
                tok(KW), tok(KW), tok(VW), tok(KW), tok(KW),
                pl.BlockSpec((1, tm // CHUNK, KW, DV), lambda bi, i, fl: (bi, i, 0, 0)),
                pl.BlockSpec((1, HEADS, DK, DV), lambda bi, i, fl: (bi, 0, 0, 0)),
                tok(VW), _resident(g),
                pl.BlockSpec((1, HALO, D_MODEL),
                             lambda bi, i, fl: (bi, jnp.maximum(i * per_halo - 1, 0), 0)),
                tok(D_MODEL),
                pl.BlockSpec((1, HALO, D_MODEL),
                             lambda bi, i, fl: (bi, jnp.minimum((i + 1) * per_halo, n_halo - 1), 0)),
                tok(D_MODEL), tok(D_MODEL), tok(D_MODEL), tok(D_MODEL),
                pl.BlockSpec((1, 1, D_MODEL), lambda bi, i, fl: (bi, 0, 2)),
                _resident(w_pool), _resident(pool_scale), _resident(w_a), _resident(w_b),
                _resident(w_o), _resident(final_g),
            ],
            out_specs=tok(D_MODEL),
            scratch_shapes=[pltpu.VMEM((HEADS, DK, DV), F32), pltpu.VMEM((tm, VW), BF16)]
                           + [pltpu.VMEM((CHUNK, DK), F32)] * 3),
        out_shape=jax.ShapeDtypeStruct((B, L, D_MODEL), F32),
        compiler_params=pltpu.CompilerParams(
            dimension_semantics=("parallel", "arbitrary"), vmem_limit_bytes=VMEM_LIMIT),
    )(flags, q, k, v, bf, bb, sb, s0, z, g, ub, ub, ub, zb, ga, gb, x, mod3,
      w_pool, pool_scale, w_a, w_b, w_o, final_g)


def _rope_tables(seq_len):
    nf = ROT_HALF
    t = np.arange(seq_len)
    freqs = ROPE_BASE ** (-np.arange(nf, dtype=np.float64) / nf)
    rowp, colp = (t // GRID_W)[:, None], (t % GRID_W)[:, None]
    ang = np.concatenate([rowp * freqs] * 2 + [colp * freqs] * 2, axis=1)
    cos, sin = np.cos(ang), np.sin(ang)
    first = (np.arange(LANES) % (2 * nf)) < nf
    sin_up = np.where(first, -sin, 0.0)
    sin_dn = np.where(first, 0.0, sin)
    return jnp.asarray(np.stack([cos, sin_up, sin_dn]).astype(np.float32))


def _chunk_sum_matrices():
    ri = np.arange(SUB_TILE)[:, None]
    ci = np.arange(SUB_TILE)[None, :]
    same = (ri // CHUNK) == (ci // CHUNK)
    mats = [np.tile(same & cond, (1, 2)) for cond in (ci <= ri, ci >= ri)]
    return jnp.asarray(np.stack(mats).astype(np.float32), dtype=BF16)


def _flags(dmax, tiles_per_block):
    worst = jnp.max(dmax[:, 0, 0].reshape(-1, tiles_per_block), axis=1)
    return (worst > SAFE_DECAY).astype(jnp.int32)


def kernel(x, c, ctx, c_ctx, w_mod, b_mod, norm_g, w_in, w_gate_up_f, b_gate_f, w_gate_up_b,
           b_gate_b, gla_norm_g, w_pool, pool_scale, w_branch_a, w_branch_b, w_out, final_norm_g):
    B, L, _ = x.shape
    assert w_mod.shape[0] == 1, "single-layer configuration"
    wi = w_in[0]
    wi = lax.optimization_barrier(wi.astype(BF16))
    w1 = wi[:, :W_IN_FIRST]
    w2 = wi[:, W_IN_FIRST + 2 * RANK:]
    w_lr = jnp.pad(wi[:, W_IN_FIRST:W_IN_FIRST + 2 * RANK], ((0, 0), (0, LANES - 2 * RANK)))
    w_up = jnp.zeros((LANES, 2 * KW), F32)
    w_up = w_up.at[:RANK, :KW].set(w_gate_up_f[0]).at[RANK:2 * RANK, KW:].set(w_gate_up_b[0])
    w_up = w_up.astype(BF16)
    b_up = jnp.concatenate([b_gate_f[0], b_gate_b[0]])[None, :]

    c_rows = jnp.zeros((MOD_ROWS, D_MODEL), F32).at[:B].set(c).at[B].set(c_ctx)
    mod = _modulation(c_rows, w_mod[0], b_mod[0][None, :])
    mod3 = mod[:, None, :]

    ng = norm_g[0][None, :]
    weights = (w1, w2, w_lr, w_up, b_up, _chunk_sum_matrices())

    *_, s_f, s_b = _project(ctx, mod3, lambda b: B, ng, None, weights, None, context=True)

    q, k, v, za, ub, zb, ga, gb, bf, bb, dm, sb_states = _project(
        x, mod3, lambda b: b, ng, _rope_tables(L), weights, s_b, context=False)
    return _tail(_flags(dm, TAIL_TILE // PROJ_TILE), q, k, v, bf, bb, sb_states, s_f, za,
                 gla_norm_g[0][None, :], ub, zb, ga, gb, x, mod3, w_pool[0].astype(BF16),
                 pool_scale[0][None, :], w_branch_a[0].astype(BF16), w_branch_b[0].astype(BF16),
                 w_out[0].astype(BF16), final_norm_g[None, :])
```

```python
import functools

import numpy as np
import jax
import jax.numpy as jnp
from jax import lax
from jax.experimental import pallas as pl
from jax.experimental.pallas import tpu as pltpu

F32 = jnp.float32
BF16 = jnp.bfloat16

D_MODEL = 1024
HEADS = 4
DK = 128
DV = 256
KW = HEADS * DK
VW = HEADS * DV
RANK = 16
GATE_NORM = 16.0
CHUNK = 256
GRID_W = 64
ROPE_BASE = 10000.0
POOL_WINDOWS = (2, 4, 8, 16)
POOL_GW = D_MODEL // len(POOL_WINDOWS)
EPS = 1e-6

LANES = 128
SUBLANES = 8
V7X_VMEM_BYTES = 64 * 1024 * 1024
ROT_HALF = DK // 4
MOD_ROWS = 16
HALO = 16
PROJ_TILE = 512
TAIL_TILE = 512
SUB_TILE = 256
OUT_COLS = 512
SAFE_DECAY = 60.0
VMEM_LIMIT = V7X_VMEM_BYTES * 7 // 8

_C_Q, _C_K, _C_V, _C_ZA = 0, KW, 2 * KW, 2 * KW + VW
_C_UB, _C_ZB, _C_GA, _C_GB = 0, D_MODEL, 2 * D_MODEL, 3 * D_MODEL
W_IN_FIRST = 2 * KW + 2 * VW


def _sigmoid(x):
    return 0.5 * jnp.tanh(0.5 * x) + 0.5


def _silu(x):
    h = 0.5 * x
    return h + h * jnp.tanh(h)


def _mod_kernel(c_ref, w_ref, b_ref, o_ref):
    c = c_ref[...]
    s = c * jax.nn.sigmoid(c)
    o_ref[...] = jnp.dot(s, w_ref[...], preferred_element_type=F32,
                         precision=lax.Precision.HIGHEST) + b_ref[...]


def _modulation(c_rows, w_mod, b_mod):
    return pl.pallas_call(
        _mod_kernel,
        out_shape=jax.ShapeDtypeStruct((c_rows.shape[0], w_mod.shape[1]), F32),
        compiler_params=pltpu.CompilerParams(vmem_limit_bytes=VMEM_LIMIT),
    )(c_rows, w_mod, b_mod)


def _as_column(row):
    return jnp.broadcast_to(row, (SUBLANES, DK)).T[:, 0:1]


def _state_step(state, hd, k, b, v, edge):
    b_edge = b[edge:edge + 1, :]
    kd_t = (k * jnp.exp(b_edge - b)).T.astype(BF16)
    state[hd] = (jnp.exp(_as_column(b_edge)) * state[hd]
                 + jnp.dot(kd_t, v, preferred_element_type=F32))


def _proj_kernel(x_ref, shift_ref, scale_ref, g_ref, *rest, context):
    if context:
        (w1_ref, wlr_ref, wup_ref, bup_ref, tri_ref, k_ref, v_ref, bf_ref, bb_ref, dmax_ref,
         sf_ref, sb_ref, state_f, state_b) = rest
    else:
        (tab_ref, w1_ref, w2_ref, wlr_ref, wup_ref, bup_ref, tri_ref, s0_ref,
         q_ref, k_ref, v_ref, za_ref, ub_ref, zb_ref, ga_ref, gb_ref,
         bf_ref, bb_ref, dmax_ref, sb_ref, state_b) = rest
    tm = x_ref.shape[1]
    sub = min(tm, SUB_TILE)
    worst = []

    if context:
        state_f[...] = jnp.zeros_like(state_f)
        state_b[...] = jnp.zeros_like(state_b)
    else:
        @pl.when(pl.program_id(1) == 0)
        def _():
            state_b[...] = s0_ref[0]

    def stages(r0):
        rows = slice(r0, r0 + sub)
        st = {}

        def prologue():
            x = x_ref[0, rows, :]
            ms = jnp.mean(x * x, axis=-1, keepdims=True)
            h = (x * lax.rsqrt(ms + EPS)) * g_ref[...]
            h = h * (1.0 + scale_ref[0]) + shift_ref[0]
            st["hb"] = h.astype(BF16)
            lr = jnp.dot(st["hb"], wlr_ref[...], preferred_element_type=F32).astype(BF16)
            gk = jnp.dot(lr, wup_ref[...], preferred_element_type=F32) + bup_ref[...]
            la = (jnp.minimum(gk, 0.0) - jnp.log(1.0 + jnp.exp(-jnp.abs(gk)))) * (1.0 / GATE_NORM)
            hi = la.astype(BF16)
            lo = (la - hi.astype(F32)).astype(BF16)
            pieces = jnp.concatenate([hi, lo], axis=0)
            cf = jnp.dot(tri_ref[0], pieces[:, :KW], preferred_element_type=F32)
            cb = jnp.dot(tri_ref[1], pieces[:, KW:], preferred_element_type=F32)
            bf_ref[0, rows, :] = cf
            bb_ref[0, rows, :] = cb
            for c0 in range(0, sub, CHUNK):
                worst.append(jnp.maximum(-cf[c0 + CHUNK - 1:c0 + CHUNK, :], -cb[c0:c0 + 1, :]))

        def plain(w_ref, c0, dst_ref, act=None):
            def run():
                p = jnp.dot(st["hb"], w_ref[:, c0:c0 + dst_ref.shape[2]],
                            preferred_element_type=F32)
                dst_ref[0, rows, :] = (p if act is None else act(p)).astype(dst_ref.dtype)
            return run

        def rotary(c0, dst_ref, scale):
            def run():
                p = jnp.dot(st["hb"], w1_ref[:, c0:c0 + KW], preferred_element_type=F32)
                cos, sin_up, sin_dn = (tab_ref[n, rows, :] for n in range(3))
                for hd in range(HEADS):
                    ph = p[:, hd * DK:(hd + 1) * DK]
                    out = (ph * cos + pltpu.roll(ph, LANES - ROT_HALF, 1) * sin_up
                           + pltpu.roll(ph, ROT_HALF, 1) * sin_dn)
                    if scale is not None:
                        out = out * scale
                    dst_ref[0, rows, hd * DK:(hd + 1) * DK] = out.astype(dst_ref.dtype)
            return run

        if context:
            return prologue, [plain(w1_ref, _C_K, k_ref), plain(w1_ref, _C_V, v_ref)]
        return prologue, [
            rotary(_C_K, k_ref, None), plain(w1_ref, _C_V, v_ref), rotary(_C_Q, q_ref, DK ** -0.5),
            plain(w1_ref, _C_ZA, za_ref, _silu), plain(w2_ref, _C_UB, ub_ref),
            plain(w2_ref, _C_ZB, zb_ref, _silu), plain(w2_ref, _C_GA, ga_ref, _sigmoid),
            plain(w2_ref, _C_GB, gb_ref, _sigmoid)]

    def scan_steps(r0, state, b_ref, reverse, emit):
        steps = []
        starts = range(r0, r0 + sub, CHUNK)
        for c0 in (reversed(starts) if reverse else starts):
            for hd in range(HEADS):
                def run(c0=c0, hd=hd):
                    rows = slice(c0, c0 + CHUNK)
                    ks = slice(hd * DK, (hd + 1) * DK)
                    if emit:
                        sb_ref[0, c0 // CHUNK, ks, :] = state[hd].astype(BF16)
                    _state_step(state, hd, k_ref[0, rows, ks].astype(F32), b_ref[0, rows, ks],
                                v_ref[0, rows, hd * DV:(hd + 1) * DV], 0 if reverse else CHUNK - 1)
                steps.append(run)
        return steps

    starts = list(reversed(range(0, tm, sub)))
    tiles = [stages(r0) for r0 in starts]
    tiles[0][0]()
    for n, (_, matmuls) in enumerate(tiles):
        for step in matmuls[:2]:
            step()
        fill = scan_steps(starts[n], state_b, bb_ref, True, not context)
        if context:
            fill += scan_steps(starts[n], state_f, bf_ref, False, False)
        if n + 1 < len(tiles):
            fill.append(tiles[n + 1][0])
        _interleave(matmuls[2:], fill)
    dmax_ref[0] = jnp.broadcast_to(
        jnp.max(functools.reduce(jnp.maximum, worst), axis=1, keepdims=True), dmax_ref.shape[1:])
    if context:
        sf_ref[0] = state_f[...]
        sb_ref[0] = state_b[...]


def _interleave(*lists):
    order = sorted(((i + 0.5) / len(lst), which, i)
                   for which, lst in enumerate(lists) for i in range(len(lst)))
    for _, which, idx in order:
        lists[which][idx]()


def _resident(arr):
    return pl.BlockSpec(arr.shape, lambda *_: (0,) * arr.ndim, pipeline_mode=pl.Buffered(1))


def _project(x, mod3, mod_row, norm_g, tables, weights, s0, *, context):
    B, L, _ = x.shape
    tm = min(L, PROJ_TILE)
    nt = L // tm
    tile = (lambda i: i) if context else (lambda i: nt - 1 - i)
    tok = lambda width: pl.BlockSpec((1, tm, width), lambda b, i: (b, tile(i), 0))
    bf = lambda width: jax.ShapeDtypeStruct((B, L, width), BF16)
    st_spec = pl.BlockSpec((1, HEADS, DK, DV), lambda b, i: (b, 0, 0, 0))
    st_shape = jax.ShapeDtypeStruct((B, HEADS, DK, DV), F32)
    state = pltpu.VMEM((HEADS, DK, DV), F32)
    if context:
        assert nt == 1, "context stream is one tile per sample"
        widths = [KW, VW]
        extra = [weights[0]] + list(weights[2:])
        extra_specs = [_resident(w) for w in extra]
        state_specs, state_shapes, scratch = [st_spec, st_spec], [st_shape, st_shape], [state, state]
    else:
        widths = [KW, KW, VW, VW, D_MODEL, D_MODEL, D_MODEL, D_MODEL]
        extra = [tables] + list(weights) + [s0]
        extra_specs = ([pl.BlockSpec((3, tm, LANES), lambda b, i: (0, tile(i), 0))]
                       + [_resident(w) for w in weights] + [st_spec])
        state_specs = [pl.BlockSpec((1, tm // CHUNK, KW, DV), lambda b, i: (b, tile(i), 0, 0))]
        state_shapes = [jax.ShapeDtypeStruct((B, L // CHUNK, KW, DV), BF16)]
        scratch = [state]
    return pl.pallas_call(
        functools.partial(_proj_kernel, context=context),
        grid=(B, nt),
        in_specs=[
            tok(D_MODEL),
            pl.BlockSpec((1, 1, D_MODEL), lambda b, i: (mod_row(b), 0, 0)),
            pl.BlockSpec((1, 1, D_MODEL), lambda b, i: (mod_row(b), 0, 1)),
            _resident(norm_g),
        ] + extra_specs,
        out_specs=[tok(w) for w in widths] + [
            tok(KW), tok(KW),
            pl.BlockSpec((1, SUBLANES, LANES), lambda b, i: (b * nt + tile(i), 0, 0))] + state_specs,
        out_shape=[bf(w) for w in widths] + [
            jax.ShapeDtypeStruct((B, L, KW), F32), jax.ShapeDtypeStruct((B, L, KW), F32),
            jax.ShapeDtypeStruct((B * nt, SUBLANES, LANES), F32)] + state_shapes,
        scratch_shapes=scratch,
        compiler_params=pltpu.CompilerParams(
            dimension_semantics=("parallel", "arbitrary"), vmem_limit_bytes=VMEM_LIMIT),
    )(x, mod3, mod3, norm_g, *extra)


def _tail_kernel(flags_ref, q_ref, k_ref, v_ref, bf_ref, bb_ref, sb_ref, s0_ref, z_ref, g_ref,
                 up_ref, uc_ref, un_ref, zb_ref, ga_ref, gb_ref, x_ref, gate_ref,
                 wpool_ref, pscale_ref, wa_ref, wb_ref, wo_ref, fg_ref,
                 o_ref, state, a_scr, k_rows, bf_rows, bb_rows, *, nblk, seq_len):
    bi = pl.program_id(0)
    i = pl.program_id(1)
    tm = q_ref.shape[1]
    sub = min(tm, SUB_TILE)
    ext = sub + 2 * HALO
    cw = OUT_COLS
    ncol = D_MODEL // cw
    ngrp = len(POOL_WINDOWS)

    @pl.when(i == 0)
    def _():
        state[...] = s0_ref[0]

    exact_path = flags_ref[bi * nblk + i] != 0
    row = lax.broadcasted_iota(jnp.int32, (CHUNK, CHUNK), 0)
    col = lax.broadcasted_iota(jnp.int32, (CHUNK, CHUNK), 1)
    nt_dims = (((1,), (1,)), ((), ()))

    def head_chunk(c, hd, exact):
        rows = slice(c * CHUNK, (c + 1) * CHUNK)
        ks = slice(hd * DK, (hd + 1) * DK)
        vs = slice(hd * DV, (hd + 1) * DV)
        q = q_ref[0, rows, ks].astype(F32)
        k = k_ref[0, rows, ks].astype(F32)
        v = v_ref[0, rows, vs]
        bf = bf_ref[0, rows, ks]
        bb = bb_ref[0, rows, ks]
        qf = (q * jnp.exp(bf)).astype(BF16)
        qb = (q * jnp.exp(bb)).astype(BF16)
        if not exact:
            kf = (k * jnp.exp(-bf)).astype(BF16)
            kb = (k * jnp.exp(-bb)).astype(BF16)
            sc = (jnp.where(col <= row, lax.dot_general(qf, kf, nt_dims,
                                                        preferred_element_type=F32), 0.0)
                  + jnp.where(col >= row, lax.dot_general(qb, kb, nt_dims,
                                                          preferred_element_type=F32), 0.0))
        else:
            k_rows[...] = k
            bf_rows[...] = bf
            bb_rows[...] = bb
            tok = lax.broadcasted_iota(jnp.int32, (CHUNK, DK), 0)

            def col_body(j, acc):
                dec = (jnp.where(tok >= j, jnp.exp(jnp.minimum(bf - bf_rows[pl.ds(j, 1), :], 0.0)), 0.0)
                       + jnp.where(tok <= j, jnp.exp(jnp.minimum(bb - bb_rows[pl.ds(j, 1), :], 0.0)), 0.0))
                w = q * k_rows[pl.ds(j, 1), :] * dec
                return jnp.where(col == j, jnp.sum(w, axis=1, keepdims=True), acc)

            sc = lax.fori_loop(0, CHUNK, col_body, jnp.zeros((CHUNK, CHUNK), F32))
        s_cat = jnp.concatenate([state[hd].astype(BF16), sb_ref[0, c, ks, :]], axis=0)
        q_cat = jnp.concatenate([qf, qb], axis=1)
        o = (jnp.dot(sc.astype(BF16), v, preferred_element_type=F32)
             + jnp.dot(q_cat, s_cat, preferred_element_type=F32))
        _state_step(state, hd, k, bf, v, CHUNK - 1)
        ms = jnp.mean(o * o, axis=-1, keepdims=True)
        on = o * lax.rsqrt(ms + EPS) * g_ref[...]
        a_scr[rows, vs] = (on * z_ref[0, rows, vs].astype(F32)).astype(a_scr.dtype)

    def out_stages(r0):
        rows = slice(r0, r0 + sub)
        st = {"bm": [None] * ngrp, "ya": [None] * ncol, "mg": [None] * ncol, "ss": []}

        def pool_input(cs):
            if r0 == 0:
                before = jnp.where(i > 0, up_ref[0, :, cs].astype(F32), 0.0)
            else:
                before = uc_ref[0, r0 - HALO:r0, cs].astype(F32)
            if r0 + sub == tm:
                after = jnp.where(i < nblk - 1, un_ref[0, :, cs].astype(F32), 0.0)
            else:
                after = uc_ref[0, r0 + sub:r0 + sub + HALO, cs].astype(F32)
            return jnp.concatenate([before, uc_ref[0, rows, cs].astype(F32), after], axis=0)

        def pool(gi):
            def run():
                w = POOL_WINDOWS[gi]
                cs = slice(gi * POOL_GW, (gi + 1) * POOL_GW)
                u_ext = pool_input(cs)
                lead = w // 2 - 1
                acc = pltpu.roll(u_ext, ext - lead, 0) if lead else u_ext
                s = 1
                while s < w:
                    acc = acc + pltpu.roll(acc, s, 0)
                    s *= 2
                t = i * tm + r0 + lax.broadcasted_iota(jnp.int32, (sub, LANES), 0)
                lo = jnp.maximum(t - w // 2, 0)
                hi = jnp.minimum(t + (w - 1 - w // 2), seq_len - 1)
                inv_cnt = 1.0 / (hi - lo + 1).astype(F32)
                inv_cnt = jnp.concatenate([inv_cnt] * (POOL_GW // LANES), axis=1)
                d = acc[HALO:HALO + sub] * inv_cnt - u_ext[HALO:HALO + sub]
                pooled = jnp.dot(d.astype(BF16), wpool_ref[gi], preferred_element_type=F32)
                st["bm"][gi] = (pooled * pscale_ref[:, cs]
                                * zb_ref[0, rows, cs].astype(F32)).astype(BF16)
            return run

        def branch_a(c):
            def run():
                cs = slice(c * cw, (c + 1) * cw)
                y_a = jnp.dot(a_scr[rows, :], wa_ref[:, cs], preferred_element_type=F32)
                st["ya"][c] = ga_ref[0, rows, cs].astype(F32) * y_a
            return run

        def branch_b(c):
            def run():
                cs = slice(c * cw, (c + 1) * cw)
                if c == 0:
                    st["bm"] = jnp.concatenate(st["bm"], axis=1)
                y_b = jnp.dot(st["bm"], wb_ref[:, cs], preferred_element_type=F32)
                st["mg"][c] = (st["ya"][c] + gb_ref[0, rows, cs].astype(F32) * y_b).astype(BF16)
            return run

        def output(c):
            def run():
                cs = slice(c * cw, (c + 1) * cw)
                if c == 0:
                    st["mg"] = jnp.concatenate(st["mg"], axis=1)
                y = jnp.dot(st["mg"], wo_ref[:, cs], preferred_element_type=F32)
                xn = x_ref[0, rows, cs] + gate_ref[0, :, cs] * y
                o_ref[0, rows, cs] = xn
                st["ss"].append(jnp.sum(xn * xn, axis=-1, keepdims=True))
            return run

        def normalise(c):
            def run():
                cs = slice(c * cw, (c + 1) * cw)
                ms = functools.reduce(jnp.add, st["ss"]) * (1.0 / D_MODEL)
                o_ref[0, rows, cs] = o_ref[0, rows, cs] * lax.rsqrt(ms + EPS) * fg_ref[:, cs]
            return run

        first = sorted([((c + 0.5) / ncol, 0, branch_a(c)) for c in range(ncol)]
                       + [((gi + 0.5) / ngrp, 1, pool(gi)) for gi in range(ngrp)],
                       key=lambda item: item[:2])
        return ([[item[2] for item in first]]
                + [[f(c) for c in range(ncol)] for f in (branch_b, output, normalise)])

    def body(exact):
        per_sub = sub // CHUNK
        gla = [[functools.partial(head_chunk, c, hd, exact)
                for c in range(n * per_sub, (n + 1) * per_sub) for hd in range(HEADS)]
               for n in range(tm // sub)]
        tiles = [out_stages(r0) for r0 in range(0, tm, sub)]
        nstage = len(tiles[0])
        for step in gla[0]:
            step()
        for step in range(len(tiles) + nstage - 1):
            lists = [tl[step - n] for n, tl in enumerate(tiles) if 0 <= step - n < nstage]
            if step + 1 < len(gla):
                lists.append(gla[step + 1])
            _interleave(*lists)

    for exact in (False, True):
        @pl.when(exact_path == exact)
        def _():
            body(exact)


def _tail(flags, q, k, v, bf, bb, sb, s0, z, g, ub, zb, ga, gb, x, mod3,
          w_pool, pool_scale, w_a, w_b, w_o, final_g):
    B, L, _ = x.shape
    tm = TAIL_TILE
    nblk = L // tm
    per_halo = tm // HALO
    n_halo = L // HALO
    tok = lambda width: pl.BlockSpec((1, tm, width), lambda bi, i, fl: (bi, i, 0))
    return pl.pallas_call(
        functools.partial(_tail_kernel, nblk=nblk, seq_len=L),
        grid_spec=pltpu.PrefetchScalarGridSpec(
            num_scalar_prefetch=1, grid=(B, nblk),
            in_specs=[
                tok(KW), tok(KW), tok(VW), tok(KW), tok(KW),
                pl.BlockSpec((1, tm // CHUNK, KW, DV), lambda bi, i, fl: (bi, i, 0, 0)),
                pl.BlockSpec((1, HEADS, DK, DV), lambda bi, i, fl: (bi, 0, 0, 0)),
                tok(VW), _resident(g),
                pl.BlockSpec((1, HALO, D_MODEL),
                             lambda bi, i, fl: (bi, jnp.maximum(i * per_halo - 1, 0), 0)),
                tok(D_MODEL),
                pl.BlockSpec((1, HALO, D_MODEL),
                             lambda bi, i, fl: (bi, jnp.minimum((i + 1) * per_halo, n_halo - 1), 0)),
                tok(D_MODEL), tok(D_MODEL), tok(D_MODEL), tok(D_MODEL),
                pl.BlockSpec((1, 1, D_MODEL), lambda bi, i, fl: (bi, 0, 2)),
                _resident(w_pool), _resident(pool_scale), _resident(w_a), _resident(w_b),
                _resident(w_o), _resident(final_g),
            ],
            out_specs=tok(D_MODEL),
            scratch_shapes=[pltpu.VMEM((HEADS, DK, DV), F32), pltpu.VMEM((tm, VW), BF16)]
                           + [pltpu.VMEM((CHUNK, DK), F32)] * 3),
        out_shape=jax.ShapeDtypeStruct((B, L, D_MODEL), F32),
        compiler_params=pltpu.CompilerParams(
            dimension_semantics=("parallel", "arbitrary"), vmem_limit_bytes=VMEM_LIMIT),
    )(flags, q, k, v, bf, bb, sb, s0, z, g, ub, ub, ub, zb, ga, gb, x, mod3,
      w_pool, pool_scale, w_a, w_b, w_o, final_g)


def _rope_tables(seq_len):
    nf = ROT_HALF
    t = np.arange(seq_len)
    freqs = ROPE_BASE ** (-np.arange(nf, dtype=np.float64) / nf)
    rowp, colp = (t // GRID_W)[:, None], (t % GRID_W)[:, None]
    ang = np.concatenate([rowp * freqs] * 2 + [colp * freqs] * 2, axis=1)
    cos, sin = np.cos(ang), np.sin(ang)
    first = (np.arange(LANES) % (2 * nf)) < nf
    sin_up = np.where(first, -sin, 0.0)
    sin_dn = np.where(first, 0.0, sin)
    return jnp.asarray(np.stack([cos, sin_up, sin_dn]).astype(np.float32))


def _chunk_sum_matrices():
    ri = np.arange(SUB_TILE)[:, None]
    ci = np.arange(SUB_TILE)[None, :]
    same = (ri // CHUNK) == (ci // CHUNK)
    mats = [np.tile(same & cond, (1, 2)) for cond in (ci <= ri, ci >= ri)]
    return jnp.asarray(np.stack(mats).astype(np.float32), dtype=BF16)


def _flags(dmax, tiles_per_block):
    worst = jnp.max(dmax[:, 0, 0].reshape(-1, tiles_per_block), axis=1)
    return (worst > SAFE_DECAY).astype(jnp.int32)


def kernel(x, c, ctx, c_ctx, w_mod, b_mod, norm_g, w_in, w_gate_up_f, b_gate_f, w_gate_up_b,
           b_gate_b, gla_norm_g, w_pool, pool_scale, w_branch_a, w_branch_b, w_out, final_norm_g):
    B, L, _ = x.shape
    assert w_mod.shape[0] == 1, "single-layer configuration"
    wi = w_in[0]
    w1 = wi[:, :W_IN_FIRST].astype(BF16)
    w2 = wi[:, W_IN_FIRST + 2 * RANK:].astype(BF16)
    w_lr = jnp.pad(wi[:, W_IN_FIRST:W_IN_FIRST + 2 * RANK],
                   ((0, 0), (0, LANES - 2 * RANK))).astype(BF16)
    w_up = jnp.zeros((LANES, 2 * KW), F32)
    w_up = w_up.at[:RANK, :KW].set(w_gate_up_f[0]).at[RANK:2 * RANK, KW:].set(w_gate_up_b[0])
    w_up = w_up.astype(BF16)
    b_up = jnp.concatenate([b_gate_f[0], b_gate_b[0]])[None, :]

    c_rows = jnp.zeros((MOD_ROWS, D_MODEL), F32).at[:B].set(c).at[B].set(c_ctx)
    mod = _modulation(c_rows, w_mod[0], b_mod[0][None, :])
    mod3 = mod[:, None, :]

    ng = norm_g[0][None, :]
    weights = (w1, w2, w_lr, w_up, b_up, _chunk_sum_matrices())

    *_, s_f, s_b = _project(ctx, mod3, lambda b: B, ng, None, weights, None, context=True)

    q, k, v, za, ub, zb, ga, gb, bf, bb, dm, sb_states = _project(
        x, mod3, lambda b: b, ng, _rope_tables(L), weights, s_b, context=False)
    return _tail(_flags(dm, TAIL_TILE // PROJ_TILE), q, k, v, bf, bb, sb_states, s_f, za,
                 gla_norm_g[0][None, :], ub, zb, ga, gb, x, mod3, w_pool[0].astype(BF16),
                 pool_scale[0][None, :], w_branch_a[0].astype(BF16), w_branch_b[0].astype(BF16),
                 w_out[0].astype(BF16), final_norm_g[None, :])
```

```python
import functools

import numpy as np
import jax
import jax.numpy as jnp
from jax import lax
from jax.experimental import pallas as pl
from jax.experimental.pallas import tpu as pltpu

F32 = jnp.float32
BF16 = jnp.bfloat16

D_MODEL = 1024
HEADS = 4
DK = 128
DV = 256
KW = HEADS * DK
VW = HEADS * DV
RANK = 16
GATE_NORM = 16.0
CHUNK = 256
GRID_W = 64
ROPE_BASE = 10000.0
POOL_WINDOWS = (2, 4, 8, 16)
POOL_GW = D_MODEL // len(POOL_WINDOWS)
EPS = 1e-6

LANES = 128
SUBLANES = 8
V7X_VMEM_BYTES = 64 * 1024 * 1024
ROT_HALF = DK // 4
MOD_ROWS = 16
HALO = 16
PROJ_TILE = 512
TAIL_TILE = 512
SUB_TILE = 256
OUT_COLS = 512
SAFE_DECAY = 60.0
VMEM_LIMIT = V7X_VMEM_BYTES * 7 // 8

_C_Q, _C_K, _C_V, _C_ZA = 0, KW, 2 * KW, 2 * KW + VW
_C_UB, _C_ZB, _C_GA, _C_GB = 0, D_MODEL, 2 * D_MODEL, 3 * D_MODEL
W_IN_FIRST = 2 * KW + 2 * VW


def _sigmoid(x):
    return 0.5 * jnp.tanh(0.5 * x) + 0.5


def _silu(x):
    h = 0.5 * x
    return h + h * jnp.tanh(h)


def _mod_kernel(c_ref, w_ref, b_ref, o_ref):
    c = c_ref[...]
    s = c * jax.nn.sigmoid(c)
    o_ref[...] = jnp.dot(s, w_ref[...], preferred_element_type=F32,
                         precision=lax.Precision.HIGHEST) + b_ref[...]


def _modulation(c_rows, w_mod, b_mod):
    return pl.pallas_call(
        _mod_kernel,
        out_shape=jax.ShapeDtypeStruct((c_rows.shape[0], w_mod.shape[1]), F32),
        compiler_params=pltpu.CompilerParams(vmem_limit_bytes=VMEM_LIMIT),
    )(c_rows, w_mod, b_mod)


def _as_column(row):
    return jnp.broadcast_to(row, (SUBLANES, DK)).T[:, 0:1]


def _state_step(state, hd, k, b, v, edge):
    b_edge = b[edge:edge + 1, :]
    kd_t = (k * jnp.exp(b_edge - b)).T.astype(BF16)
    state[hd] = (jnp.exp(_as_column(b_edge)) * state[hd]
                 + jnp.dot(kd_t, v, preferred_element_type=F32))


def _proj_kernel(x_ref, shift_ref, scale_ref, g_ref, *rest, context):
    if context:
        (w1_ref, wlr_ref, wup_ref, bup_ref, tri_ref, k_ref, v_ref, bf_ref, bb_ref, dmax_ref,
         sf_ref, sb_ref, state_f, state_b) = rest
    else:
        (tab_ref, w1_ref, w2_ref, wlr_ref, wup_ref, bup_ref, tri_ref, s0_ref,
         q_ref, k_ref, v_ref, za_ref, ub_ref, zb_ref, ga_ref, gb_ref,
         bf_ref, bb_ref, dmax_ref, sb_ref, state_b) = rest
    tm = x_ref.shape[1]
    sub = min(tm, SUB_TILE)
    worst = []

    if context:
        state_f[...] = jnp.zeros_like(state_f)
        state_b[...] = jnp.zeros_like(state_b)
    else:
        @pl.when(pl.program_id(1) == 0)
        def _():
            state_b[...] = s0_ref[0]

    def stages(r0):
        rows = slice(r0, r0 + sub)
        st = {}

        def prologue():
            x = x_ref[0, rows, :]
            ms = jnp.mean(x * x, axis=-1, keepdims=True)
            h = (x * lax.rsqrt(ms + EPS)) * g_ref[...]
            h = h * (1.0 + scale_ref[0]) + shift_ref[0]
            st["hb"] = h.astype(BF16)
            lr = jnp.dot(st["hb"], wlr_ref[...], preferred_element_type=F32).astype(BF16)
            gk = jnp.dot(lr, wup_ref[...], preferred_element_type=F32) + bup_ref[...]
            la = (jnp.minimum(gk, 0.0) - jnp.log(1.0 + jnp.exp(-jnp.abs(gk)))) * (1.0 / GATE_NORM)
            hi = la.astype(BF16)
            lo = (la - hi.astype(F32)).astype(BF16)
            pieces = jnp.concatenate([hi, lo], axis=0)
            cf = jnp.dot(tri_ref[0], pieces[:, :KW], preferred_element_type=F32)
            cb = jnp.dot(tri_ref[1], pieces[:, KW:], preferred_element_type=F32)
            bf_ref[0, rows, :] = cf
            bb_ref[0, rows, :] = cb
            for c0 in range(0, sub, CHUNK):
                worst.append(jnp.maximum(-cf[c0 + CHUNK - 1:c0 + CHUNK, :], -cb[c0:c0 + 1, :]))

        def plain(w_ref, c0, dst_ref, act=None):
            def run():
                p = jnp.dot(st["hb"], w_ref[:, c0:c0 + dst_ref.shape[2]],
                            preferred_element_type=F32)
                dst_ref[0, rows, :] = (p if act is None else act(p)).astype(dst_ref.dtype)
            return run

        def rotary(c0, dst_ref, scale):
            def run():
                p = jnp.dot(st["hb"], w1_ref[:, c0:c0 + KW], preferred_element_type=F32)
                cos, sin_up, sin_dn = (tab_ref[n, rows, :] for n in range(3))
                for hd in range(HEADS):
                    ph = p[:, hd * DK:(hd + 1) * DK]
                    out = (ph * cos + pltpu.roll(ph, LANES - ROT_HALF, 1) * sin_up
                           + pltpu.roll(ph, ROT_HALF, 1) * sin_dn)
                    if scale is not None:
                        out = out * scale
                    dst_ref[0, rows, hd * DK:(hd + 1) * DK] = out.astype(dst_ref.dtype)
            return run

        if context:
            return prologue, [plain(w1_ref, _C_K, k_ref), plain(w1_ref, _C_V, v_ref)]
        return prologue, [
            rotary(_C_K, k_ref, None), plain(w1_ref, _C_V, v_ref), rotary(_C_Q, q_ref, DK ** -0.5),
            plain(w1_ref, _C_ZA, za_ref, _silu), plain(w2_ref, _C_UB, ub_ref),
            plain(w2_ref, _C_ZB, zb_ref, _silu), plain(w2_ref, _C_GA, ga_ref, _sigmoid),
            plain(w2_ref, _C_GB, gb_ref, _sigmoid)]

    def scan_steps(r0, state, b_ref, reverse, emit):
        steps = []
        starts = range(r0, r0 + sub, CHUNK)
        for c0 in (reversed(starts) if reverse else starts):
            for hd in range(HEADS):
                def run(c0=c0, hd=hd):
                    rows = slice(c0, c0 + CHUNK)
                    ks = slice(hd * DK, (hd + 1) * DK)
                    if emit:
                        sb_ref[0, c0 // CHUNK, ks, :] = state[hd].astype(BF16)
                    _state_step(state, hd, k_ref[0, rows, ks].astype(F32), b_ref[0, rows, ks],
                                v_ref[0, rows, hd * DV:(hd + 1) * DV], 0 if reverse else CHUNK - 1)
                steps.append(run)
        return steps

    starts = list(reversed(range(0, tm, sub)))
    tiles = [stages(r0) for r0 in starts]
    tiles[0][0]()
    for n, (_, matmuls) in enumerate(tiles):
        for step in matmuls[:2]:
            step()
        fill = scan_steps(starts[n], state_b, bb_ref, True, not context)
        if context:
            fill += scan_steps(starts[n], state_f, bf_ref, False, False)
        if n + 1 < len(tiles):
            fill.append(tiles[n + 1][0])
        _interleave(matmuls[2:], fill)
    dmax_ref[0] = jnp.broadcast_to(
        jnp.max(functools.reduce(jnp.maximum, worst), axis=1, keepdims=True), dmax_ref.shape[1:])
    if context:
        sf_ref[0] = state_f[...]
        sb_ref[0] = state_b[...]


def _interleave(*lists):
    order = sorted(((i + 0.5) / len(lst), which, i)
                   for which, lst in enumerate(lists) for i in range(len(lst)))
    for _, which, idx in order:
        lists[which][idx]()


def _resident(arr):
    return pl.BlockSpec(arr.shape, lambda *_: (0,) * arr.ndim, pipeline_mode=pl.Buffered(1))


def _project(x, mod3, mod_row, norm_g, tables, weights, s0, *, context):
    B, L, _ = x.shape
    tm = min(L, PROJ_TILE)
    nt = L // tm
    tile = (lambda i: i) if context else (lambda i: nt - 1 - i)
    tok = lambda width: pl.BlockSpec((1, tm, width), lambda b, i: (b, tile(i), 0))
    bf = lambda width: jax.ShapeDtypeStruct((B, L, width), BF16)
    st_spec = pl.BlockSpec((1, HEADS, DK, DV), lambda b, i: (b, 0, 0, 0))
    st_shape = jax.ShapeDtypeStruct((B, HEADS, DK, DV), F32)
    state = pltpu.VMEM((HEADS, DK, DV), F32)
    if context:
        assert nt == 1, "context stream is one tile per sample"
        widths = [KW, VW]
        extra = [weights[0]] + list(weights[2:])
        extra_specs = [_resident(w) for w in extra]
        state_specs, state_shapes, scratch = [st_spec, st_spec], [st_shape, st_shape], [state, state]
    else:
        widths = [KW, KW, VW, VW, D_MODEL, D_MODEL, D_MODEL, D_MODEL]
        extra = [tables] + list(weights) + [s0]
        extra_specs = ([pl.BlockSpec((3, tm, LANES), lambda b, i: (0, tile(i), 0))]
                       + [_resident(w) for w in weights] + [st_spec])
        state_specs = [pl.BlockSpec((1, tm // CHUNK, KW, DV), lambda b, i: (b, tile(i), 0, 0))]
        state_shapes = [jax.ShapeDtypeStruct((B, L // CHUNK, KW, DV), BF16)]
        scratch = [state]
    return pl.pallas_call(
        functools.partial(_proj_kernel, context=context),
        grid=(B, nt),
        in_specs=[
            tok(D_MODEL),
            pl.BlockSpec((1, 1, D_MODEL), lambda b, i: (mod_row(b), 0, 0)),
            pl.BlockSpec((1, 1, D_MODEL), lambda b, i: (mod_row(b), 0, 1)),
            _resident(norm_g),
        ] + extra_specs,
        out_specs=[tok(w) for w in widths] + [
            tok(KW), tok(KW),
            pl.BlockSpec((1, SUBLANES, LANES), lambda b, i: (b * nt + tile(i), 0, 0))] + state_specs,
        out_shape=[bf(w) for w in widths] + [
            jax.ShapeDtypeStruct((B, L, KW), F32), jax.ShapeDtypeStruct((B, L, KW), F32),
            jax.ShapeDtypeStruct((B * nt, SUBLANES, LANES), F32)] + state_shapes,
        scratch_shapes=scratch,
        compiler_params=pltpu.CompilerParams(
            dimension_semantics=("parallel", "arbitrary"), vmem_limit_bytes=VMEM_LIMIT),
    )(x, mod3, mod3, norm_g, *extra)


def _tail_kernel(flags_ref, q_ref, k_ref, v_ref, bf_ref, bb_ref, sb_ref, s0_ref, z_ref, g_ref,
                 up_ref, uc_ref, un_ref, zb_ref, ga_ref, gb_ref, x_ref, gate_ref,
                 wpool_ref, pscale_ref, wa_ref, wb_ref, wo_ref, fg_ref,
                 o_ref, state, a_scr, k_rows, bf_rows, bb_rows, *, nblk, seq_len):
    bi = pl.program_id(0)
    i = pl.program_id(1)
    tm = q_ref.shape[1]
    sub = min(tm, SUB_TILE)
    ext = sub + 2 * HALO
    cw = OUT_COLS
    ncol = D_MODEL // cw
    ngrp = len(POOL_WINDOWS)

    @pl.when(i == 0)
    def _():
        state[...] = s0_ref[0]

    exact_path = flags_ref[bi * nblk + i] != 0
    row = lax.broadcasted_iota(jnp.int32, (CHUNK, CHUNK), 0)
    col = lax.broadcasted_iota(jnp.int32, (CHUNK, CHUNK), 1)
    lower, upper = col <= row, col >= row
    nt_dims = (((1,), (1,)), ((), ()))

    def head_chunk(c, hd, exact):
        rows = slice(c * CHUNK, (c + 1) * CHUNK)
        ks = slice(hd * DK, (hd + 1) * DK)
        vs = slice(hd * DV, (hd + 1) * DV)
        q = q_ref[0, rows, ks].astype(F32)
        k = k_ref[0, rows, ks].astype(F32)
        v = v_ref[0, rows, vs]
        bf = bf_ref[0, rows, ks]
        bb = bb_ref[0, rows, ks]
        qf = (q * jnp.exp(bf)).astype(BF16)
        qb = (q * jnp.exp(bb)).astype(BF16)
        if not exact:
            kf = (k * jnp.exp(-bf)).astype(BF16)
            kb = (k * jnp.exp(-bb)).astype(BF16)
            sc = (jnp.where(lower, lax.dot_general(qf, kf, nt_dims,
                                                   preferred_element_type=F32), 0.0)
                  + jnp.where(upper, lax.dot_general(qb, kb, nt_dims,
                                                     preferred_element_type=F32), 0.0))
        else:
            k_rows[...] = k
            bf_rows[...] = bf
            bb_rows[...] = bb
            tok = lax.broadcasted_iota(jnp.int32, (CHUNK, DK), 0)

            def col_body(j, acc):
                dec = (jnp.where(tok >= j, jnp.exp(jnp.minimum(bf - bf_rows[pl.ds(j, 1), :], 0.0)), 0.0)
                       + jnp.where(tok <= j, jnp.exp(jnp.minimum(bb - bb_rows[pl.ds(j, 1), :], 0.0)), 0.0))
                w = q * k_rows[pl.ds(j, 1), :] * dec
                return jnp.where(col == j, jnp.sum(w, axis=1, keepdims=True), acc)

            sc = lax.fori_loop(0, CHUNK, col_body, jnp.zeros((CHUNK, CHUNK), F32))
        s_cat = jnp.concatenate([state[hd].astype(BF16), sb_ref[0, c, ks, :]], axis=0)
        q_cat = jnp.concatenate([qf, qb], axis=1)
        o = (jnp.dot(sc.astype(BF16), v, preferred_element_type=F32)
             + jnp.dot(q_cat, s_cat, preferred_element_type=F32))
        _state_step(state, hd, k, bf, v, CHUNK - 1)
        ms = jnp.mean(o * o, axis=-1, keepdims=True)
        on = o * lax.rsqrt(ms + EPS) * g_ref[...]
        a_scr[rows, vs] = (on * z_ref[0, rows, vs].astype(F32)).astype(a_scr.dtype)

    def out_stages(r0):
        rows = slice(r0, r0 + sub)
        st = {"bm": [None] * ngrp, "ya": [None] * ncol, "mg": [None] * ncol, "ss": []}

        def pool_input(cs):
            if r0 == 0:
                before = jnp.where(i > 0, up_ref[0, :, cs].astype(F32), 0.0)
            else:
                before = uc_ref[0, r0 - HALO:r0, cs].astype(F32)
            if r0 + sub == tm:
                after = jnp.where(i < nblk - 1, un_ref[0, :, cs].astype(F32), 0.0)
            else:
                after = uc_ref[0, r0 + sub:r0 + sub + HALO, cs].astype(F32)
            return jnp.concatenate([before, uc_ref[0, rows, cs].astype(F32), after], axis=0)

        def pool(gi):
            def run():
                w = POOL_WINDOWS[gi]
                cs = slice(gi * POOL_GW, (gi + 1) * POOL_GW)
                u_ext = pool_input(cs)
                lead = w // 2 - 1
                acc = pltpu.roll(u_ext, ext - lead, 0) if lead else u_ext
                s = 1
                while s < w:
                    acc = acc + pltpu.roll(acc, s, 0)
                    s *= 2
                def inv_rows(first):
                    t = i * tm + r0 + first + lax.broadcasted_iota(jnp.int32, (SUBLANES, LANES), 0)
                    lo = jnp.maximum(t - w // 2, 0)
                    hi = jnp.minimum(t + (w - 1 - w // 2), seq_len - 1)
                    return 1.0 / (hi - lo + 1).astype(F32)

                inv_cnt = jnp.concatenate(
                    [inv_rows(0), jnp.full((sub - 2 * SUBLANES, LANES), 1.0 / w, F32),
                     inv_rows(sub - SUBLANES)], axis=0)
                inv_cnt = jnp.concatenate([inv_cnt] * (POOL_GW // LANES), axis=1)
                d = acc[HALO:HALO + sub] * inv_cnt - u_ext[HALO:HALO + sub]
                pooled = jnp.dot(d.astype(BF16), wpool_ref[gi], preferred_element_type=F32)
                st["bm"][gi] = (pooled * pscale_ref[:, cs]
                                * zb_ref[0, rows, cs].astype(F32)).astype(BF16)
            return run

        def branch_a(c):
            def run():
                cs = slice(c * cw, (c + 1) * cw)
                y_a = jnp.dot(a_scr[rows, :], wa_ref[:, cs], preferred_element_type=F32)
                st["ya"][c] = ga_ref[0, rows, cs].astype(F32) * y_a
            return run

        def branch_b(c):
            def run():
                cs = slice(c * cw, (c + 1) * cw)
                if c == 0:
                    st["bm"] = jnp.concatenate(st["bm"], axis=1)
                y_b = jnp.dot(st["bm"], wb_ref[:, cs], preferred_element_type=F32)
                st["mg"][c] = (st["ya"][c] + gb_ref[0, rows, cs].astype(F32) * y_b).astype(BF16)
            return run

        def output(c):
            def run():
                cs = slice(c * cw, (c + 1) * cw)
                if c == 0:
                    st["mg"] = jnp.concatenate(st["mg"], axis=1)
                y = jnp.dot(st["mg"], wo_ref[:, cs], preferred_element_type=F32)
                xn = x_ref[0, rows, cs] + gate_ref[0, :, cs] * y
                o_ref[0, rows, cs] = xn
                st["ss"].append(jnp.sum(xn * xn, axis=-1, keepdims=True))
            return run

        def normalise(c):
            def run():
                cs = slice(c * cw, (c + 1) * cw)
                ms = functools.reduce(jnp.add, st["ss"]) * (1.0 / D_MODEL)
                o_ref[0, rows, cs] = o_ref[0, rows, cs] * lax.rsqrt(ms + EPS) * fg_ref[:, cs]
            return run

        first = sorted([((c + 0.5) / ncol, 0, branch_a(c)) for c in range(ncol)]
                       + [((gi + 0.5) / ngrp, 1, pool(gi)) for gi in range(ngrp)],
                       key=lambda item: item[:2])
        return ([[item[2] for item in first]]
                + [[f(c) for c in range(ncol)] for f in (branch_b, output, normalise)])

    def body(exact):
        per_sub = sub // CHUNK
        gla = [[functools.partial(head_chunk, c, hd, exact)
                for c in range(n * per_sub, (n + 1) * per_sub) for hd in range(HEADS)]
               for n in range(tm // sub)]
        tiles = [out_stages(r0) for r0 in range(0, tm, sub)]
        nstage = len(tiles[0])
        for step in gla[0]:
            step()
        for step in range(len(tiles) + nstage - 1):
            lists = [tl[step - n] for n, tl in enumerate(tiles) if 0 <= step - n < nstage]
            if step + 1 < len(gla):
                lists.append(gla[step + 1])
            _interleave(*lists)

    for exact in (False, True):
        @pl.when(exact_path == exact)
        def _():
            body(exact)


def _tail(flags, q, k, v, bf, bb, sb, s0, z, g, ub, zb, ga, gb, x, mod3,
          w_pool, pool_scale, w_a, w_b, w_o, final_g):
    B, L, _ = x.shape
    tm = TAIL_TILE
    nblk = L // tm
    per_halo = tm // HALO
    n_halo = L // HALO
    tok = lambda width: pl.BlockSpec((1, tm, width), lambda bi, i, fl: (bi, i, 0))
    return pl.pallas_call(
        functools.partial(_tail_kernel, nblk=nblk, seq_len=L),
        grid_spec=pltpu.PrefetchScalarGridSpec(
            num_scalar_prefetch=1, grid=(B, nblk),
            in_specs=[
                tok(KW), tok(KW), tok(VW), tok(KW), tok(KW),
                pl.BlockSpec((1, tm // CHUNK, KW, DV), lambda bi, i, fl: (bi, i, 0, 0)),
                pl.BlockSpec((1, HEADS, DK, DV), lambda bi, i, fl: (bi, 0, 0, 0)),
                tok(VW), _resident(g),
                pl.BlockSpec((1, HALO, D_MODEL),
                             lambda bi, i, fl: (bi, jnp.maximum(i * per_halo - 1, 0), 0)),
                tok(D_MODEL),
                pl.BlockSpec((1, HALO, D_MODEL),
                             lambda bi, i, fl: (bi, jnp.minimum((i + 1) * per_halo, n_halo - 1), 0)),
                tok(D_MODEL), tok(D_MODEL), tok(D_MODEL), tok(D_MODEL),
                pl.BlockSpec((1, 1, D_MODEL), lambda bi, i, fl: (bi, 0, 2)),
                _resident(w_pool), _resident(pool_scale), _resident(w_a), _resident(w_b),
                _resident(w_o), _resident(final_g),
            ],
            out_specs=tok(D_MODEL),
            scratch_shapes=[pltpu.VMEM((HEADS, DK, DV), F32), pltpu.VMEM((tm, VW), BF16)]
                           + [pltpu.VMEM((CHUNK, DK), F32)] * 3),
        out_shape=jax.ShapeDtypeStruct((B, L, D_MODEL), F32),
        compiler_params=pltpu.CompilerParams(
            dimension_semantics=("parallel", "arbitrary"), vmem_limit_bytes=VMEM_LIMIT),
    )(flags, q, k, v, bf, bb, sb, s0, z, g, ub, ub, ub, zb, ga, gb, x, mod3,
      w_pool, pool_scale, w_a, w_b, w_o, final_g)


def _rope_tables(seq_len):
    nf = ROT_HALF
    t = np.arange(seq_len)
    freqs = ROPE_BASE ** (-np.arange(nf, dtype=np.float64) / nf)
    rowp, colp = (t // GRID_W)[:, None], (t % GRID_W)[:, None]
    ang = np.concatenate([rowp * freqs] * 2 + [colp * freqs] * 2, axis=1)
    cos, sin = np.cos(ang), np.sin(ang)
    first = (np.arange(LANES) % (2 * nf)) < nf
    sin_up = np.where(first, -sin, 0.0)
    sin_dn = np.where(first, 0.0, sin)
    return jnp.asarray(np.stack([cos, sin_up, sin_dn]).astype(np.float32))


def _chunk_sum_matrices():
    ri = np.arange(SUB_TILE)[:, None]
    ci = np.arange(SUB_TILE)[None, :]
    same = (ri // CHUNK) == (ci // CHUNK)
    mats = [np.tile(same & cond, (1, 2)) for cond in (ci <= ri, ci >= ri)]
    return jnp.asarray(np.stack(mats).astype(np.float32), dtype=BF16)


def _flags(dmax, tiles_per_block):
    worst = jnp.max(dmax[:, 0, 0].reshape(-1, tiles_per_block), axis=1)
    return (worst > SAFE_DECAY).astype(jnp.int32)


def kernel(x, c, ctx, c_ctx, w_mod, b_mod, norm_g, w_in, w_gate_up_f, b_gate_f, w_gate_up_b,
           b_gate_b, gla_norm_g, w_pool, pool_scale, w_branch_a, w_branch_b, w_out, final_norm_g):
    B, L, _ = x.shape
    assert w_mod.shape[0] == 1, "single-layer configuration"
    wi = w_in[0]
    w1 = wi[:, :W_IN_FIRST].astype(BF16)
    w2 = wi[:, W_IN_FIRST + 2 * RANK:].astype(BF16)
    w_lr = jnp.pad(wi[:, W_IN_FIRST:W_IN_FIRST + 2 * RANK],
                   ((0, 0), (0, LANES - 2 * RANK))).astype(BF16)
    w_up = jnp.zeros((LANES, 2 * KW), F32)
    w_up = w_up.at[:RANK, :KW].set(w_gate_up_f[0]).at[RANK:2 * RANK, KW:].set(w_gate_up_b[0])
    w_up = w_up.astype(BF16)
    b_up = jnp.concatenate([b_gate_f[0], b_gate_b[0]])[None, :]

    c_rows = jnp.zeros((MOD_ROWS, D_MODEL), F32).at[:B].set(c).at[B].set(c_ctx)
    mod = _modulation(c_rows, w_mod[0], b_mod[0][None, :])
    mod3 = mod[:, None, :]

    ng = norm_g[0][None, :]
    weights = (w1, w2, w_lr, w_up, b_up, _chunk_sum_matrices())

    *_, s_f, s_b = _project(ctx, mod3, lambda b: B, ng, None, weights, None, context=True)

    q, k, v, za, ub, zb, ga, gb, bf, bb, dm, sb_states = _project(
        x, mod3, lambda b: b, ng, _rope_tables(L), weights, s_b, context=False)
    return _tail(_flags(dm, TAIL_TILE // PROJ_TILE), q, k, v, bf, bb, sb_states, s_f, za,
                 gla_norm_g[0][None, :], ub, zb, ga, gb, x, mod3, w_pool[0].astype(BF16),
                 pool_scale[0][None, :], w_branch_a[0].astype(BF16), w_branch_b[0].astype(BF16),
                 w_out[0].astype(BF16), final_norm_g[None, :])
```

```python
import functools

import numpy as np
import jax
import jax.numpy as jnp
from jax import lax
from jax.experimental import pallas as pl
from jax.experimental.pallas import tpu as pltpu

F32 = jnp.float32
BF16 = jnp.bfloat16

D_MODEL = 1024
HEADS = 4
DK = 128
DV = 256
KW = HEADS * DK
VW = HEADS * DV
RANK = 16
GATE_NORM = 16.0
CHUNK = 256
GRID_W = 64
ROPE_BASE = 10000.0
POOL_WINDOWS = (2, 4, 8, 16)
POOL_GW = D_MODEL // len(POOL_WINDOWS)
EPS = 1e-6

LANES = 128
SUBLANES = 8
V7X_VMEM_BYTES = 64 * 1024 * 1024
ROT_HALF = DK // 4
MOD_ROWS = 16
HALO = 16
PROJ_TILE = 512
TAIL_TILE = 512
SUB_TILE = 256
OUT_COLS = 512
SAFE_DECAY = 60.0
VMEM_LIMIT = V7X_VMEM_BYTES * 7 // 8

_C_Q, _C_K, _C_V, _C_ZA = 0, KW, 2 * KW, 2 * KW + VW
_C_UB, _C_ZB, _C_GA, _C_GB = 0, D_MODEL, 2 * D_MODEL, 3 * D_MODEL
W_IN_FIRST = 2 * KW + 2 * VW


def _sigmoid(x):
    return 0.5 * jnp.tanh(0.5 * x) + 0.5


def _silu(x):
    h = 0.5 * x
    return h + h * jnp.tanh(h)


def _mod_kernel(c_ref, w_ref, b_ref, o_ref):
    c = c_ref[...]
    s = c * jax.nn.sigmoid(c)
    o_ref[...] = jnp.dot(s, w_ref[...], preferred_element_type=F32,
                         precision=lax.Precision.HIGHEST) + b_ref[...]


def _modulation(c_rows, w_mod, b_mod):
    return pl.pallas_call(
        _mod_kernel,
        out_shape=jax.ShapeDtypeStruct((c_rows.shape[0], w_mod.shape[1]), F32),
        compiler_params=pltpu.CompilerParams(vmem_limit_bytes=VMEM_LIMIT),
    )(c_rows, w_mod, b_mod)


def _as_column(row):
    return jnp.broadcast_to(row, (SUBLANES, DK)).T[:, 0:1]


def _state_step(state, hd, k, b, v, edge):
    b_edge = b[edge:edge + 1, :]
    kd_t = (k * jnp.exp(b_edge - b)).T.astype(BF16)
    state[hd] = (jnp.exp(_as_column(b_edge)) * state[hd]
                 + jnp.dot(kd_t, v, preferred_element_type=F32))


def _proj_kernel(x_ref, shift_ref, scale_ref, g_ref, *rest, context):
    if context:
        (w1_ref, wlr_ref, wup_ref, bup_ref, tri_ref, k_ref, v_ref, bf_ref, bb_ref, dmax_ref,
         sf_ref, sb_ref, state_f, state_b) = rest
    else:
        (tab_ref, w1_ref, w2_ref, wlr_ref, wup_ref, bup_ref, tri_ref, s0_ref,
         q_ref, k_ref, v_ref, za_ref, ub_ref, zb_ref, ga_ref, gb_ref,
         bf_ref, bb_ref, dmax_ref, sb_ref, state_b) = rest
    tm = x_ref.shape[1]
    sub = min(tm, SUB_TILE)
    worst = []

    if context:
        state_f[...] = jnp.zeros_like(state_f)
        state_b[...] = jnp.zeros_like(state_b)
    else:
        @pl.when(pl.program_id(1) == 0)
        def _():
            state_b[...] = s0_ref[0]

    def stages(r0):
        rows = slice(r0, r0 + sub)
        st = {}

        def prologue():
            x = x_ref[0, rows, :]
            ms = jnp.mean(x * x, axis=-1, keepdims=True)
            h = (x * lax.rsqrt(ms + EPS)) * g_ref[...]
            h = h * (1.0 + scale_ref[0]) + shift_ref[0]
            st["hb"] = h.astype(BF16)
            lr = jnp.dot(st["hb"], wlr_ref[...], preferred_element_type=F32).astype(BF16)
            gk = jnp.dot(lr, wup_ref[...], preferred_element_type=F32) + bup_ref[...]
            la = (jnp.minimum(gk, 0.0) - jnp.log(1.0 + jnp.exp(-jnp.abs(gk)))) * (1.0 / GATE_NORM)
            hi = la.astype(BF16)
            lo = (la - hi.astype(F32)).astype(BF16)
            pieces = jnp.concatenate([hi, lo], axis=0)
            cf = jnp.dot(tri_ref[0], pieces[:, :KW], preferred_element_type=F32)
            cb = jnp.dot(tri_ref[1], pieces[:, KW:], preferred_element_type=F32)
            bf_ref[0, rows, :] = cf
            bb_ref[0, rows, :] = cb
            for c0 in range(0, sub, CHUNK):
                worst.append(jnp.maximum(-cf[c0 + CHUNK - 1:c0 + CHUNK, :], -cb[c0:c0 + 1, :]))

        def plain(w_ref, c0, dst_ref, act=None):
            def run():
                p = jnp.dot(st["hb"], w_ref[:, c0:c0 + dst_ref.shape[2]],
                            preferred_element_type=F32)
                dst_ref[0, rows, :] = (p if act is None else act(p)).astype(dst_ref.dtype)
            return run

        def rotary(c0, dst_ref, scale):
            def run():
                p = jnp.dot(st["hb"], w1_ref[:, c0:c0 + KW], preferred_element_type=F32)
                cos, sin_up, sin_dn = (tab_ref[n, rows, :] for n in range(3))
                for hd in range(HEADS):
                    ph = p[:, hd * DK:(hd + 1) * DK]
                    out = (ph * cos + pltpu.roll(ph, LANES - ROT_HALF, 1) * sin_up
                           + pltpu.roll(ph, ROT_HALF, 1) * sin_dn)
                    if scale is not None:
                        out = out * scale
                    dst_ref[0, rows, hd * DK:(hd + 1) * DK] = out.astype(dst_ref.dtype)
            return run

        if context:
            return prologue, [plain(w1_ref, _C_K, k_ref), plain(w1_ref, _C_V, v_ref)]
        return prologue, [
            rotary(_C_K, k_ref, None), plain(w1_ref, _C_V, v_ref), rotary(_C_Q, q_ref, DK ** -0.5),
            plain(w1_ref, _C_ZA, za_ref, _silu), plain(w2_ref, _C_UB, ub_ref),
            plain(w2_ref, _C_ZB, zb_ref, _silu), plain(w2_ref, _C_GA, ga_ref, _sigmoid),
            plain(w2_ref, _C_GB, gb_ref, _sigmoid)]

    def scan_steps(r0, state, b_ref, reverse, emit):
        steps = []
        starts = range(r0, r0 + sub, CHUNK)
        for c0 in (reversed(starts) if reverse else starts):
            for hd in range(HEADS):
                def run(c0=c0, hd=hd):
                    rows = slice(c0, c0 + CHUNK)
                    ks = slice(hd * DK, (hd + 1) * DK)
                    if emit:
                        sb_ref[0, c0 // CHUNK, ks, :] = state[hd].astype(BF16)
                    _state_step(state, hd, k_ref[0, rows, ks].astype(F32), b_ref[0, rows, ks],
                                v_ref[0, rows, hd * DV:(hd + 1) * DV], 0 if reverse else CHUNK - 1)
                steps.append(run)
        return steps

    starts = list(reversed(range(0, tm, sub)))
    tiles = [stages(r0) for r0 in starts]
    tiles[0][0]()
    for n, (_, matmuls) in enumerate(tiles):
        for step in matmuls[:2]:
            step()
        fill = scan_steps(starts[n], state_b, bb_ref, True, not context)
        if context:
            fill += scan_steps(starts[n], state_f, bf_ref, False, False)
        if n + 1 < len(tiles):
            fill.append(tiles[n + 1][0])
        _interleave(matmuls[2:], fill)
    dmax_ref[0] = jnp.broadcast_to(
        jnp.max(functools.reduce(jnp.maximum, worst), axis=1, keepdims=True), dmax_ref.shape[1:])
    if context:
        sf_ref[0] = state_f[...]
        sb_ref[0] = state_b[...]


def _interleave(*lists):
    order = sorted(((i + 0.5) / len(lst), which, i)
                   for which, lst in enumerate(lists) for i in range(len(lst)))
    for _, which, idx in order:
        lists[which][idx]()


def _resident(arr):
    return pl.BlockSpec(arr.shape, lambda *_: (0,) * arr.ndim, pipeline_mode=pl.Buffered(1))


def _project(x, mod3, mod_row, norm_g, tables, weights, s0, *, context):
    B, L, _ = x.shape
    tm = min(L, PROJ_TILE)
    nt = L // tm
    tile = (lambda i: i) if context else (lambda i: nt - 1 - i)
    tok = lambda width: pl.BlockSpec((1, tm, width), lambda b, i: (b, tile(i), 0))
    bf = lambda width: jax.ShapeDtypeStruct((B, L, width), BF16)
    st_spec = pl.BlockSpec((1, HEADS, DK, DV), lambda b, i: (b, 0, 0, 0))
    st_shape = jax.ShapeDtypeStruct((B, HEADS, DK, DV), F32)
    state = pltpu.VMEM((HEADS, DK, DV), F32)
    if context:
        assert nt == 1, "context stream is one tile per sample"
        widths = [KW, VW]
        extra = [weights[0]] + list(weights[2:])
        extra_specs = [_resident(w) for w in extra]
        state_specs, state_shapes, scratch = [st_spec, st_spec], [st_shape, st_shape], [state, state]
    else:
        widths = [KW, KW, VW, VW, D_MODEL, D_MODEL, D_MODEL, D_MODEL]
        extra = [tables] + list(weights) + [s0]
        extra_specs = ([pl.BlockSpec((3, tm, LANES), lambda b, i: (0, tile(i), 0))]
                       + [_resident(w) for w in weights] + [st_spec])
        state_specs = [pl.BlockSpec((1, tm // CHUNK, KW, DV), lambda b, i: (b, tile(i), 0, 0))]
        state_shapes = [jax.ShapeDtypeStruct((B, L // CHUNK, KW, DV), BF16)]
        scratch = [state]
    return pl.pallas_call(
        functools.partial(_proj_kernel, context=context),
        grid=(B, nt),
        in_specs=[
            tok(D_MODEL),
            pl.BlockSpec((1, 1, D_MODEL), lambda b, i: (mod_row(b), 0, 0)),
            pl.BlockSpec((1, 1, D_MODEL), lambda b, i: (mod_row(b), 0, 1)),
            _resident(norm_g),
        ] + extra_specs,
        out_specs=[tok(w) for w in widths] + [
            tok(KW), tok(KW),
            pl.BlockSpec((1, SUBLANES, LANES), lambda b, i: (b * nt + tile(i), 0, 0))] + state_specs,
        out_shape=[bf(w) for w in widths] + [
            jax.ShapeDtypeStruct((B, L, KW), F32), jax.ShapeDtypeStruct((B, L, KW), F32),
            jax.ShapeDtypeStruct((B * nt, SUBLANES, LANES), F32)] + state_shapes,
        scratch_shapes=scratch,
        compiler_params=pltpu.CompilerParams(
            dimension_semantics=("parallel", "arbitrary"), vmem_limit_bytes=VMEM_LIMIT,
            allow_input_fusion=[False] * 4 + [any(a is w for w in weights[:4]) for a in extra]),
    )(x, mod3, mod3, norm_g, *extra)


def _tail_kernel(flags_ref, q_ref, k_ref, v_ref, bf_ref, bb_ref, sb_ref, s0_ref, z_ref, g_ref,
                 up_ref, uc_ref, un_ref, zb_ref, ga_ref, gb_ref, x_ref, gate_ref,
                 wpool_ref, pscale_ref, wa_ref, wb_ref, wo_ref, fg_ref,
                 o_ref, state, a_scr, k_rows, bf_rows, bb_rows, *, nblk, seq_len):
    bi = pl.program_id(0)
    i = pl.program_id(1)
    tm = q_ref.shape[1]
    sub = min(tm, SUB_TILE)
    ext = sub + 2 * HALO
    cw = OUT_COLS
    ncol = D_MODEL // cw
    ngrp = len(POOL_WINDOWS)

    @pl.when(i == 0)
    def _():
        state[...] = s0_ref[0]

    exact_path = flags_ref[bi * nblk + i] != 0
    row = lax.broadcasted_iota(jnp.int32, (CHUNK, CHUNK), 0)
    col = lax.broadcasted_iota(jnp.int32, (CHUNK, CHUNK), 1)
    lower, upper = col <= row, col >= row
    nt_dims = (((1,), (1,)), ((), ()))

    def head_chunk(c, hd, exact):
        rows = slice(c * CHUNK, (c + 1) * CHUNK)
        ks = slice(hd * DK, (hd + 1) * DK)
        vs = slice(hd * DV, (hd + 1) * DV)
        q = q_ref[0, rows, ks].astype(F32)
        k = k_ref[0, rows, ks].astype(F32)
        v = v_ref[0, rows, vs]
        bf = bf_ref[0, rows, ks]
        bb = bb_ref[0, rows, ks]
        qf = (q * jnp.exp(bf)).astype(BF16)
        qb = (q * jnp.exp(bb)).astype(BF16)
        if not exact:
            kf = (k * jnp.exp(-bf)).astype(BF16)
            kb = (k * jnp.exp(-bb)).astype(BF16)
            sc = (jnp.where(lower, lax.dot_general(qf, kf, nt_dims,
                                                   preferred_element_type=F32), 0.0)
                  + jnp.where(upper, lax.dot_general(qb, kb, nt_dims,
                                                     preferred_element_type=F32), 0.0))
        else:
            k_rows[...] = k
            bf_rows[...] = bf
            bb_rows[...] = bb
            tok = lax.broadcasted_iota(jnp.int32, (CHUNK, DK), 0)

            def col_body(j, acc):
                dec = (jnp.where(tok >= j, jnp.exp(jnp.minimum(bf - bf_rows[pl.ds(j, 1), :], 0.0)), 0.0)
                       + jnp.where(tok <= j, jnp.exp(jnp.minimum(bb - bb_rows[pl.ds(j, 1), :], 0.0)), 0.0))
                w = q * k_rows[pl.ds(j, 1), :] * dec
                return jnp.where(col == j, jnp.sum(w, axis=1, keepdims=True), acc)

            sc = lax.fori_loop(0, CHUNK, col_body, jnp.zeros((CHUNK, CHUNK), F32))
        s_cat = jnp.concatenate([state[hd].astype(BF16), sb_ref[0, c, ks, :]], axis=0)
        q_cat = jnp.concatenate([qf, qb], axis=1)
        o = (jnp.dot(sc.astype(BF16), v, preferred_element_type=F32)
             + jnp.dot(q_cat, s_cat, preferred_element_type=F32))
        _state_step(state, hd, k, bf, v, CHUNK - 1)
        ms = jnp.mean(o * o, axis=-1, keepdims=True)
        on = o * lax.rsqrt(ms + EPS) * g_ref[...]
        a_scr[rows, vs] = (on * z_ref[0, rows, vs].astype(F32)).astype(a_scr.dtype)

    def out_stages(r0):
        rows = slice(r0, r0 + sub)
        st = {"bm": [None] * ngrp, "ya": [None] * ncol, "mg": [None] * ncol, "ss": []}

        def pool_input(cs):
            if r0 == 0:
                before = jnp.where(i > 0, up_ref[0, :, cs].astype(F32), 0.0)
            else:
                before = uc_ref[0, r0 - HALO:r0, cs].astype(F32)
            if r0 + sub == tm:
                after = jnp.where(i < nblk - 1, un_ref[0, :, cs].astype(F32), 0.0)
            else:
                after = uc_ref[0, r0 + sub:r0 + sub + HALO, cs].astype(F32)
            return jnp.concatenate([before, uc_ref[0, rows, cs].astype(F32), after], axis=0)

        def pool(gi):
            def run():
                w = POOL_WINDOWS[gi]
                cs = slice(gi * POOL_GW, (gi + 1) * POOL_GW)
                u_ext = pool_input(cs)
                lead = w // 2 - 1
                acc = pltpu.roll(u_ext, ext - lead, 0) if lead else u_ext
                s = 1
                while s < w:
                    acc = acc + pltpu.roll(acc, s, 0)
                    s *= 2
                def inv_rows(first):
                    t = i * tm + r0 + first + lax.broadcasted_iota(jnp.int32, (SUBLANES, LANES), 0)
                    lo = jnp.maximum(t - w // 2, 0)
                    hi = jnp.minimum(t + (w - 1 - w // 2), seq_len - 1)
                    return 1.0 / (hi - lo + 1).astype(F32)

                inv_cnt = jnp.concatenate(
                    [inv_rows(0), jnp.full((sub - 2 * SUBLANES, LANES), 1.0 / w, F32),
                     inv_rows(sub - SUBLANES)], axis=0)
                inv_cnt = jnp.concatenate([inv_cnt] * (POOL_GW // LANES), axis=1)
                d = acc[HALO:HALO + sub] * inv_cnt - u_ext[HALO:HALO + sub]
                pooled = jnp.dot(d.astype(BF16), wpool_ref[gi], preferred_element_type=F32)
                st["bm"][gi] = (pooled * pscale_ref[:, cs]
                                * zb_ref[0, rows, cs].astype(F32)).astype(BF16)
            return run

        def branch_a(c):
            def run():
                cs = slice(c * cw, (c + 1) * cw)
                y_a = jnp.dot(a_scr[rows, :], wa_ref[:, cs], preferred_element_type=F32)
                st["ya"][c] = ga_ref[0, rows, cs].astype(F32) * y_a
            return run

        def branch_b(c):
            def run():
                cs = slice(c * cw, (c + 1) * cw)
                if c == 0:
                    st["bm"] = jnp.concatenate(st["bm"], axis=1)
                y_b = jnp.dot(st["bm"], wb_ref[:, cs], preferred_element_type=F32)
                st["mg"][c] = (st["ya"][c] + gb_ref[0, rows, cs].astype(F32) * y_b).astype(BF16)
            return run

        def output(c):
            def run():
                cs = slice(c * cw, (c + 1) * cw)
                if c == 0:
                    st["mg"] = jnp.concatenate(st["mg"], axis=1)
                y = jnp.dot(st["mg"], wo_ref[:, cs], preferred_element_type=F32)
                xn = x_ref[0, rows, cs] + gate_ref[0, :, cs] * y
                o_ref[0, rows, cs] = xn
                st["ss"].append(jnp.sum(xn * xn, axis=-1, keepdims=True))
            return run

        def normalise(c):
            def run():
                cs = slice(c * cw, (c + 1) * cw)
                ms = functools.reduce(jnp.add, st["ss"]) * (1.0 / D_MODEL)
                o_ref[0, rows, cs] = o_ref[0, rows, cs] * lax.rsqrt(ms + EPS) * fg_ref[:, cs]
            return run

        first = sorted([((c + 0.5) / ncol, 0, branch_a(c)) for c in range(ncol)]
                       + [((gi + 0.5) / ngrp, 1, pool(gi)) for gi in range(ngrp)],
                       key=lambda item: item[:2])
        return ([[item[2] for item in first]]
                + [[f(c) for c in range(ncol)] for f in (branch_b, output, normalise)])

    def body(exact):
        per_sub = sub // CHUNK
        gla = [[functools.partial(head_chunk, c, hd, exact)
                for c in range(n * per_sub, (n + 1) * per_sub) for hd in range(HEADS)]
               for n in range(tm // sub)]
        tiles = [out_stages(r0) for r0 in range(0, tm, sub)]
        nstage = len(tiles[0])
        for step in gla[0]:
            step()
        for step in range(len(tiles) + nstage - 1):
            lists = [tl[step - n] for n, tl in enumerate(tiles) if 0 <= step - n < nstage]
            if step + 1 < len(gla):
                lists.append(gla[step + 1])
            _interleave(*lists)

    for exact in (False, True):
        @pl.when(exact_path == exact)
        def _():
            body(exact)


def _tail(flags, q, k, v, bf, bb, sb, s0, z, g, ub, zb, ga, gb, x, mod3,
          w_pool, pool_scale, w_a, w_b, w_o, final_g):
    B, L, _ = x.shape
    tm = TAIL_TILE
    nblk = L // tm
    per_halo = tm // HALO
    n_halo = L // HALO
    tok = lambda width: pl.BlockSpec((1, tm, width), lambda bi, i, fl: (bi, i, 0))
    return pl.pallas_call(
        functools.partial(_tail_kernel, nblk=nblk, seq_len=L),
        grid_spec=pltpu.PrefetchScalarGridSpec(
            num_scalar_prefetch=1, grid=(B, nblk),
            in_specs=[
                tok(KW), tok(KW), tok(VW), tok(KW), tok(KW),
                pl.BlockSpec((1, tm // CHUNK, KW, DV), lambda bi, i, fl: (bi, i, 0, 0)),
                pl.BlockSpec((1, HEADS, DK, DV), lambda bi, i, fl: (bi, 0, 0, 0)),
                tok(VW), _resident(g),
                pl.BlockSpec((1, HALO, D_MODEL),
                             lambda bi, i, fl: (bi, jnp.maximum(i * per_halo - 1, 0), 0)),
                tok(D_MODEL),
                pl.BlockSpec((1, HALO, D_MODEL),
                             lambda bi, i, fl: (bi, jnp.minimum((i + 1) * per_halo, n_halo - 1), 0)),
                tok(D_MODEL), tok(D_MODEL), tok(D_MODEL), tok(D_MODEL),
                pl.BlockSpec((1, 1, D_MODEL), lambda bi, i, fl: (bi, 0, 2)),
                _resident(w_pool), _resident(pool_scale), _resident(w_a), _resident(w_b),
                _resident(w_o), _resident(final_g),
            ],
            out_specs=tok(D_MODEL),
            scratch_shapes=[pltpu.VMEM((HEADS, DK, DV), F32), pltpu.VMEM((tm, VW), BF16)]
                           + [pltpu.VMEM((CHUNK, DK), F32)] * 3),
        out_shape=jax.ShapeDtypeStruct((B, L, D_MODEL), F32),
        compiler_params=pltpu.CompilerParams(
            dimension_semantics=("parallel", "arbitrary"), vmem_limit_bytes=VMEM_LIMIT,
            allow_input_fusion=[False] * 18 + [True, False, True, True, True, False]),
    )(flags, q, k, v, bf, bb, sb, s0, z, g, ub, ub, ub, zb, ga, gb, x, mod3,
      w_pool, pool_scale, w_a, w_b, w_o, final_g)


def _rope_tables(seq_len):
    nf = ROT_HALF
    t = np.arange(seq_len)
    freqs = ROPE_BASE ** (-np.arange(nf, dtype=np.float64) / nf)
    rowp, colp = (t // GRID_W)[:, None], (t % GRID_W)[:, None]
    ang = np.concatenate([rowp * freqs] * 2 + [colp * freqs] * 2, axis=1)
    cos, sin = np.cos(ang), np.sin(ang)
    first = (np.arange(LANES) % (2 * nf)) < nf
    sin_up = np.where(first, -sin, 0.0)
    sin_dn = np.where(first, 0.0, sin)
    return jnp.asarray(np.stack([cos, sin_up, sin_dn]).astype(np.float32))


def _chunk_sum_matrices():
    ri = np.arange(SUB_TILE)[:, None]
    ci = np.arange(SUB_TILE)[None, :]
    same = (ri // CHUNK) == (ci // CHUNK)
    mats = [np.tile(same & cond, (1, 2)) for cond in (ci <= ri, ci >= ri)]
    return jnp.asarray(np.stack(mats).astype(np.float32), dtype=BF16)


def _flags(dmax, tiles_per_block):
    worst = jnp.max(dmax[:, 0, 0].reshape(-1, tiles_per_block), axis=1)
    return (worst > SAFE_DECAY).astype(jnp.int32)


def kernel(x, c, ctx, c_ctx, w_mod, b_mod, norm_g, w_in, w_gate_up_f, b_gate_f, w_gate_up_b,
           b_gate_b, gla_norm_g, w_pool, pool_scale, w_branch_a, w_branch_b, w_out, final_norm_g):
    B, L, _ = x.shape
    assert w_mod.shape[0] == 1, "single-layer configuration"
    wi = w_in[0]
    w1 = wi[:, :W_IN_FIRST].astype(BF16)
    w2 = wi[:, W_IN_FIRST + 2 * RANK:].astype(BF16)
    w_lr = jnp.pad(wi[:, W_IN_FIRST:W_IN_FIRST + 2 * RANK],
                   ((0, 0), (0, LANES - 2 * RANK))).astype(BF16)
    w_up = jnp.zeros((LANES, 2 * KW), F32)
    w_up = w_up.at[:RANK, :KW].set(w_gate_up_f[0]).at[RANK:2 * RANK, KW:].set(w_gate_up_b[0])
    w_up = w_up.astype(BF16)
    b_up = jnp.concatenate([b_gate_f[0], b_gate_b[0]])[None, :]

    c_rows = jnp.zeros((MOD_ROWS, D_MODEL), F32).at[:B].set(c).at[B].set(c_ctx)
    mod = _modulation(c_rows, w_mod[0], b_mod[0][None, :])
    mod3 = mod[:, None, :]

    ng = norm_g[0][None, :]
    weights = (w1, w2, w_lr, w_up, b_up, _chunk_sum_matrices())

    *_, s_f, s_b = _project(ctx, mod3, lambda b: B, ng, None, weights, None, context=True)

    q, k, v, za, ub, zb, ga, gb, bf, bb, dm, sb_states = _project(
        x, mod3, lambda b: b, ng, _rope_tables(L), weights, s_b, context=False)
    return _tail(_flags(dm, TAIL_TILE // PROJ_TILE), q, k, v, bf, bb, sb_states, s_f, za,
                 gla_norm_g[0][None, :], ub, zb, ga, gb, x, mod3, w_pool[0].astype(BF16),
                 pool_scale[0][None, :], w_branch_a[0].astype(BF16), w_branch_b[0].astype(BF16),
                 w_out[0].astype(BF16), final_norm_g[None, :])
```
